```python
import math
import jax, jax.numpy as jnp
from jax import lax
import numpy as np

D_MODEL = 1024
BATCH = 4
SEQ = 4096
DEPTH = 1

SSD_EXPAND = 2
D_INNER = SSD_EXPAND * D_MODEL
SSD_HEAD_DIM = 64
SSD_HEADS = D_INNER // SSD_HEAD_DIM
SSD_GROUPS = 4
SSD_STATE = 128
SSD_CONV = 4
SSD_CHUNK = 256
CONV_DIM = D_INNER + 2 * SSD_GROUPS * SSD_STATE

DIFF_HEADS = 8
DIFF_HEAD_DIM = 64
DIFF_QK = DIFF_HEADS * 2 * DIFF_HEAD_DIM
DIFF_V = DIFF_HEADS * 2 * DIFF_HEAD_DIM
Q_BLOCK = 128
ROPE_THETA = 10000.0

D_FF = -(-8 * D_MODEL // (3 * 256)) * 256

N_BRANCH = 2
EPS = 1e-6

IN_SIZES = (D_INNER, CONV_DIM, SSD_HEADS, DIFF_QK, DIFF_QK, DIFF_V, N_BRANCH * D_MODEL)
IN_SPLITS = tuple(int(v) for v in np.cumsum(IN_SIZES)[:-1])
D_IN_PROJ = sum(IN_SIZES)

kernel_name = 'hybrid_ssd_diffattn_gated_block'


def rmsnorm(x, w):
    xf = x.astype(jnp.float32)
    y = xf * lax.rsqrt(jnp.mean(xf * xf, axis=-1, keepdims=True) + EPS)
    return (y * w.astype(jnp.float32)).astype(x.dtype)


def rope_tables(positions):
    inv_freq = 1.0 / (ROPE_THETA ** (jnp.arange(0, DIFF_HEAD_DIM, 2, dtype=jnp.float32) / DIFF_HEAD_DIM))
    ang = positions.astype(jnp.float32)[..., None] * inv_freq
    return jnp.cos(ang), jnp.sin(ang)


def apply_rope(x, cos, sin):
    cos = cos[:, :, None, None, :]
    sin = sin[:, :, None, None, :]
    xf = x.astype(jnp.float32)
    x1, x2 = jnp.split(xf, 2, axis=-1)
    out = jnp.concatenate([x1 * cos - x2 * sin, x1 * sin + x2 * cos], axis=-1)
    return out.astype(x.dtype)


def causal_depthwise_conv(u, w, b):
    rhs = jnp.transpose(w)[:, None, :].astype(u.dtype)
    out = lax.conv_general_dilated(u, rhs, window_strides=(1,), padding=[(SSD_CONV - 1, 0)],
                                   dimension_numbers=('NWC', 'WIO', 'NWC'),
                                   feature_group_count=u.shape[-1])
    return out + b.astype(u.dtype)


def ssd_chunked_scan(x, dt, A, Bm, Cm):
    b, s, h, p = x.shape
    g, n = Bm.shape[2], Bm.shape[3]
    hg = h // g
    L = SSD_CHUNK
    pad = (-s) % L
    xf = x.astype(jnp.float32) * dt[..., None]
    a = dt * A
    Bf = Bm.astype(jnp.float32)
    Cf = Cm.astype(jnp.float32)
    if pad:
        xf = jnp.pad(xf, ((0, 0), (0, pad), (0, 0), (0, 0)))
        a = jnp.pad(a, ((0, 0), (0, pad), (0, 0)))
        Bf = jnp.pad(Bf, ((0, 0), (0, pad), (0, 0), (0, 0)))
        Cf = jnp.pad(Cf, ((0, 0), (0, pad), (0, 0), (0, 0)))
    nc = (s + pad) // L
    xc = xf.reshape(b, nc, L, g, hg, p)
    a_cs = jnp.cumsum(a.reshape(b, nc, L, g, hg), axis=2)
    Bc = Bf.reshape(b, nc, L, g, n)
    Cc = Cf.reshape(b, nc, L, g, n)
    mask = jnp.tril(jnp.ones((L, L), dtype=bool))[:, :, None, None]
    seg = a_cs[:, :, :, None] - a_cs[:, :, None, :]
    decay = jnp.exp(jnp.where(mask, seg, -jnp.inf))
    scores = jnp.einsum('bclgn,bcsgn->bclsg', Cc, Bc)
    y_diag = jnp.einsum('bclsg,bclsgh,bcsghp->bclghp', scores, decay, xc)
    decay_to_end = jnp.exp(a_cs[:, :, -1:] - a_cs)
    states = jnp.einsum('bclgn,bclgh,bclghp->bcghpn', Bc, decay_to_end, xc)
    chunk_decay = jnp.exp(a_cs[:, :, -1])

    def step(carry, inp):
        st, dec = inp
        return carry * dec[..., None, None] + st, carry

    init = jnp.zeros((b, g, hg, p, n), jnp.float32)
    _, prev = lax.scan(step, init, (jnp.moveaxis(states, 1, 0), jnp.moveaxis(chunk_decay, 1, 0)))
    prev = jnp.moveaxis(prev, 0, 1)
    y_off = jnp.einsum('bclgn,bcghpn,bclgh->bclghp', Cc, prev, jnp.exp(a_cs))
    y = (y_diag + y_off).reshape(b, nc * L, h, p)
    return y[:, :s]


def diff_attention(q, k, v, lam):
    b, s, h, _, d = q.shape
    nb = s // Q_BLOCK
    qb = q.reshape(b, nb, Q_BLOCK, h, 2, d).transpose(1, 0, 3, 4, 2, 5)
    kt = k.transpose(0, 2, 3, 1, 4)
    vt = v.transpose(0, 2, 1, 3)
    key_idx = jnp.arange(s)
    scale = d ** -0.5

    def block(args):
        qblk, i = args
        sc = jnp.einsum('bhcqd,bhckd->bhcqk', qblk, kt).astype(jnp.float32) * scale
        q_idx = i * Q_BLOCK + jnp.arange(Q_BLOCK)
        causal = key_idx[None, :] <= q_idx[:, None]
        pr = jax.nn.softmax(jnp.where(causal, sc, -jnp.inf), axis=-1)
        attn = pr[:, :, 0] - lam * pr[:, :, 1]
        return jnp.einsum('bhqk,bhkv->bhqv', attn.astype(v.dtype), vt)

    out = lax.map(block, (qb, jnp.arange(nb)))
    return out.transpose(1, 0, 3, 2, 4).reshape(b, s, h, 2 * d)


def hybrid_mixer(h, cos, sin, lambda_init, w_in, conv_w, conv_b, dt_bias, a_log, d_skip, ssd_norm, w_o_ssd,
                 lambda_q1, lambda_k1, lambda_q2, lambda_k2, subln, w_o_attn, w_out):
    b, s, _ = h.shape
    proj = h @ w_in
    z, xbc, dt_raw, q, k, v, gate_logits = jnp.split(proj, IN_SPLITS, axis=-1)

    xbc = jax.nn.silu(causal_depthwise_conv(xbc, conv_w, conv_b))
    xs, Bm, Cm = jnp.split(xbc, [D_INNER, D_INNER + SSD_GROUPS * SSD_STATE], axis=-1)
    xs = xs.reshape(b, s, SSD_HEADS, SSD_HEAD_DIM)
    Bm = Bm.reshape(b, s, SSD_GROUPS, SSD_STATE)
    Cm = Cm.reshape(b, s, SSD_GROUPS, SSD_STATE)
    dt = jax.nn.softplus(dt_raw.astype(jnp.float32) + dt_bias.astype(jnp.float32))
    A = -jnp.exp(a_log.astype(jnp.float32))
    y = ssd_chunked_scan(xs, dt, A, Bm, Cm) + xs.astype(jnp.float32) * d_skip.astype(jnp.float32)[:, None]
    y = y.reshape(b, s, D_INNER) * jax.nn.silu(z.astype(jnp.float32))
    y = rmsnorm(y.reshape(b, s, SSD_GROUPS, D_INNER // SSD_GROUPS), ssd_norm.reshape(SSD_GROUPS, -1))
    y_ssd = y.reshape(b, s, D_INNER).astype(h.dtype) @ w_o_ssd

    q = apply_rope(q.reshape(b, s, DIFF_HEADS, 2, DIFF_HEAD_DIM), cos, sin)
    k = apply_rope(k.reshape(b, s, DIFF_HEADS, 2, DIFF_HEAD_DIM), cos, sin)
    v = v.reshape(b, s, DIFF_HEADS, 2 * DIFF_HEAD_DIM)
    lam = (jnp.exp(jnp.sum(lambda_q1.astype(jnp.float32) * lambda_k1.astype(jnp.float32)))
           - jnp.exp(jnp.sum(lambda_q2.astype(jnp.float32) * lambda_k2.astype(jnp.float32))) + lambda_init)
    o = diff_attention(q, k, v, lam)
    o = rmsnorm(o, subln) * (1.0 - lambda_init)
    y_attn = o.reshape(b, s, DIFF_V) @ w_o_attn

    g_ssd, g_attn = jnp.split(jax.nn.sigmoid(gate_logits), N_BRANCH, axis=-1)
    merged = g_ssd * y_ssd + g_attn * y_attn
    return merged @ w_out


def setup_inputs(seed: int = 0) -> dict:
    key = jax.random.key(seed)
    ks = jax.random.split(key, 32)
    f32 = jnp.float32
    nrm = lambda k, shape, fan: jax.random.normal(k, shape, f32) * fan ** -0.5
    gain = lambda k, shape: 1.0 + 0.02 * jax.random.normal(k, shape, f32)
    x = jax.random.normal(ks[0], (BATCH, SEQ, D_MODEL), f32)
    c = jax.random.normal(ks[1], (BATCH, D_MODEL), f32)
    offset = jax.random.randint(ks[2], (BATCH, 1), 0, 1024, dtype=jnp.int32)
    positions = offset + jnp.arange(SEQ, dtype=jnp.int32)[None, :]
    dt_init = jnp.exp(jax.random.uniform(ks[3], (DEPTH, SSD_HEADS), f32, math.log(1e-3), math.log(1e-1)))
    dt_bias = dt_init + jnp.log(-jnp.expm1(-dt_init))
    a_log = jnp.log(jax.random.uniform(ks[4], (DEPTH, SSD_HEADS), f32, 1.0, 16.0))
    return {
        'x': x, 'c': c, 'positions': positions,
        'w_ada': 0.5 * nrm(ks[5], (DEPTH, D_MODEL, 6 * D_MODEL), D_MODEL),
        'b_ada': 0.01 * jax.random.normal(ks[6], (DEPTH, 6 * D_MODEL), f32),
        'norm_pre_mix': gain(ks[7], (DEPTH, D_MODEL)),
        'norm_post_mix': gain(ks[8], (DEPTH, D_MODEL)),
        'norm_pre_ffn': gain(ks[9], (DEPTH, D_MODEL)),
        'norm_post_ffn': gain(ks[10], (DEPTH, D_MODEL)),
        'w_in': nrm(ks[11], (DEPTH, D_MODEL, D_IN_PROJ), D_MODEL),
        'conv_w': nrm(ks[12], (DEPTH, CONV_DIM, SSD_CONV), SSD_CONV),
        'conv_b': 0.01 * jax.random.normal(ks[13], (DEPTH, CONV_DIM), f32),
        'dt_bias': dt_bias,
        'a_log': a_log,
        'd_skip': gain(ks[14], (DEPTH, SSD_HEADS)),
        'ssd_norm': gain(ks[15], (DEPTH, D_INNER)),
        'w_o_ssd': nrm(ks[16], (DEPTH, D_INNER, D_MODEL), D_INNER),
        'lambda_q1': 0.1 * jax.random.normal(ks[17], (DEPTH, DIFF_HEAD_DIM), f32),
        'lambda_k1': 0.1 * jax.random.normal(ks[18], (DEPTH, DIFF_HEAD_DIM), f32),
        'lambda_q2': 0.1 * jax.random.normal(ks[19], (DEPTH, DIFF_HEAD_DIM), f32),
        'lambda_k2': 0.1 * jax.random.normal(ks[20], (DEPTH, DIFF_HEAD_DIM), f32),
        'subln': gain(ks[21], (DEPTH, 2 * DIFF_HEAD_DIM)),
        'w_o_attn': nrm(ks[22], (DEPTH, DIFF_V, D_MODEL), DIFF_V),
        'w_out': nrm(ks[23], (DEPTH, D_MODEL, D_MODEL), D_MODEL),
        'w_gate': nrm(ks[24], (DEPTH, D_MODEL, D_FF), D_MODEL),
        'w_up': nrm(ks[25], (DEPTH, D_MODEL, D_FF), D_MODEL),
        'w_down': nrm(ks[26], (DEPTH, D_FF, D_MODEL), D_FF),
    }


def reference(x, c, positions, w_ada, b_ada, norm_pre_mix, norm_post_mix, norm_pre_ffn, norm_post_ffn,
              w_in, conv_w, conv_b, dt_bias, a_log, d_skip, ssd_norm, w_o_ssd,
              lambda_q1, lambda_k1, lambda_q2, lambda_k2, subln, w_o_attn, w_out,
              w_gate, w_up, w_down):
    cos, sin = rope_tables(positions)
    cond = jax.nn.silu(c)
    for layer in range(DEPTH):
        lambda_init = 0.8 - 0.6 * math.exp(-0.3 * layer)
        mod = (cond @ w_ada[layer] + b_ada[layer])[:, None, :]
        sh1, sc1, g1, sh2, sc2, g2 = jnp.split(mod, 6, axis=-1)
        h = rmsnorm(x, norm_pre_mix[layer]) * (1.0 + sc1) + sh1
        mix = hybrid_mixer(h, cos, sin, lambda_init, w_in[layer], conv_w[layer], conv_b[layer], dt_bias[layer],
                           a_log[layer], d_skip[layer], ssd_norm[layer], w_o_ssd[layer],
                           lambda_q1[layer], lambda_k1[layer], lambda_q2[layer], lambda_k2[layer],
                           subln[layer], w_o_attn[layer], w_out[layer])
        x = x + g1 * rmsnorm(mix, norm_post_mix[layer])
        h = rmsnorm(x, norm_pre_ffn[layer]) * (1.0 + sc2) + sh2
        f = (jax.nn.silu(h @ w_gate[layer]) * (h @ w_up[layer])) @ w_down[layer]
        x = x + g2 * rmsnorm(f, norm_post_ffn[layer])
    return x
```

```python
import functools
import math

import jax
import jax.numpy as jnp
import numpy as np
from jax import lax
from jax.experimental import pallas as pl
from jax.experimental.pallas import tpu as pltpu

F32 = jnp.float32
BF16 = jnp.bfloat16

D_MODEL = 1024
D_INNER = 2048
SSD_HEAD_DIM = 64
SSD_HEADS = 32
SSD_GROUPS = 4
SSD_STATE = 128
SSD_CONV = 4
SSD_CHUNK = 256
CONV_DIM = D_INNER + 2 * SSD_GROUPS * SSD_STATE
GROUP_CH = D_INNER // SSD_GROUPS
DIFF_HEADS = 8
DIFF_HEAD_DIM = 64
DIFF_QK = 1024
DIFF_V = 1024
ROPE_THETA = 10000.0
D_FF = 2816
EPS = 1e-6
IN_SIZES = (D_INNER, CONV_DIM, SSD_HEADS, DIFF_QK, DIFF_QK, DIFF_V, 2 * D_MODEL)
IN_SPLITS = tuple(int(v) for v in np.cumsum(IN_SIZES)[:-1])

LANES = 128
HEAD_W = 2 * DIFF_HEAD_DIM

XBC_OFF, Q_OFF, Z_OFF, GATE_OFF, K_OFF, V_OFF = 0, 3072, 4096, 6144, 8192, 9216
PROJ_W = 10240
DT_W = LANES

VMEM_LIMIT = 56 * 1024 * 1024

Q_SCALE = DIFF_HEAD_DIM ** -0.5 * math.log2(math.e)
NEG_BIG = -1e30


def _sigmoid(v):
    return 1.0 / (1.0 + jnp.exp(-v))


def _silu(v):
    return v * _sigmoid(v)


def _rms(v, w):
    return v * lax.rsqrt(jnp.mean(v * v, axis=-1, keepdims=True) + EPS) * w


def _params(sem):
    return pltpu.CompilerParams(dimension_semantics=sem, vmem_limit_bytes=VMEM_LIMIT)


def _const_spec(shape):
    nd = len(shape)
    return pl.BlockSpec(shape, lambda *_: (0,) * nd, pipeline_mode=pl.Buffered(1))


def _ada_kernel(c_ref, w_ref, b_ref, o_ref):
    cond = _silu(c_ref[...])
    o_ref[...] = jnp.dot(cond.astype(BF16), w_ref[...].astype(BF16), preferred_element_type=F32) + b_ref[...]


def _ada(c_pad, w_ada, b_ada):
    n = w_ada.shape[1]
    tn = 1024
    return pl.pallas_call(
        _ada_kernel,
        out_shape=jax.ShapeDtypeStruct((c_pad.shape[0], n), F32),
        grid=(n // tn,),
        in_specs=[pl.BlockSpec(c_pad.shape, lambda j: (0, 0)),
                  pl.BlockSpec((D_MODEL, tn), lambda j: (0, j)),
                  pl.BlockSpec((1, tn), lambda j: (0, j))],
        out_specs=pl.BlockSpec((c_pad.shape[0], tn), lambda j: (0, j)),
        compiler_params=_params(("arbitrary",)),
        name="ada",
    )(c_pad, w_ada, b_ada)


def _inproj_kernel(x_ref, mod_ref, nw_ref, w_ref, wdt_ref, out_ref, dt_ref, h_ref):
    @pl.when(pl.program_id(1) == 0)
    def _():
        mod = mod_ref[0]
        h = _rms(x_ref[...], nw_ref[...]) * (1.0 + mod[:, D_MODEL:2 * D_MODEL]) + mod[:, 0:D_MODEL]
        hb = h.astype(BF16)
        h_ref[...] = hb
        dt_ref[...] = jnp.dot(hb, wdt_ref[...], preferred_element_type=F32)

    out_ref[...] = jnp.dot(h_ref[...], w_ref[...], preferred_element_type=F32).astype(BF16)


def _in_proj(x2, mod3, nw, w_wide, w_dt, seq):
    t = x2.shape[0]
    tm, tn = 1024, 2048
    per_b = seq // tm
    return pl.pallas_call(
        _inproj_kernel,
        out_shape=(jax.ShapeDtypeStruct((t, PROJ_W), BF16), jax.ShapeDtypeStruct((t, DT_W), F32)),
        grid=(t // tm, PROJ_W // tn),
        in_specs=[pl.BlockSpec((tm, D_MODEL), lambda i, j: (i, 0)),
                  pl.BlockSpec((1, 1, 6 * D_MODEL), lambda i, j: (i // per_b, 0, 0)),
                  pl.BlockSpec((1, D_MODEL), lambda i, j: (0, 0)),
                  pl.BlockSpec((D_MODEL, tn), lambda i, j: (0, j)),
                  pl.BlockSpec((D_MODEL, DT_W), lambda i, j: (0, 0))],
        out_specs=(pl.BlockSpec((tm, tn), lambda i, j: (i, j)),
                   pl.BlockSpec((tm, DT_W), lambda i, j: (i, 0))),
        scratch_shapes=[pltpu.VMEM((tm, D_MODEL), BF16)],
        compiler_params=_params(("parallel", "arbitrary")),
        name="in_proj",
    )(x2, mod3, nw, w_wide, w_dt)


def _rope_kernel(q_ref, k_ref, pos_ref, invf_ref, sgn_ref, qo_ref, ko_ref):
    ang = pos_ref[...].astype(F32) * invf_ref[...]
    cos = jnp.cos(ang)
    sin = jnp.sin(ang) * sgn_ref[...]
    for h in range(DIFF_HEADS):
        sl = slice(h * HEAD_W, (h + 1) * HEAD_W)
        xq = q_ref[:, sl].astype(F32)
        xk = k_ref[:, sl].astype(F32)
        qo_ref[:, sl] = ((xq * cos + pltpu.roll(xq, 64, 1) * sin) * Q_SCALE).astype(BF16)
        ko_ref[:, sl] = (xk * cos + pltpu.roll(xk, 64, 1) * sin).astype(BF16)


def _rope(proj, pos2, invf, sgn):
    t = proj.shape[0]
    tm = 1024
    return pl.pallas_call(
        _rope_kernel,
        out_shape=(jax.ShapeDtypeStruct((t, DIFF_QK), BF16), jax.ShapeDtypeStruct((t, DIFF_QK), BF16)),
        grid=(t // tm,),
        in_specs=[pl.BlockSpec((tm, DIFF_QK), lambda i: (i, Q_OFF // DIFF_QK)),
                  pl.BlockSpec((tm, DIFF_QK), lambda i: (i, K_OFF // DIFF_QK)),
                  pl.BlockSpec((tm, 1), lambda i: (i, 0)),
                  _const_spec((1, LANES)), _const_spec((1, LANES))],
        out_specs=(pl.BlockSpec((tm, DIFF_QK), lambda i: (i, 0)),
                   pl.BlockSpec((tm, DIFF_QK), lambda i: (i, 0))),
        compiler_params=_params(("parallel",)),
        name="rope",
    )(proj, proj, pos2, invf, sgn)


def _split2(v):
    hi = v.astype(BF16)
    lo = (v - hi.astype(F32)).astype(BF16)
    return hi, lo


def _split3(v):
    hi = v.astype(BF16)
    r = v - hi.astype(F32)
    mid = r.astype(BF16)
    lo = (r - mid.astype(F32)).astype(BF16)
    return hi, mid, lo


def _dot(a, b):
    return jnp.dot(a, b, preferred_element_type=F32)


def _ssd_kernel(xbc_ref, z_ref, dt_ref, convw_ref, convb_ref, dtb_ref, alog_ref, dskip_ref, normw_ref,
                expand_ref, tril_ref, y_ref, xpad_ref, act_ref, yacc_ref, state_ref):
    L = SSD_CHUNK
    halo = 8

    @pl.when(pl.program_id(1) == 0)
    def _():
        xpad_ref[0:halo, :] = jnp.zeros((halo, CONV_DIM), F32)
        state_ref[...] = jnp.zeros(state_ref.shape, F32)

    xpad_ref[halo:halo + L, :] = xbc_ref[...].astype(F32)
    conv = convb_ref[...]
    for k in range(SSD_CONV):
        off = halo - (SSD_CONV - 1) + k
        conv = conv + convw_ref[k:k + 1, :] * xpad_ref[off:off + L, :]
    xpad_ref[0:halo, :] = xpad_ref[L:L + halo, :]
    act_ref[...] = _silu(conv)

    xs = act_ref[:, 0:D_INNER]
    expand = expand_ref[...]

    dt = dt_ref[...] + dtb_ref[...]
    dt = jnp.maximum(dt, 0.0) + jnp.log1p(jnp.exp(-jnp.abs(dt)))
    a = dt * (-jnp.exp(alog_ref[...]))
    tril = tril_ref[...]
    cs = sum(_dot(tril, p) for p in _split3(a))
    cs_t = cs.T
    cs_last = cs[L - 1:L, :]
    ecs = jnp.exp(cs)
    dte = jnp.exp(cs_last - cs)

    dt_e = sum(_dot(p, expand) for p in _split2(dt))
    ecs_e = sum(_dot(p, expand) for p in _split2(ecs))
    dte_e = _dot(dte.astype(BF16), expand)
    cd_e = ecs_e[L - 1:L, :]

    xdt = xs * dt_e
    xw = (xdt * dte_e).astype(BF16)

    row = lax.broadcasted_iota(jnp.int32, (L, L), 0)
    col = lax.broadcasted_iota(jnp.int32, (L, L), 1)
    causal = col <= row
    lane = lax.broadcasted_iota(jnp.int32, (L, LANES), 1)
    first_head = lane < SSD_HEAD_DIM

    for g in range(SSD_GROUPS):
        gs = slice(g * GROUP_CH, (g + 1) * GROUP_CH)
        b_g = act_ref[:, D_INNER + g * SSD_STATE:D_INNER + (g + 1) * SSD_STATE]
        c_g = act_ref[:, D_INNER + (SSD_GROUPS + g) * SSD_STATE:
                      D_INNER + (SSD_GROUPS + g + 1) * SSD_STATE].astype(BF16)
        b_gt = b_g.T.astype(BF16)
        scores = _dot(c_g, b_gt)
        st = state_ref[g]
        y_off = _dot(c_g, st.astype(BF16)) * ecs_e[:, gs]
        state_ref[g] = st * cd_e[:, gs] + _dot(b_gt, xw[:, gs])
        for pair in range(GROUP_CH // LANES):
            h0 = g * (SSD_HEADS // SSD_GROUPS) + 2 * pair
            ps = slice(g * GROUP_CH + pair * LANES, g * GROUP_CH + (pair + 1) * LANES)
            xp = xdt[:, ps].astype(BF16)
            zero = jnp.zeros_like(xp)
            y_pair = None
            for sub, rhs in ((0, jnp.where(first_head, xp, zero)), (1, jnp.where(first_head, zero, xp))):
                h = h0 + sub
                seg = cs[:, h:h + 1] - cs_t[h:h + 1, :]
                m = (scores * jnp.exp(jnp.where(causal, seg, NEG_BIG))).astype(BF16)
                part = _dot(m, rhs)
                y_pair = part if y_pair is None else y_pair + part
            yacc_ref[:, ps] = (y_pair + y_off[:, pair * LANES:(pair + 1) * LANES]
                               + xs[:, ps] * dskip_ref[:, ps])

    for g in range(SSD_GROUPS):
        gs = slice(g * GROUP_CH, (g + 1) * GROUP_CH)
        yg = yacc_ref[:, gs] * _silu(z_ref[:, gs].astype(F32))
        y_ref[:, gs] = _rms(yg, normw_ref[:, gs]).astype(y_ref.dtype)


def _ssd(proj, dt_raw, convw_t, convb, dtb, alog, dskip_e, normw, expand, tril, batch, seq):
    t = proj.shape[0]
    L = SSD_CHUNK
    nc = seq // L
    row = lambda b, c: b * nc + c
    return pl.pallas_call(
        _ssd_kernel,
        out_shape=jax.ShapeDtypeStruct((t, D_INNER), BF16),
        grid=(batch, nc),
        in_specs=[pl.BlockSpec((L, CONV_DIM), lambda b, c: (row(b, c), XBC_OFF // CONV_DIM)),
                  pl.BlockSpec((L, D_INNER), lambda b, c: (row(b, c), Z_OFF // D_INNER)),
                  pl.BlockSpec((L, DT_W), lambda b, c: (row(b, c), 0)),
                  _const_spec((SSD_CONV, CONV_DIM)), _const_spec((1, CONV_DIM)),
                  _const_spec((1, DT_W)), _const_spec((1, DT_W)),
                  _const_spec((1, D_INNER)), _const_spec((1, D_INNER)),
                  _const_spec((DT_W, D_INNER)), _const_spec((L, L))],
        out_specs=pl.BlockSpec((L, D_INNER), lambda b, c: (row(b, c), 0)),
        scratch_shapes=[pltpu.VMEM((L + 8, CONV_DIM), F32),
                        pltpu.VMEM((L, CONV_DIM), F32),
                        pltpu.VMEM((L, D_INNER), F32),
                        pltpu.VMEM((SSD_GROUPS, SSD_STATE, GROUP_CH), F32)],
        compiler_params=_params(("parallel", "arbitrary")),
        name="ssd",
    )(proj, proj, dt_raw, convw_t, convb, dtb, alog, dskip_e, normw, expand, tril)


def _attn_kernel(q_ref, k_ref, v_ref, lq1_ref, lk1_ref, lq2_ref, lk2_ref, subln_ref, o_ref,
                 q2_ref, m_ref, l_ref, acc_ref, *, tq, tk, lambda_init):
    qi = pl.program_id(2)
    q = q_ref[...]
    lane = lax.broadcasted_iota(jnp.int32, (tq, HEAD_W), 1)
    comp0 = (lane % DIFF_HEAD_DIM) < (DIFF_HEAD_DIM // 2)
    zero = jnp.zeros_like(q)
    q2_ref[0:tq, :] = jnp.where(comp0, q, zero)
    q2_ref[tq:2 * tq, :] = jnp.where(comp0, zero, q)
    m_ref[...] = jnp.full(m_ref.shape, NEG_BIG, F32)
    l_ref[...] = jnp.zeros(l_ref.shape, F32)
    acc_ref[...] = jnp.zeros(acc_ref.shape, F32)

    def step(j, masked):
        start = pl.multiple_of(j * tk, tk)
        k = k_ref[pl.ds(start, tk), :]
        v = v_ref[pl.ds(start, tk), :]
        s = lax.dot_general(q2_ref[...], k, (((1,), (1,)), ((), ())), preferred_element_type=F32)
        if masked:
            qpos = qi * tq + lax.broadcasted_iota(jnp.int32, (2 * tq, tk), 0) % tq
            kpos = start + lax.broadcasted_iota(jnp.int32, (2 * tq, tk), 1)
            s = jnp.where(kpos <= qpos, s, NEG_BIG)
        m_prev = m_ref[...]
        m_new = jnp.maximum(m_prev, jnp.max(s, axis=-1, keepdims=True))
        alpha = jnp.exp2(m_prev - m_new)
        p = jnp.exp2(s - m_new)
        l_ref[...] = alpha * l_ref[...] + jnp.sum(p, axis=-1, keepdims=True)
        acc_ref[...] = alpha * acc_ref[...] + jnp.dot(p.astype(BF16), v, preferred_element_type=F32)
        m_ref[...] = m_new

    per_q = tq // tk
    lax.fori_loop(0, qi * per_q, lambda j, carry: (step(j, False), carry)[1], 0)
    for d in range(per_q):
        step(qi * per_q + d, True)

    lam = (jnp.exp(jnp.sum(lq1_ref[...] * lk1_ref[...], axis=-1, keepdims=True))
           - jnp.exp(jnp.sum(lq2_ref[...] * lk2_ref[...], axis=-1, keepdims=True)) + lambda_init)
    o = acc_ref[0:tq, :] / l_ref[0:tq, :] - lam * (acc_ref[tq:2 * tq, :] / l_ref[tq:2 * tq, :])
    o_ref[...] = (_rms(o, subln_ref[...]) * (1.0 - lambda_init)).astype(o_ref.dtype)


def _attn(q_r, k_r, proj, lq1, lk1, lq2, lk2, subln, batch, seq, lambda_init):
    t = q_r.shape[0]
    tq, tk = 512, 512
    nq = seq // tq
    lam_spec = _const_spec((1, DIFF_HEAD_DIM))
    return pl.pallas_call(
        functools.partial(_attn_kernel, tq=tq, tk=tk, lambda_init=lambda_init),
        out_shape=jax.ShapeDtypeStruct((t, DIFF_V), BF16),
        grid=(batch, DIFF_HEADS, nq),
        in_specs=[pl.BlockSpec((tq, HEAD_W), lambda b, h, i: (b * nq + i, h)),
                  pl.BlockSpec((seq, HEAD_W), lambda b, h, i: (b, h)),
                  pl.BlockSpec((seq, HEAD_W), lambda b, h, i: (b, V_OFF // HEAD_W + h)),
                  lam_spec, lam_spec, lam_spec, lam_spec, _const_spec((1, HEAD_W))],
        out_specs=pl.BlockSpec((tq, HEAD_W), lambda b, h, i: (b * nq + i, h)),
        scratch_shapes=[pltpu.VMEM((2 * tq, HEAD_W), BF16),
                        pltpu.VMEM((2 * tq, 1), F32),
                        pltpu.VMEM((2 * tq, 1), F32),
                        pltpu.VMEM((2 * tq, HEAD_W), F32)],
        compiler_params=_params(("parallel", "parallel", "arbitrary")),
        name="diff_attn",
    )(q_r, k_r, proj, lq1, lk1, lq2, lk2, subln)


def _merge_kernel(y_ref, o_ref, gate_ref, x_ref, mod_ref, wos_ref, woa_ref, wout_ref, nw_ref, out_ref):
    y_ssd = _dot(y_ref[...], wos_ref[...])
    y_attn = _dot(o_ref[...], woa_ref[...])
    g_ssd = _sigmoid(gate_ref[:, 0:D_MODEL].astype(F32))
    g_attn = _sigmoid(gate_ref[:, D_MODEL:2 * D_MODEL].astype(F32))
    merged = (g_ssd * y_ssd + g_attn * y_attn).astype(BF16)
    mix = _dot(merged, wout_ref[...])
    g1 = mod_ref[0][:, 2 * D_MODEL:3 * D_MODEL]
    out_ref[...] = x_ref[...] + g1 * _rms(mix, nw_ref[...])


def _merge(y, o, proj, x2, mod3, wos, woa, wout, nw, seq):
    t = x2.shape[0]
    tm = 512
    per_b = seq // tm
    return pl.pallas_call(
        _merge_kernel,
        out_shape=jax.ShapeDtypeStruct((t, D_MODEL), F32),
        grid=(t // tm,),
        in_specs=[pl.BlockSpec((tm, D_INNER), lambda i: (i, 0)),
                  pl.BlockSpec((tm, DIFF_V), lambda i: (i, 0)),
                  pl.BlockSpec((tm, 2 * D_MODEL), lambda i: (i, GATE_OFF // (2 * D_MODEL))),
                  pl.BlockSpec((tm, D_MODEL), lambda i: (i, 0)),
                  pl.BlockSpec((1, 1, 6 * D_MODEL), lambda i: (i // per_b, 0, 0)),
                  _const_spec((D_INNER, D_MODEL)), _const_spec((DIFF_V, D_MODEL)),
                  _const_spec((D_MODEL, D_MODEL)), _const_spec((1, D_MODEL))],
        out_specs=pl.BlockSpec((tm, D_MODEL), lambda i: (i, 0)),
        compiler_params=_params(("parallel",)),
        name="merge",
    )(y, o, proj, x2, mod3, wos, woa, wout, nw)


def _ffn_kernel(x_ref, mod_ref, npre_ref, npost_ref, wg_ref, wu_ref, wd_ref, out_ref):
    mod = mod_ref[0]
    x = x_ref[...]
    h = (_rms(x, npre_ref[...]) * (1.0 + mod[:, 4 * D_MODEL:5 * D_MODEL]) + mod[:, 3 * D_MODEL:4 * D_MODEL]).astype(BF16)
    act = (_silu(_dot(h, wg_ref[...])) * _dot(h, wu_ref[...])).astype(BF16)
    f = _dot(act, wd_ref[...])
    out_ref[...] = x + mod[:, 5 * D_MODEL:6 * D_MODEL] * _rms(f, npost_ref[...])


def _ffn(x1, mod3, npre, npost, wg, wu, wd, seq):
    t = x1.shape[0]
    tm = 512
    per_b = seq // tm
    return pl.pallas_call(
        _ffn_kernel,
        out_shape=jax.ShapeDtypeStruct((t, D_MODEL), F32),
        grid=(t // tm,),
        in_specs=[pl.BlockSpec((tm, D_MODEL), lambda i: (i, 0)),
                  pl.BlockSpec((1, 1, 6 * D_MODEL), lambda i: (i // per_b, 0, 0)),
                  _const_spec((1, D_MODEL)), _const_spec((1, D_MODEL)),
                  _const_spec((D_MODEL, D_FF)), _const_spec((D_MODEL, D_FF)), _const_spec((D_FF, D_MODEL))],
        out_specs=pl.BlockSpec((tm, D_MODEL), lambda i: (i, 0)),
        compiler_params=_params(("parallel",)),
        name="ffn",
    )(x1, mod3, npre, npost, wg, wu, wd)


def _rope_perm(w):
    return w.reshape(D_MODEL, DIFF_HEADS, 2, 2, DIFF_HEAD_DIM // 2).transpose(0, 1, 3, 2, 4).reshape(D_MODEL, DIFF_QK)


def _layer(x2, mod3, pos2, batch, seq, lambda_init, norm_pre_mix, norm_post_mix, norm_pre_ffn, norm_post_ffn,
           w_in, conv_w, conv_b, dt_bias, a_log, d_skip, ssd_norm, w_o_ssd,
           lambda_q1, lambda_k1, lambda_q2, lambda_k2, subln, w_o_attn, w_out, w_gate, w_up, w_down):
    row = lambda v: v.reshape(1, -1).astype(F32)
    w_z, w_xbc, w_dt, w_q, w_k, w_v, w_g = jnp.split(w_in, IN_SPLITS, axis=1)
    w_wide = jnp.concatenate([w_xbc, _rope_perm(w_q), w_z, w_g, _rope_perm(w_k), w_v], axis=1).astype(BF16)
    w_dt = jnp.pad(w_dt, ((0, 0), (0, DT_W - SSD_HEADS))).astype(BF16)

    proj, dt_raw = _in_proj(x2, mod3, row(norm_pre_mix), w_wide, w_dt, seq)

    inv_freq = 1.0 / (ROPE_THETA ** (jnp.arange(0, DIFF_HEAD_DIM, 2, dtype=F32) / DIFF_HEAD_DIM))
    invf = jnp.tile(inv_freq, LANES // inv_freq.shape[0]).reshape(1, LANES)
    sgn = jnp.where(jnp.arange(LANES) < LANES // 2, -1.0, 1.0).astype(F32).reshape(1, LANES)
    q_r, k_r = _rope(proj, pos2, invf, sgn)

    pad_h = lambda v: jnp.pad(v.astype(F32), (0, DT_W - SSD_HEADS)).reshape(1, DT_W)
    expand = (jnp.arange(DT_W)[:, None] == (jnp.arange(D_INNER)[None, :] // SSD_HEAD_DIM)).astype(BF16)
    tril = (jnp.arange(SSD_CHUNK)[None, :] <= jnp.arange(SSD_CHUNK)[:, None]).astype(BF16)
    y = _ssd(proj, dt_raw, conv_w.T.astype(F32), row(conv_b), pad_h(dt_bias), pad_h(a_log),
             row(jnp.repeat(d_skip, SSD_HEAD_DIM)), row(ssd_norm), expand, tril, batch, seq)

    o = _attn(q_r, k_r, proj, row(lambda_q1), row(lambda_k1), row(lambda_q2), row(lambda_k2), row(subln),
              batch, seq, lambda_init)

    x1 = _merge(y, o, proj, x2, mod3, w_o_ssd.astype(BF16), w_o_attn.astype(BF16), w_out.astype(BF16),
                row(norm_post_mix), seq)
    return _ffn(x1, mod3, row(norm_pre_ffn), row(norm_post_ffn), w_gate.astype(BF16), w_up.astype(BF16),
                w_down.astype(BF16), seq)


def kernel(x, c, positions, w_ada, b_ada, norm_pre_mix, norm_post_mix, norm_pre_ffn, norm_post_ffn, w_in, conv_w, conv_b, dt_bias, a_log, d_skip, ssd_norm, w_o_ssd, lambda_q1, lambda_k1, lambda_q2, lambda_k2, subln, w_o_attn, w_out, w_gate, w_up, w_down):
    batch, seq, _ = x.shape
    depth = w_in.shape[0]
    x2 = x.reshape(batch * seq, D_MODEL)
    pos2 = positions.reshape(batch * seq, 1)
    c_pad = jnp.pad(c, ((0, 8 - batch), (0, 0)))
    for layer in range(depth):
        lambda_init = 0.8 - 0.6 * math.exp(-0.3 * layer)
        mod3 = _ada(c_pad, w_ada[layer], b_ada[layer].reshape(1, -1))[:batch].reshape(batch, 1, 6 * D_MODEL)
        x2 = _layer(x2, mod3, pos2, batch, seq, lambda_init, norm_pre_mix[layer], norm_post_mix[layer],
                    norm_pre_ffn[layer], norm_post_ffn[layer], w_in[layer], conv_w[layer], conv_b[layer],
                    dt_bias[layer], a_log[layer], d_skip[layer], ssd_norm[layer], w_o_ssd[layer],
                    lambda_q1[layer], lambda_k1[layer], lambda_q2[layer], lambda_k2[layer], subln[layer],
                    w_o_attn[layer], w_out[layer], w_gate[layer], w_up[layer], w_down[layer])
    return x2.reshape(batch, seq, D_MODEL)
```

```python
import functools
import math

import jax
import jax.numpy as jnp
import numpy as np
from jax import lax
from jax.experimental import pallas as pl
from jax.experimental.pallas import tpu as pltpu

F32 = jnp.float32
BF16 = jnp.bfloat16

D_MODEL = 1024
D_INNER = 2048
SSD_HEAD_DIM = 64
SSD_HEADS = 32
SSD_GROUPS = 4
SSD_STATE = 128
SSD_CONV = 4
SSD_CHUNK = 256
CONV_DIM = D_INNER + 2 * SSD_GROUPS * SSD_STATE
GROUP_CH = D_INNER // SSD_GROUPS
DIFF_HEADS = 8
DIFF_HEAD_DIM = 64
DIFF_QK = 1024
DIFF_V = 1024
ROPE_THETA = 10000.0
D_FF = 2816
EPS = 1e-6
IN_SIZES = (D_INNER, CONV_DIM, SSD_HEADS, DIFF_QK, DIFF_QK, DIFF_V, 2 * D_MODEL)
IN_SPLITS = tuple(int(v) for v in np.cumsum(IN_SIZES)[:-1])

LANES = 128
HEAD_W = 2 * DIFF_HEAD_DIM

XBC_OFF, Q_OFF, Z_OFF, GATE_OFF, K_OFF, V_OFF = 0, 3072, 4096, 6144, 8192, 9216
PROJ_W = 10240
DT_W = LANES

VMEM_LIMIT = 56 * 1024 * 1024

Q_SCALE = DIFF_HEAD_DIM ** -0.5 * math.log2(math.e)
NEG_BIG = -1e30


def _sigmoid(v):
    return 1.0 / (1.0 + jnp.exp(-v))


def _silu(v):
    return v * _sigmoid(v)


def _rms(v, w):
    return v * lax.rsqrt(jnp.mean(v * v, axis=-1, keepdims=True) + EPS) * w


def _params(sem):
    return pltpu.CompilerParams(dimension_semantics=sem, vmem_limit_bytes=VMEM_LIMIT)


def _const_spec(shape):
    nd = len(shape)
    return pl.BlockSpec(shape, lambda *_: (0,) * nd, pipeline_mode=pl.Buffered(1))


def _ada_kernel(c_ref, w_ref, b_ref, o_ref):
    cond = _silu(c_ref[...])
    o_ref[...] = jnp.dot(cond.astype(BF16), w_ref[...].astype(BF16), preferred_element_type=F32) + b_ref[...]


def _ada(c_pad, w_ada, b_ada):
    n = w_ada.shape[1]
    tn = 1024
    return pl.pallas_call(
        _ada_kernel,
        out_shape=jax.ShapeDtypeStruct((c_pad.shape[0], n), F32),
        grid=(n // tn,),
        in_specs=[pl.BlockSpec(c_pad.shape, lambda j: (0, 0)),
                  pl.BlockSpec((D_MODEL, tn), lambda j: (0, j)),
                  pl.BlockSpec((1, tn), lambda j: (0, j))],
        out_specs=pl.BlockSpec((c_pad.shape[0], tn), lambda j: (0, j)),
        compiler_params=_params(("arbitrary",)),
        name="ada",
    )(c_pad, w_ada, b_ada)


def _inproj_kernel(x_ref, mod_ref, nw_ref, w_ref, wdt_ref, out_ref, dt_ref, h_ref):
    @pl.when(pl.program_id(1) == 0)
    def _():
        mod = mod_ref[0]
        h = _rms(x_ref[...], nw_ref[...]) * (1.0 + mod[:, D_MODEL:2 * D_MODEL]) + mod[:, 0:D_MODEL]
        hb = h.astype(BF16)
        h_ref[...] = hb
        dt_ref[...] = jnp.dot(hb, wdt_ref[...], preferred_element_type=F32)

    out_ref[...] = jnp.dot(h_ref[...], w_ref[...], preferred_element_type=F32).astype(BF16)


def _in_proj(x2, mod3, nw, w_wide, w_dt, seq):
    t = x2.shape[0]
    tm, tn = 1024, 2048
    per_b = seq // tm
    return pl.pallas_call(
        _inproj_kernel,
        out_shape=(jax.ShapeDtypeStruct((t, PROJ_W), BF16), jax.ShapeDtypeStruct((t, DT_W), F32)),
        grid=(t // tm, PROJ_W // tn),
        in_specs=[pl.BlockSpec((tm, D_MODEL), lambda i, j: (i, 0)),
                  pl.BlockSpec((1, 1, 6 * D_MODEL), lambda i, j: (i // per_b, 0, 0)),
                  pl.BlockSpec((1, D_MODEL), lambda i, j: (0, 0)),
                  pl.BlockSpec((D_MODEL, tn), lambda i, j: (0, j)),
                  pl.BlockSpec((D_MODEL, DT_W), lambda i, j: (0, 0))],
        out_specs=(pl.BlockSpec((tm, tn), lambda i, j: (i, j)),
                   pl.BlockSpec((tm, DT_W), lambda i, j: (i, 0))),
        scratch_shapes=[pltpu.VMEM((tm, D_MODEL), BF16)],
        compiler_params=_params(("parallel", "arbitrary")),
        name="in_proj",
    )(x2, mod3, nw, w_wide, w_dt)


def _rope_kernel(q_ref, k_ref, pos_ref, invf_ref, sgn_ref, qo_ref, ko_ref):
    ang = pos_ref[...].astype(F32) * invf_ref[...]
    cos = jnp.cos(ang)
    sin = jnp.sin(ang) * sgn_ref[...]
    for h in range(DIFF_HEADS):
        sl = slice(h * HEAD_W, (h + 1) * HEAD_W)
        xq = q_ref[:, sl].astype(F32)
        xk = k_ref[:, sl].astype(F32)
        qo_ref[:, sl] = ((xq * cos + pltpu.roll(xq, 64, 1) * sin) * Q_SCALE).astype(BF16)
        ko_ref[:, sl] = (xk * cos + pltpu.roll(xk, 64, 1) * sin).astype(BF16)


def _rope(proj, pos2, invf, sgn):
    t = proj.shape[0]
    tm = 1024
    return pl.pallas_call(
        _rope_kernel,
        out_shape=(jax.ShapeDtypeStruct((t, DIFF_QK), BF16), jax.ShapeDtypeStruct((t, DIFF_QK), BF16)),
        grid=(t // tm,),
        in_specs=[pl.BlockSpec((tm, DIFF_QK), lambda i: (i, Q_OFF // DIFF_QK)),
                  pl.BlockSpec((tm, DIFF_QK), lambda i: (i, K_OFF // DIFF_QK)),
                  pl.BlockSpec((tm, 1), lambda i: (i, 0)),
                  _const_spec((1, LANES)), _const_spec((1, LANES))],
        out_specs=(pl.BlockSpec((tm, DIFF_QK), lambda i: (i, 0)),
                   pl.BlockSpec((tm, DIFF_QK), lambda i: (i, 0))),
        compiler_params=_params(("parallel",)),
        name="rope",
    )(proj, proj, pos2, invf, sgn)


def _split2(v):
    hi = v.astype(BF16)
    lo = (v - hi.astype(F32)).astype(BF16)
    return hi, lo


def _split3(v):
    hi = v.astype(BF16)
    r = v - hi.astype(F32)
    mid = r.astype(BF16)
    lo = (r - mid.astype(F32)).astype(BF16)
    return hi, mid, lo


def _dot(a, b):
    return jnp.dot(a, b, preferred_element_type=F32)


def _ssd_kernel(xbc_ref, z_ref, dt_ref, convw_ref, convb_ref, dtb_ref, alog_ref, dskip_ref, normw_ref,
                expand_ref, tril_ref, y_ref, xpad_ref, act_ref, yacc_ref, state_ref):
    L = SSD_CHUNK
    halo = 8

    @pl.when(pl.program_id(1) == 0)
    def _():
        xpad_ref[0:halo, :] = jnp.zeros((halo, CONV_DIM), F32)
        state_ref[...] = jnp.zeros(state_ref.shape, F32)

    xpad_ref[halo:halo + L, :] = xbc_ref[...].astype(F32)
    conv = convb_ref[...]
    for k in range(SSD_CONV):
        off = halo - (SSD_CONV - 1) + k
        conv = conv + convw_ref[k:k + 1, :] * xpad_ref[off:off + L, :]
    xpad_ref[0:halo, :] = xpad_ref[L:L + halo, :]
    act_ref[...] = _silu(conv)

    xs = act_ref[:, 0:D_INNER]
    expand = expand_ref[...]

    dt = dt_ref[...] + dtb_ref[...]
    dt = jnp.maximum(dt, 0.0) + jnp.log1p(jnp.exp(-jnp.abs(dt)))
    a = dt * (-jnp.exp(alog_ref[...]))
    tril = tril_ref[...]
    cs = sum(_dot(tril, p) for p in _split3(a))
    cs_t = cs.T
    cs_last = cs[L - 1:L, :]
    ecs = jnp.exp(cs)
    dte = jnp.exp(cs_last - cs)

    dt_e = sum(_dot(p, expand) for p in _split2(dt))
    ecs_e = sum(_dot(p, expand) for p in _split2(ecs))
    dte_e = _dot(dte.astype(BF16), expand)
    cd_e = ecs_e[L - 1:L, :]

    xdt = xs * dt_e
    xw = (xdt * dte_e).astype(BF16)

    row = lax.broadcasted_iota(jnp.int32, (L, L), 0)
    col = lax.broadcasted_iota(jnp.int32, (L, L), 1)
    causal = col <= row
    lane = lax.broadcasted_iota(jnp.int32, (L, LANES), 1)
    first_head = lane < SSD_HEAD_DIM

    for g in range(SSD_GROUPS):
        gs = slice(g * GROUP_CH, (g + 1) * GROUP_CH)
        b_g = act_ref[:, D_INNER + g * SSD_STATE:D_INNER + (g + 1) * SSD_STATE]
        c_g = act_ref[:, D_INNER + (SSD_GROUPS + g) * SSD_STATE:
                      D_INNER + (SSD_GROUPS + g + 1) * SSD_STATE].astype(BF16)
        b_gt = b_g.T.astype(BF16)
        scores = _dot(c_g, b_gt)
        st = state_ref[g]
        y_off = _dot(c_g, st.astype(BF16)) * ecs_e[:, gs]
        state_ref[g] = st * cd_e[:, gs] + _dot(b_gt, xw[:, gs])
        for pair in range(GROUP_CH // LANES):
            h0 = g * (SSD_HEADS // SSD_GROUPS) + 2 * pair
            ps = slice(g * GROUP_CH + pair * LANES, g * GROUP_CH + (pair + 1) * LANES)
            xp = xdt[:, ps].astype(BF16)
            zero = jnp.zeros_like(xp)
            y_pair = None
            for sub, rhs in ((0, jnp.where(first_head, xp, zero)), (1, jnp.where(first_head, zero, xp))):
                h = h0 + sub
                seg = cs[:, h:h + 1] - cs_t[h:h + 1, :]
                m = (scores * jnp.exp(jnp.where(causal, seg, NEG_BIG))).astype(BF16)
                part = _dot(m, rhs)
                y_pair = part if y_pair is None else y_pair + part
            yacc_ref[:, ps] = (y_pair + y_off[:, pair * LANES:(pair + 1) * LANES]
                               + xs[:, ps] * dskip_ref[:, ps])

    for g in range(SSD_GROUPS):
        gs = slice(g * GROUP_CH, (g + 1) * GROUP_CH)
        yg = yacc_ref[:, gs] * _silu(z_ref[:, gs].astype(F32))
        y_ref[:, gs] = _rms(yg, normw_ref[:, gs]).astype(y_ref.dtype)


def _ssd(proj, dt_raw, convw_t, convb, dtb, alog, dskip_e, normw, expand, tril, batch, seq):
    t = proj.shape[0]
    L = SSD_CHUNK
    nc = seq // L
    row = lambda b, c: b * nc + c
    return pl.pallas_call(
        _ssd_kernel,
        out_shape=jax.ShapeDtypeStruct((t, D_INNER), BF16),
        grid=(batch, nc),
        in_specs=[pl.BlockSpec((L, CONV_DIM), lambda b, c: (row(b, c), XBC_OFF // CONV_DIM)),
                  pl.BlockSpec((L, D_INNER), lambda b, c: (row(b, c), Z_OFF // D_INNER)),
                  pl.BlockSpec((L, DT_W), lambda b, c: (row(b, c), 0)),
                  _const_spec((SSD_CONV, CONV_DIM)), _const_spec((1, CONV_DIM)),
                  _const_spec((1, DT_W)), _const_spec((1, DT_W)),
                  _const_spec((1, D_INNER)), _const_spec((1, D_INNER)),
                  _const_spec((DT_W, D_INNER)), _const_spec((L, L))],
        out_specs=pl.BlockSpec((L, D_INNER), lambda b, c: (row(b, c), 0)),
        scratch_shapes=[pltpu.VMEM((L + 8, CONV_DIM), F32),
                        pltpu.VMEM((L, CONV_DIM), F32),
                        pltpu.VMEM((L, D_INNER), F32),
                        pltpu.VMEM((SSD_GROUPS, SSD_STATE, GROUP_CH), F32)],
        compiler_params=_params(("parallel", "arbitrary")),
        name="ssd",
    )(proj, proj, dt_raw, convw_t, convb, dtb, alog, dskip_e, normw, expand, tril)


def _attn_kernel(q_ref, k_ref, v_ref, lq1_ref, lk1_ref, lq2_ref, lk2_ref, subln_ref, o_ref,
                 q2_ref, s0_ref, s1_ref, m_ref, l_ref, acc_ref, *, tq, tk, rb, lambda_init):
    assert tq == tk and tq % rb == 0
    qi = pl.program_id(2)
    q = q_ref[...]
    lane = lax.broadcasted_iota(jnp.int32, (tq, HEAD_W), 1)
    comp0 = (lane % DIFF_HEAD_DIM) < (DIFF_HEAD_DIM // 2)
    zero = jnp.zeros_like(q)
    q2_ref[0:tq, :] = jnp.where(comp0, q, zero)
    q2_ref[tq:2 * tq, :] = jnp.where(comp0, zero, q)
    m_ref[...] = jnp.full(m_ref.shape, NEG_BIG, F32)
    l_ref[...] = jnp.zeros(l_ref.shape, F32)
    acc_ref[...] = jnp.zeros(acc_ref.shape, F32)

    def scores(j, s_ref):
        start = pl.multiple_of(j * tk, tk)
        s_ref[...] = lax.dot_general(q2_ref[...], k_ref[pl.ds(start, tk), :], (((1,), (1,)), ((), ())),
                                     preferred_element_type=F32)

    def softmax_pv(j, s_ref, masked):
        start = pl.multiple_of(j * tk, tk)
        v = v_ref[pl.ds(start, tk), :]
        for r in range(2 * tq // rb):
            rows = slice(r * rb, (r + 1) * rb)
            s = s_ref[rows, :]
            if masked:
                qpos = (r * rb) % tq + lax.broadcasted_iota(jnp.int32, (rb, tk), 0)
                s = jnp.where(lax.broadcasted_iota(jnp.int32, (rb, tk), 1) <= qpos, s, NEG_BIG)
            m_prev = m_ref[rows, :]
            m_next = jnp.maximum(m_prev, jnp.max(s, axis=1, keepdims=True))
            alpha = jnp.exp2(m_prev - m_next)
            p = jnp.exp2(s - pltpu.repeat(m_next, tk // LANES, axis=1))
            l_ref[rows, :] = alpha * l_ref[rows, :] + jnp.sum(p, axis=1, keepdims=True)
            acc_ref[rows, :] = alpha * acc_ref[rows, :] + jnp.dot(p.astype(BF16), v, preferred_element_type=F32)
            m_ref[rows, :] = m_next

    scores(0, s0_ref)

    def pair(i, carry):
        scores(2 * i + 1, s1_ref)
        softmax_pv(2 * i, s0_ref, False)
        scores(2 * i + 2, s0_ref)
        softmax_pv(2 * i + 1, s1_ref, False)
        return carry

    lax.fori_loop(0, qi // 2, pair, 0)

    @pl.when(qi % 2 == 1)
    def _():
        scores(qi, s1_ref)
        softmax_pv(qi - 1, s0_ref, False)
        softmax_pv(qi, s1_ref, True)

    @pl.when(qi % 2 == 0)
    def _():
        softmax_pv(qi, s0_ref, True)

    lam = (jnp.exp(jnp.sum(lq1_ref[...] * lk1_ref[...], axis=-1, keepdims=True))
           - jnp.exp(jnp.sum(lq2_ref[...] * lk2_ref[...], axis=-1, keepdims=True)) + lambda_init)
    o = acc_ref[0:tq, :] / l_ref[0:tq, :] - lam * (acc_ref[tq:2 * tq, :] / l_ref[tq:2 * tq, :])
    o_ref[...] = (_rms(o, subln_ref[...]) * (1.0 - lambda_init)).astype(o_ref.dtype)


def _attn(q_r, k_r, proj, lq1, lk1, lq2, lk2, subln, batch, seq, lambda_init):
    t = q_r.shape[0]
    tq, tk, rb = 512, 512, 256
    nq = seq // tq
    lam_spec = _const_spec((1, DIFF_HEAD_DIM))
    return pl.pallas_call(
        functools.partial(_attn_kernel, tq=tq, tk=tk, rb=rb, lambda_init=lambda_init),
        out_shape=jax.ShapeDtypeStruct((t, DIFF_V), BF16),
        grid=(batch, DIFF_HEADS, nq),
        in_specs=[pl.BlockSpec((tq, HEAD_W), lambda b, h, i: (b * nq + i, h)),
                  pl.BlockSpec((seq, HEAD_W), lambda b, h, i: (b, h)),
                  pl.BlockSpec((seq, HEAD_W), lambda b, h, i: (b, V_OFF // HEAD_W + h)),
                  lam_spec, lam_spec, lam_spec, lam_spec, _const_spec((1, HEAD_W))],
        out_specs=pl.BlockSpec((tq, HEAD_W), lambda b, h, i: (b * nq + i, h)),
        scratch_shapes=[pltpu.VMEM((2 * tq, HEAD_W), BF16),
                        pltpu.VMEM((2 * tq, tk), F32),
                        pltpu.VMEM((2 * tq, tk), F32),
                        pltpu.VMEM((2 * tq, LANES), F32),
                        pltpu.VMEM((2 * tq, LANES), F32),
                        pltpu.VMEM((2 * tq, HEAD_W), F32)],
        compiler_params=_params(("parallel", "parallel", "arbitrary")),
        name="diff_attn",
    )(q_r, k_r, proj, lq1, lk1, lq2, lk2, subln)


def _merge_kernel(y_ref, o_ref, gate_ref, x_ref, mod_ref, wos_ref, woa_ref, wout_ref, nw_ref, out_ref):
    y_ssd = _dot(y_ref[...], wos_ref[...])
    y_attn = _dot(o_ref[...], woa_ref[...])
    g_ssd = _sigmoid(gate_ref[:, 0:D_MODEL].astype(F32))
    g_attn = _sigmoid(gate_ref[:, D_MODEL:2 * D_MODEL].astype(F32))
    merged = (g_ssd * y_ssd + g_attn * y_attn).astype(BF16)
    mix = _dot(merged, wout_ref[...])
    g1 = mod_ref[0][:, 2 * D_MODEL:3 * D_MODEL]
    out_ref[...] = x_ref[...] + g1 * _rms(mix, nw_ref[...])


def _merge(y, o, proj, x2, mod3, wos, woa, wout, nw, seq):
    t = x2.shape[0]
    tm = 512
    per_b = seq // tm
    return pl.pallas_call(
        _merge_kernel,
        out_shape=jax.ShapeDtypeStruct((t, D_MODEL), F32),
        grid=(t // tm,),
        in_specs=[pl.BlockSpec((tm, D_INNER), lambda i: (i, 0)),
                  pl.BlockSpec((tm, DIFF_V), lambda i: (i, 0)),
                  pl.BlockSpec((tm, 2 * D_MODEL), lambda i: (i, GATE_OFF // (2 * D_MODEL))),
                  pl.BlockSpec((tm, D_MODEL), lambda i: (i, 0)),
                  pl.BlockSpec((1, 1, 6 * D_MODEL), lambda i: (i // per_b, 0, 0)),
                  _const_spec((D_INNER, D_MODEL)), _const_spec((DIFF_V, D_MODEL)),
                  _const_spec((D_MODEL, D_MODEL)), _const_spec((1, D_MODEL))],
        out_specs=pl.BlockSpec((tm, D_MODEL), lambda i: (i, 0)),
        compiler_params=_params(("parallel",)),
        name="merge",
    )(y, o, proj, x2, mod3, wos, woa, wout, nw)


def _ffn_kernel(x_ref, mod_ref, npre_ref, npost_ref, wg_ref, wu_ref, wd_ref, out_ref):
    mod = mod_ref[0]
    x = x_ref[...]
    h = (_rms(x, npre_ref[...]) * (1.0 + mod[:, 4 * D_MODEL:5 * D_MODEL]) + mod[:, 3 * D_MODEL:4 * D_MODEL]).astype(BF16)
    act = (_silu(_dot(h, wg_ref[...])) * _dot(h, wu_ref[...])).astype(BF16)
    f = _dot(act, wd_ref[...])
    out_ref[...] = x + mod[:, 5 * D_MODEL:6 * D_MODEL] * _rms(f, npost_ref[...])


def _ffn(x1, mod3, npre, npost, wg, wu, wd, seq):
    t = x1.shape[0]
    tm = 512
    per_b = seq // tm
    return pl.pallas_call(
        _ffn_kernel,
        out_shape=jax.ShapeDtypeStruct((t, D_MODEL), F32),
        grid=(t // tm,),
        in_specs=[pl.BlockSpec((tm, D_MODEL), lambda i: (i, 0)),
                  pl.BlockSpec((1, 1, 6 * D_MODEL), lambda i: (i // per_b, 0, 0)),
                  _const_spec((1, D_MODEL)), _const_spec((1, D_MODEL)),
                  _const_spec((D_MODEL, D_FF)), _const_spec((D_MODEL, D_FF)), _const_spec((D_FF, D_MODEL))],
        out_specs=pl.BlockSpec((tm, D_MODEL), lambda i: (i, 0)),
        compiler_params=_params(("parallel",)),
        name="ffn",
    )(x1, mod3, npre, npost, wg, wu, wd)


def _rope_perm(w):
    return w.reshape(D_MODEL, DIFF_HEADS, 2, 2, DIFF_HEAD_DIM // 2).transpose(0, 1, 3, 2, 4).reshape(D_MODEL, DIFF_QK)


def _layer(x2, mod3, pos2, batch, seq, lambda_init, norm_pre_mix, norm_post_mix, norm_pre_ffn, norm_post_ffn,
           w_in, conv_w, conv_b, dt_bias, a_log, d_skip, ssd_norm, w_o_ssd,
           lambda_q1, lambda_k1, lambda_q2, lambda_k2, subln, w_o_attn, w_out, w_gate, w_up, w_down):
    row = lambda v: v.reshape(1, -1).astype(F32)
    w_z, w_xbc, w_dt, w_q, w_k, w_v, w_g = jnp.split(w_in, IN_SPLITS, axis=1)
    w_wide = jnp.concatenate([w_xbc, _rope_perm(w_q), w_z, w_g, _rope_perm(w_k), w_v], axis=1).astype(BF16)
    w_dt = jnp.pad(w_dt, ((0, 0), (0, DT_W - SSD_HEADS))).astype(BF16)

    proj, dt_raw = _in_proj(x2, mod3, row(norm_pre_mix), w_wide, w_dt, seq)

    inv_freq = 1.0 / (ROPE_THETA ** (jnp.arange(0, DIFF_HEAD_DIM, 2, dtype=F32) / DIFF_HEAD_DIM))
    invf = jnp.tile(inv_freq, LANES // inv_freq.shape[0]).reshape(1, LANES)
    sgn = jnp.where(jnp.arange(LANES) < LANES // 2, -1.0, 1.0).astype(F32).reshape(1, LANES)
    q_r, k_r = _rope(proj, pos2, invf, sgn)

    pad_h = lambda v: jnp.pad(v.astype(F32), (0, DT_W - SSD_HEADS)).reshape(1, DT_W)
    expand = (jnp.arange(DT_W)[:, None] == (jnp.arange(D_INNER)[None, :] // SSD_HEAD_DIM)).astype(BF16)
    tril = (jnp.arange(SSD_CHUNK)[None, :] <= jnp.arange(SSD_CHUNK)[:, None]).astype(BF16)
    y = _ssd(proj, dt_raw, conv_w.T.astype(F32), row(conv_b), pad_h(dt_bias), pad_h(a_log),
             row(jnp.repeat(d_skip, SSD_HEAD_DIM)), row(ssd_norm), expand, tril, batch, seq)

    o = _attn(q_r, k_r, proj, row(lambda_q1), row(lambda_k1), row(lambda_q2), row(lambda_k2), row(subln),
              batch, seq, lambda_init)

    x1 = _merge(y, o, proj, x2, mod3, w_o_ssd.astype(BF16), w_o_attn.astype(BF16), w_out.astype(BF16),
                row(norm_post_mix), seq)
    return _ffn(x1, mod3, row(norm_pre_ffn), row(norm_post_ffn), w_gate.astype(BF16), w_up.astype(BF16),
                w_down.astype(BF16), seq)


def kernel(x, c, positions, w_ada, b_ada, norm_pre_mix, norm_post_mix, norm_pre_ffn, norm_post_ffn, w_in, conv_w, conv_b, dt_bias, a_log, d_skip, ssd_norm, w_o_ssd, lambda_q1, lambda_k1, lambda_q2, lambda_k2, subln, w_o_attn, w_out, w_gate, w_up, w_down):
    batch, seq, _ = x.shape
    depth = w_in.shape[0]
    x2 = x.reshape(batch * seq, D_MODEL)
    pos2 = positions.reshape(batch * seq, 1)
    c_pad = jnp.pad(c, ((0, 8 - batch), (0, 0)))
    for layer in range(depth):
        lambda_init = 0.8 - 0.6 * math.exp(-0.3 * layer)
        mod3 = _ada(c_pad, w_ada[layer], b_ada[layer].reshape(1, -1))[:batch].reshape(batch, 1, 6 * D_MODEL)
        x2 = _layer(x2, mod3, pos2, batch, seq, lambda_init, norm_pre_mix[layer], norm_post_mix[layer],
                    norm_pre_ffn[layer], norm_post_ffn[layer], w_in[layer], conv_w[layer], conv_b[layer],
                    dt_bias[layer], a_log[layer], d_skip[layer], ssd_norm[layer], w_o_ssd[layer],
                    lambda_q1[layer], lambda_k1[layer], lambda_q2[layer], lambda_k2[layer], subln[layer],
                    w_o_attn[layer], w_out[layer], w_gate[layer], w_up[layer], w_down[layer])
    return x2.reshape(batch, seq, D_MODEL)
```

```python
import functools
import math

import jax
import jax.numpy as jnp
import numpy as np
from jax import lax
from jax.experimental import pallas as pl
from jax.experimental.pallas import tpu as pltpu

F32 = jnp.float32
BF16 = jnp.bfloat16

D_MODEL = 1024
D_INNER = 2048
SSD_HEAD_DIM = 64
SSD_HEADS = 32
SSD_GROUPS = 4
SSD_STATE = 128
SSD_CONV = 4
SSD_CHUNK = 256
CONV_DIM = D_INNER + 2 * SSD_GROUPS * SSD_STATE
GROUP_CH = D_INNER // SSD_GROUPS
DIFF_HEADS = 8
DIFF_HEAD_DIM = 64
DIFF_QK = 1024
DIFF_V = 1024
ROPE_THETA = 10000.0
D_FF = 2816
EPS = 1e-6
IN_SIZES = (D_INNER, CONV_DIM, SSD_HEADS, DIFF_QK, DIFF_QK, DIFF_V, 2 * D_MODEL)
IN_SPLITS = tuple(int(v) for v in np.cumsum(IN_SIZES)[:-1])

LANES = 128
SUBLANES = 8
HEAD_W = 2 * DIFF_HEAD_DIM

XBC_OFF, Q_OFF, Z_OFF, GATE_OFF, K_OFF, V_OFF = 0, 3072, 4096, 6144, 8192, 9216
PROJ_W = 10240
PROJ_TN = 1024
DT_W = LANES

VMEM_LIMIT = 56 * 1024 * 1024

LOG2E = math.log2(math.e)
Q_SCALE = DIFF_HEAD_DIM ** -0.5 * LOG2E
NEG_BIG = -1e30


def _sigmoid(v):
    return 1.0 / (1.0 + jnp.exp(-v))


def _silu(v):
    return v * _sigmoid(v)


def _rms(v, w):
    return v * lax.rsqrt(jnp.mean(v * v, axis=-1, keepdims=True) + EPS) * w


def _dot(a, b):
    return jnp.dot(a, b, preferred_element_type=F32)


def _params(sem):
    return pltpu.CompilerParams(dimension_semantics=sem, vmem_limit_bytes=VMEM_LIMIT)


def _const_spec(shape):
    nd = len(shape)
    return pl.BlockSpec(shape, lambda *_: (0,) * nd, pipeline_mode=pl.Buffered(1))


def _ada_kernel(c_ref, w_ref, b_ref, o_ref):
    cond = _silu(c_ref[...])
    o_ref[...] = _dot(cond.astype(BF16), w_ref[...].astype(BF16)) + b_ref[...]


def _ada(c_pad, w_ada, b_ada):
    n = w_ada.shape[1]
    tn = 1024
    return pl.pallas_call(
        _ada_kernel,
        out_shape=jax.ShapeDtypeStruct((c_pad.shape[0], n), F32),
        grid=(n // tn,),
        in_specs=[pl.BlockSpec(c_pad.shape, lambda j: (0, 0)),
                  pl.BlockSpec((D_MODEL, tn), lambda j: (0, j)),
                  pl.BlockSpec((1, tn), lambda j: (0, j))],
        out_specs=pl.BlockSpec((c_pad.shape[0], tn), lambda j: (0, j)),
        compiler_params=_params(("arbitrary",)),
        name="ada",
    )(c_pad, w_ada, b_ada)


N_CONV_TILES = CONV_DIM // PROJ_TN
Q_TILE, K_TILE = Q_OFF // PROJ_TN, K_OFF // PROJ_TN
Z_TILES = (Z_OFF // PROJ_TN, Z_OFF // PROJ_TN + 1)
GATE_TILES = (GATE_OFF // PROJ_TN, GATE_OFF // PROJ_TN + 1)
V_TILE = V_OFF // PROJ_TN
CONV_ROW_BLOCKS = 2


def _inproj_kernel(x_ref, mod_ref, nw_ref, pos_ref, invf_ref, sgn_ref, w_ref, wdt_ref, convw_ref, convb_ref,
                   out_ref, dt_ref, h_ref, cos_ref, sin_ref, halo_ref, *, tiles_per_seq):
    i, j = pl.program_id(0), pl.program_id(1)
    tm = x_ref.shape[0]

    @pl.when(j == 0)
    def _():
        mod = mod_ref[0]
        h = _rms(x_ref[...], nw_ref[...]) * (1.0 + mod[:, D_MODEL:2 * D_MODEL]) + mod[:, 0:D_MODEL]
        hb = h.astype(BF16)
        h_ref[...] = hb
        dt_ref[...] = _dot(hb, wdt_ref[...])
        ang = pos_ref[...].astype(F32) * invf_ref[...]
        cos_ref[...] = jnp.cos(ang)
        sin_ref[...] = jnp.sin(ang) * sgn_ref[...]

    def proj():
        return _dot(h_ref[...], w_ref[...])

    def rope(r, scale):
        cos, sin = cos_ref[...], sin_ref[...]
        for hd in range(PROJ_TN // HEAD_W):
            sl = slice(hd * HEAD_W, (hd + 1) * HEAD_W)
            xh = r[:, sl]
            out_ref[:, sl] = ((xh * cos + pltpu.roll(xh, HEAD_W // 2, 1) * sin) * scale).astype(BF16)

    @pl.when(j < N_CONV_TILES)
    def _():
        w = convw_ref[...]
        b = convb_ref[...]
        taps = [w[SSD_CONV - 1 - s:SSD_CONV - s, :] for s in range(SSD_CONV)]

        @pl.when(i % tiles_per_seq == 0)
        def _():
            halo_ref[j] = jnp.zeros((SUBLANES, PROJ_TN), F32)

        rows = lax.broadcasted_iota(jnp.int32, (SUBLANES, PROJ_TN), 0)
        tail = halo_ref[j]
        rb = tm // CONV_ROW_BLOCKS
        for blk in range(CONV_ROW_BLOCKS):
            r = _dot(h_ref[blk * rb:(blk + 1) * rb, :], w_ref[...])
            conv = b + taps[0] * r
            for s in range(1, SSD_CONV):
                conv = conv + taps[s] * pltpu.roll(r, s, 0)
            out_ref[blk * rb:(blk + 1) * rb, :] = _silu(conv).astype(BF16)
            head = r[0:SUBLANES, :]
            fix = b + taps[0] * head
            for s in range(1, SSD_CONV):
                fix = fix + taps[s] * jnp.where(rows < s, pltpu.roll(tail, s, 0), pltpu.roll(head, s, 0))
            out_ref[blk * rb:blk * rb + SUBLANES, :] = _silu(fix).astype(BF16)
            tail = r[rb - SUBLANES:rb, :]
        halo_ref[j] = tail

    @pl.when(j == Q_TILE)
    def _():
        rope(proj(), Q_SCALE)

    @pl.when(j == K_TILE)
    def _():
        rope(proj(), 1.0)

    @pl.when((j == Z_TILES[0]) | (j == Z_TILES[1]))
    def _():
        out_ref[...] = _silu(proj()).astype(BF16)

    @pl.when((j == GATE_TILES[0]) | (j == GATE_TILES[1]))
    def _():
        out_ref[...] = _sigmoid(proj()).astype(BF16)

    @pl.when(j == V_TILE)
    def _():
        out_ref[...] = proj().astype(BF16)


def _in_proj(x2, mod3, nw, pos2, invf, sgn, w_wide, w_dt, convw_t, convb, seq):
    t = x2.shape[0]
    tm, tn = 1024, PROJ_TN
    per_b = seq // tm
    conv_tile = lambda i, j: (0, jnp.minimum(j, N_CONV_TILES - 1))
    return pl.pallas_call(
        functools.partial(_inproj_kernel, tiles_per_seq=per_b),
        out_shape=(jax.ShapeDtypeStruct((t, PROJ_W), BF16), jax.ShapeDtypeStruct((t, DT_W), F32)),
        grid=(t // tm, PROJ_W // tn),
        in_specs=[pl.BlockSpec((tm, D_MODEL), lambda i, j: (i, 0)),
                  pl.BlockSpec((1, 1, 6 * D_MODEL), lambda i, j: (i // per_b, 0, 0)),
                  _const_spec((1, D_MODEL)),
                  pl.BlockSpec((tm, 1), lambda i, j: (i, 0)),
                  _const_spec((1, LANES)), _const_spec((1, LANES)),
                  pl.BlockSpec((D_MODEL, tn), lambda i, j: (0, j)),
                  _const_spec((D_MODEL, DT_W)),
                  pl.BlockSpec((SSD_CONV, tn), conv_tile),
                  pl.BlockSpec((1, tn), conv_tile)],
        out_specs=(pl.BlockSpec((tm, tn), lambda i, j: (i, j)),
                   pl.BlockSpec((tm, DT_W), lambda i, j: (i, 0))),
        scratch_shapes=[pltpu.VMEM((tm, D_MODEL), BF16),
                        pltpu.VMEM((tm, LANES), F32),
                        pltpu.VMEM((tm, LANES), F32),
                        pltpu.VMEM((N_CONV_TILES, SUBLANES, tn), F32)],
        compiler_params=_params(("arbitrary", "arbitrary")),
        name="in_proj",
    )(x2, mod3, nw, pos2, invf, sgn, w_wide, w_dt, convw_t, convb)


def _split2(v):
    hi = v.astype(BF16)
    lo = (v - hi.astype(F32)).astype(BF16)
    return hi, lo


def _split3(v):
    hi = v.astype(BF16)
    r = v - hi.astype(F32)
    mid = r.astype(BF16)
    lo = (r - mid.astype(F32)).astype(BF16)
    return hi, mid, lo


def _ssd_kernel(act_ref, zs_ref, dt_ref, dtb_ref, alog_ref, dskip_ref, normw_ref, expand_ref, tril_ref,
                y_ref, yacc_ref, state_ref):
    L = SSD_CHUNK
    H = L // 2

    @pl.when(pl.program_id(1) == 0)
    def _():
        state_ref[...] = jnp.zeros(state_ref.shape, F32)

    xs = act_ref[:, 0:D_INNER].astype(F32)
    expand = expand_ref[...]

    dt = dt_ref[...] + dtb_ref[...]
    dt = jnp.maximum(dt, 0.0) + jnp.log1p(jnp.exp(-jnp.abs(dt)))
    a = dt * (-LOG2E * jnp.exp(alog_ref[...]))
    tril = tril_ref[...]
    cs = sum(_dot(tril, p) for p in _split3(a))
    key_t = cs.T - jnp.log2(dt.T)
    cs_last = cs[L - 1:L, :]
    ecs_e = _dot(jnp.exp2(cs).astype(BF16), expand)
    dtw_e = _dot((dt * jnp.exp2(cs_last - cs)).astype(BF16), expand)
    cd16 = jnp.broadcast_to(jnp.exp2(cs_last), (2 * SUBLANES, DT_W))
    cd_e = sum(_dot(p, expand) for p in _split3(cd16))[0:1, :]

    xw = (xs * dtw_e).astype(BF16)

    tri = lax.broadcasted_iota(jnp.int32, (H, H), 1) <= lax.broadcasted_iota(jnp.int32, (H, H), 0)
    first_head = lax.broadcasted_iota(jnp.int32, (L, LANES), 1) < SSD_HEAD_DIM

    for g in range(SSD_GROUPS):
        gs = slice(g * GROUP_CH, (g + 1) * GROUP_CH)
        b_g = act_ref[:, D_INNER + g * SSD_STATE:D_INNER + (g + 1) * SSD_STATE]
        c_g = act_ref[:, D_INNER + (SSD_GROUPS + g) * SSD_STATE:D_INNER + (SSD_GROUPS + g + 1) * SSD_STATE]
        b_gt = b_g.astype(F32).T.astype(BF16)
        sc_top = _dot(c_g[0:H, :], b_gt[:, 0:H])
        sc_bot = _dot(c_g[H:L, :], b_gt)
        st = state_ref[g]
        y_off = _dot(c_g, st.astype(BF16)) * ecs_e[:, gs]
        state_ref[g] = st * cd_e[:, gs] + _dot(b_gt, xw[:, gs])
        for pair in range(GROUP_CH // LANES):
            h0 = g * (SSD_HEADS // SSD_GROUPS) + 2 * pair
            ps = slice(g * GROUP_CH + pair * LANES, g * GROUP_CH + (pair + 1) * LANES)
            xp = act_ref[:, ps]
            zero = jnp.zeros_like(xp)
            y_top = y_bot = None
            for sub, rhs in ((0, jnp.where(first_head, xp, zero)), (1, jnp.where(first_head, zero, xp))):
                h = h0 + sub
                col = cs[:, h:h + 1]
                rowv = key_t[h:h + 1, :]
                d_tl = jnp.exp2(jnp.where(tri, col[0:H] - rowv[:, 0:H], NEG_BIG))
                d_bl = jnp.exp2(col[H:L] - rowv[:, 0:H])
                d_br = jnp.exp2(jnp.where(tri, col[H:L] - rowv[:, H:L], NEG_BIG))
                m_top = (sc_top * d_tl).astype(BF16)
                m_bot = jnp.concatenate([(sc_bot[:, 0:H] * d_bl).astype(BF16),
                                         (sc_bot[:, H:L] * d_br).astype(BF16)], axis=1)
                p_top = _dot(m_top, rhs[0:H, :])
                p_bot = _dot(m_bot, rhs)
                y_top = p_top if y_top is None else y_top + p_top
                y_bot = p_bot if y_bot is None else y_bot + p_bot
            y_pair = jnp.concatenate([y_top, y_bot], axis=0)
            yacc_ref[:, ps] = (y_pair + y_off[:, pair * LANES:(pair + 1) * LANES]
                               + xs[:, ps] * dskip_ref[:, ps])

    for g in range(SSD_GROUPS):
        gs = slice(g * GROUP_CH, (g + 1) * GROUP_CH)
        yg = yacc_ref[:, gs] * zs_ref[:, gs].astype(F32)
        y_ref[:, gs] = _rms(yg, normw_ref[:, gs]).astype(y_ref.dtype)


def _ssd(proj, dt_raw, dtb, alog, dskip_e, normw, expand, tril, batch, seq):
    t = proj.shape[0]
    L = SSD_CHUNK
    nc = seq // L
    row = lambda b, c: b * nc + c
    return pl.pallas_call(
        _ssd_kernel,
        out_shape=jax.ShapeDtypeStruct((t, D_INNER), BF16),
        grid=(batch, nc),
        in_specs=[pl.BlockSpec((L, CONV_DIM), lambda b, c: (row(b, c), XBC_OFF // CONV_DIM)),
                  pl.BlockSpec((L, D_INNER), lambda b, c: (row(b, c), Z_OFF // D_INNER)),
                  pl.BlockSpec((L, DT_W), lambda b, c: (row(b, c), 0)),
                  _const_spec((1, DT_W)), _const_spec((1, DT_W)),
                  _const_spec((1, D_INNER)), _const_spec((1, D_INNER)),
                  _const_spec((DT_W, D_INNER)), _const_spec((L, L))],
        out_specs=pl.BlockSpec((L, D_INNER), lambda b, c: (row(b, c), 0)),
        scratch_shapes=[pltpu.VMEM((L, D_INNER), F32),
                        pltpu.VMEM((SSD_GROUPS, SSD_STATE, GROUP_CH), F32)],
        compiler_params=_params(("parallel", "arbitrary")),
        name="ssd",
    )(proj, proj, dt_raw, dtb, alog, dskip_e, normw, expand, tril)


def _attn_kernel(q_ref, k_ref, v_ref, lq1_ref, lk1_ref, lq2_ref, lk2_ref, subln_ref, o_ref,
                 q2_ref, vaug_ref, s0_ref, s1_ref, s2_ref, m_ref, acc_ref, *, tq, tk, rb, lambda_init):
    assert tq == tk and tq % rb == 0
    seq = q_ref.shape[0]
    nq = seq // tq

    lane = lax.broadcasted_iota(jnp.int32, (tq, HEAD_W), 1)
    comp0 = (lane % DIFF_HEAD_DIM) < (DIFF_HEAD_DIM // 2)
    for t in range(nq):
        q = q_ref[t * tq:(t + 1) * tq, :]
        zero = jnp.zeros_like(q)
        q2_ref[t, 0:tq, :] = jnp.where(comp0, q, zero)
        q2_ref[t, tq:2 * tq, :] = jnp.where(comp0, zero, q)
    vaug_ref[:, 0:HEAD_W] = v_ref[...]
    vaug_ref[:, HEAD_W:2 * HEAD_W] = jnp.ones((seq, HEAD_W), BF16)

    lam = (jnp.exp(jnp.sum(lq1_ref[...] * lk1_ref[...], axis=-1, keepdims=True))
           - jnp.exp(jnp.sum(lq2_ref[...] * lk2_ref[...], axis=-1, keepdims=True)) + lambda_init)

    def scores(qi, j, s_ref):
        start = pl.multiple_of(j * tk, tk)
        s_ref[...] = lax.dot_general(q2_ref[qi], k_ref[pl.ds(start, tk), :], (((1,), (1,)), ((), ())),
                                     preferred_element_type=F32)

    def softmax_pv(j, s_ref, masked):
        start = pl.multiple_of(j * tk, tk)
        v = vaug_ref[pl.ds(start, tk), :]
        for r in range(2 * tq // rb):
            rows = slice(r * rb, (r + 1) * rb)
            s = s_ref[rows, :]
            if masked:
                qpos = (r * rb) % tq + lax.broadcasted_iota(jnp.int32, (rb, tk), 0)
                s = jnp.where(lax.broadcasted_iota(jnp.int32, (rb, tk), 1) <= qpos, s, NEG_BIG)
            m_prev = m_ref[rows, :]
            m_next = jnp.maximum(m_prev, jnp.max(s, axis=1, keepdims=True))
            alpha = jnp.exp2(m_prev - m_next)
            p = jnp.exp2(s - jnp.concatenate([m_next] * (tk // LANES), axis=1)).astype(BF16)
            acc_ref[rows, :] = jnp.concatenate([alpha, alpha], axis=1) * acc_ref[rows, :] + _dot(p, v)
            m_ref[rows, :] = m_next

    def finish(qi):
        o = (acc_ref[0:tq, 0:HEAD_W] / acc_ref[0:tq, HEAD_W:2 * HEAD_W]
             - lam * (acc_ref[tq:2 * tq, 0:HEAD_W] / acc_ref[tq:2 * tq, HEAD_W:2 * HEAD_W]))
        start = pl.multiple_of(qi * tq, tq)
        o_ref[pl.ds(start, tq), :] = (_rms(o, subln_ref[...]) * (1.0 - lambda_init)).astype(o_ref.dtype)

    scores(0, 0, s2_ref)

    def q_tile(qi, carry):
        m_ref[...] = jnp.full(m_ref.shape, NEG_BIG, F32)
        acc_ref[...] = jnp.zeros(acc_ref.shape, F32)

        def pair(i, c):
            scores(qi, 2 * i + 1, s1_ref)
            softmax_pv(2 * i, s0_ref, False)
            scores(qi, 2 * i + 2, s0_ref)
            softmax_pv(2 * i + 1, s1_ref, False)
            return c

        n_pairs = jnp.maximum(qi - 1, 0) // 2
        lax.fori_loop(0, n_pairs, pair, 0)

        @pl.when(qi % 2 == 1)
        def _():
            scores(qi, qi, s2_ref)
            softmax_pv(qi - 1, s0_ref, False)

        @pl.when((qi % 2 == 0) & (qi > 0))
        def _():
            scores(qi, qi - 1, s1_ref)
            softmax_pv(qi - 2, s0_ref, False)
            scores(qi, qi, s2_ref)
            softmax_pv(qi - 1, s1_ref, False)

        scores(jnp.minimum(qi + 1, nq - 1), 0, s0_ref)
        softmax_pv(qi, s2_ref, True)
        finish(qi)
        return carry

    lax.fori_loop(0, nq, q_tile, 0)


def _attn(proj, lq1, lk1, lq2, lk2, subln, batch, seq, lambda_init):
    t = proj.shape[0]
    tq, tk, rb = 512, 512, 256
    nq = seq // tq
    lam_spec = _const_spec((1, DIFF_HEAD_DIM))
    head_cols = lambda off: (lambda b, h: (b, off // HEAD_W + h))
    return pl.pallas_call(
        functools.partial(_attn_kernel, tq=tq, tk=tk, rb=rb, lambda_init=lambda_init),
        out_shape=jax.ShapeDtypeStruct((t, DIFF_V), BF16),
        grid=(batch, DIFF_HEADS),
        in_specs=[pl.BlockSpec((seq, HEAD_W), head_cols(Q_OFF)),
                  pl.BlockSpec((seq, HEAD_W), head_cols(K_OFF)),
                  pl.BlockSpec((seq, HEAD_W), head_cols(V_OFF)),
                  lam_spec, lam_spec, lam_spec, lam_spec, _const_spec((1, HEAD_W))],
        out_specs=pl.BlockSpec((seq, HEAD_W), lambda b, h: (b, h)),
        scratch_shapes=[pltpu.VMEM((nq, 2 * tq, HEAD_W), BF16),
                        pltpu.VMEM((seq, 2 * HEAD_W), BF16),
                        pltpu.VMEM((2 * tq, tk), F32),
                        pltpu.VMEM((2 * tq, tk), F32),
                        pltpu.VMEM((2 * tq, tk), F32),
                        pltpu.VMEM((2 * tq, LANES), F32),
                        pltpu.VMEM((2 * tq, 2 * HEAD_W), F32)],
        compiler_params=_params(("parallel", "parallel")),
        name="diff_attn",
    )(proj, proj, proj, lq1, lk1, lq2, lk2, subln)


def _merge_kernel(y_ref, o_ref, gate_ref, x_ref, mod_ref, wos_ref, woa_ref, wout_ref, nw_ref, out_ref):
    y_ssd = _dot(y_ref[...], wos_ref[...])
    y_attn = _dot(o_ref[...], woa_ref[...])
    g_ssd = gate_ref[:, 0:D_MODEL].astype(F32)
    g_attn = gate_ref[:, D_MODEL:2 * D_MODEL].astype(F32)
    merged = (g_ssd * y_ssd + g_attn * y_attn).astype(BF16)
    mix = _dot(merged, wout_ref[...])
    g1 = mod_ref[0][:, 2 * D_MODEL:3 * D_MODEL]
    out_ref[...] = x_ref[...] + g1 * _rms(mix, nw_ref[...])


def _merge(y, o, proj, x2, mod3, wos, woa, wout, nw, seq):
    t = x2.shape[0]
    tm = 512
    per_b = seq // tm
    return pl.pallas_call(
        _merge_kernel,
        out_shape=jax.ShapeDtypeStruct((t, D_MODEL), F32),
        grid=(t // tm,),
        in_specs=[pl.BlockSpec((tm, D_INNER), lambda i: (i, 0)),
                  pl.BlockSpec((tm, DIFF_V), lambda i: (i, 0)),
                  pl.BlockSpec((tm, 2 * D_MODEL), lambda i: (i, GATE_OFF // (2 * D_MODEL))),
                  pl.BlockSpec((tm, D_MODEL), lambda i: (i, 0)),
                  pl.BlockSpec((1, 1, 6 * D_MODEL), lambda i: (i // per_b, 0, 0)),
                  _const_spec((D_INNER, D_MODEL)), _const_spec((DIFF_V, D_MODEL)),
                  _const_spec((D_MODEL, D_MODEL)), _const_spec((1, D_MODEL))],
        out_specs=pl.BlockSpec((tm, D_MODEL), lambda i: (i, 0)),
        compiler_params=_params(("parallel",)),
        name="merge",
    )(y, o, proj, x2, mod3, wos, woa, wout, nw)


def _ffn_kernel(x_ref, mod_ref, npre_ref, npost_ref, wg_ref, wu_ref, wd_ref, out_ref):
    mod = mod_ref[0]
    x = x_ref[...]
    h = (_rms(x, npre_ref[...]) * (1.0 + mod[:, 4 * D_MODEL:5 * D_MODEL]) + mod[:, 3 * D_MODEL:4 * D_MODEL]).astype(BF16)
    act = (_silu(_dot(h, wg_ref[...])) * _dot(h, wu_ref[...])).astype(BF16)
    f = _dot(act, wd_ref[...])
    out_ref[...] = x + mod[:, 5 * D_MODEL:6 * D_MODEL] * _rms(f, npost_ref[...])


def _ffn(x1, mod3, npre, npost, wg, wu, wd, seq):
    t = x1.shape[0]
    tm = 512
    per_b = seq // tm
    return pl.pallas_call(
        _ffn_kernel,
        out_shape=jax.ShapeDtypeStruct((t, D_MODEL), F32),
        grid=(t // tm,),
        in_specs=[pl.BlockSpec((tm, D_MODEL), lambda i: (i, 0)),
                  pl.BlockSpec((1, 1, 6 * D_MODEL), lambda i: (i // per_b, 0, 0)),
                  _const_spec((1, D_MODEL)), _const_spec((1, D_MODEL)),
                  _const_spec((D_MODEL, D_FF)), _const_spec((D_MODEL, D_FF)), _const_spec((D_FF, D_MODEL))],
        out_specs=pl.BlockSpec((tm, D_MODEL), lambda i: (i, 0)),
        compiler_params=_params(("parallel",)),
        name="ffn",
    )(x1, mod3, npre, npost, wg, wu, wd)


def _rope_perm(w):
    return w.reshape(D_MODEL, DIFF_HEADS, 2, 2, DIFF_HEAD_DIM // 2).transpose(0, 1, 3, 2, 4).reshape(D_MODEL, DIFF_QK)


def _layer(x2, mod3, pos2, batch, seq, lambda_init, norm_pre_mix, norm_post_mix, norm_pre_ffn, norm_post_ffn,
           w_in, conv_w, conv_b, dt_bias, a_log, d_skip, ssd_norm, w_o_ssd,
           lambda_q1, lambda_k1, lambda_q2, lambda_k2, subln, w_o_attn, w_out, w_gate, w_up, w_down):
    row = lambda v: v.reshape(1, -1).astype(F32)
    w_z, w_xbc, w_dt, w_q, w_k, w_v, w_g = jnp.split(w_in, IN_SPLITS, axis=1)
    w_wide = jnp.concatenate([w_xbc, _rope_perm(w_q), w_z, w_g, _rope_perm(w_k), w_v], axis=1).astype(BF16)
    w_dt = jnp.pad(w_dt, ((0, 0), (0, DT_W - SSD_HEADS))).astype(BF16)
    inv_freq = 1.0 / (ROPE_THETA ** (jnp.arange(0, DIFF_HEAD_DIM, 2, dtype=F32) / DIFF_HEAD_DIM))
    invf = jnp.tile(inv_freq, LANES // inv_freq.shape[0]).reshape(1, LANES)
    sgn = jnp.where(jnp.arange(LANES) < LANES // 2, -1.0, 1.0).astype(F32).reshape(1, LANES)

    proj, dt_raw = _in_proj(x2, mod3, row(norm_pre_mix), pos2, invf, sgn, w_wide, w_dt,
                            conv_w.T.astype(F32), row(conv_b), seq)

    pad_h = lambda v: jnp.pad(v.astype(F32), (0, DT_W - SSD_HEADS)).reshape(1, DT_W)
    expand = (jnp.arange(DT_W)[:, None] == (jnp.arange(D_INNER)[None, :] // SSD_HEAD_DIM)).astype(BF16)
    tril = (jnp.arange(SSD_CHUNK)[None, :] <= jnp.arange(SSD_CHUNK)[:, None]).astype(BF16)
    y = _ssd(proj, dt_raw, pad_h(dt_bias), pad_h(a_log), row(jnp.repeat(d_skip, SSD_HEAD_DIM)), row(ssd_norm),
             expand, tril, batch, seq)

    o = _attn(proj, row(lambda_q1), row(lambda_k1), row(lambda_q2), row(lambda_k2), row(subln),
              batch, seq, lambda_init)

    x1 = _merge(y, o, proj, x2, mod3, w_o_ssd.astype(BF16), w_o_attn.astype(BF16), w_out.astype(BF16),
                row(norm_post_mix), seq)
    return _ffn(x1, mod3, row(norm_pre_ffn), row(norm_post_ffn), w_gate.astype(BF16), w_up.astype(BF16),
                w_down.astype(BF16), seq)


def kernel(x, c, positions, w_ada, b_ada, norm_pre_mix, norm_post_mix, norm_pre_ffn, norm_post_ffn, w_in, conv_w, conv_b, dt_bias, a_log, d_skip, ssd_norm, w_o_ssd, lambda_q1, lambda_k1, lambda_q2, lambda_k2, subln, w_o_attn, w_out, w_gate, w_up, w_down):
    batch, seq, _ = x.shape
    depth = w_in.shape[0]
    x2 = x.reshape(batch * seq, D_MODEL)
    pos2 = positions.reshape(batch * seq, 1)
    c_pad = jnp.pad(c, ((0, SUBLANES - batch), (0, 0)))
    for layer in range(depth):
        lambda_init = 0.8 - 0.6 * math.exp(-0.3 * layer)
        mod3 = _ada(c_pad, w_ada[layer], b_ada[layer].reshape(1, -1))[:batch].reshape(batch, 1, 6 * D_MODEL)
        x2 = _layer(x2, mod3, pos2, batch, seq, lambda_init, norm_pre_mix[layer], norm_post_mix[layer],
                    norm_pre_ffn[layer], norm_post_ffn[layer], w_in[layer], conv_w[layer], conv_b[layer],
                    dt_bias[layer], a_log[layer], d_skip[layer], ssd_norm[layer], w_o_ssd[layer],
                    lambda_q1[layer], lambda_k1[layer], lambda_q2[layer], lambda_k2[layer], subln[layer],
                    w_o_attn[layer], w_out[layer], w_gate[layer], w_up[layer], w_down[layer])
    return x2.reshape(batch, seq, D_MODEL)
```

```python
import functools
import math

import jax
import jax.numpy as jnp
import numpy as np
from jax import lax
from jax.experimental import pallas as pl
from jax.experimental.pallas import tpu as pltpu

F32 = jnp.float32
BF16 = jnp.bfloat16

D_MODEL = 1024
D_INNER = 2048
SSD_HEAD_DIM = 64
SSD_HEADS = 32
SSD_GROUPS = 4
SSD_STATE = 128
SSD_CONV = 4
SSD_CHUNK = 256
CONV_DIM = D_INNER + 2 * SSD_GROUPS * SSD_STATE
GROUP_CH = D_INNER // SSD_GROUPS
DIFF_HEADS = 8
DIFF_HEAD_DIM = 64
DIFF_QK = 1024
DIFF_V = 1024
ROPE_THETA = 10000.0
D_FF = 2816
EPS = 1e-6
IN_SIZES = (D_INNER, CONV_DIM, SSD_HEADS, DIFF_QK, DIFF_QK, DIFF_V, 2 * D_MODEL)
IN_SPLITS = tuple(int(v) for v in np.cumsum(IN_SIZES)[:-1])

LANES = 128
SUBLANES = 8
HEAD_W = 2 * DIFF_HEAD_DIM

XBC_OFF, V_OFF, Z_OFF, GATE_OFF, K_OFF, Q_OFF = 0, 3072, 4096, 6144, 8192, 9216
PROJ_W = 10240
PROJ_TN = 1024
W_XBC_TILE0 = D_INNER // PROJ_TN
W_TAIL_OFF = D_INNER + CONV_DIM + SSD_HEADS
DT_W = LANES

VMEM_LIMIT = 56 * 1024 * 1024

LOG2E = math.log2(math.e)
Q_SCALE = DIFF_HEAD_DIM ** -0.5 * LOG2E
NEG_BIG = -1e30


def _sigmoid(v):
    return 1.0 / (1.0 + jnp.exp(-v))


def _silu(v):
    return v * _sigmoid(v)


def _rms(v, w):
    return v * lax.rsqrt(jnp.mean(v * v, axis=-1, keepdims=True) + EPS) * w


def _dot(a, b):
    return jnp.dot(a, b, preferred_element_type=F32)


def _params(sem):
    return pltpu.CompilerParams(dimension_semantics=sem, vmem_limit_bytes=VMEM_LIMIT)


def _const_spec(shape):
    nd = len(shape)
    return pl.BlockSpec(shape, lambda *_: (0,) * nd, pipeline_mode=pl.Buffered(1))


def _ada_kernel(c_ref, w_ref, b_ref, o_ref):
    cond = _silu(c_ref[...])
    o_ref[...] = _dot(cond.astype(BF16), w_ref[...].astype(BF16)) + b_ref[...]


def _ada(c_pad, w_ada, b_ada):
    n = w_ada.shape[1]
    tn = 1024
    return pl.pallas_call(
        _ada_kernel,
        out_shape=jax.ShapeDtypeStruct((c_pad.shape[0], n), F32),
        grid=(n // tn,),
        in_specs=[pl.BlockSpec(c_pad.shape, lambda j: (0, 0)),
                  pl.BlockSpec((D_MODEL, tn), lambda j: (0, j)),
                  pl.BlockSpec((1, tn), lambda j: (0, j))],
        out_specs=pl.BlockSpec((c_pad.shape[0], tn), lambda j: (0, j)),
        compiler_params=_params(("arbitrary",)),
        name="ada",
    )(c_pad, w_ada, b_ada)


COL_KINDS = ("xbc", "xbc", "xbc", "v", "z", "z", "gate", "gate", "k", "q")
N_COL = len(COL_KINDS)
TRIG_COL = 4
EPI_RB = 128
TRIG_RB = 64
MM_ROWS, MM_COLS = 256, 256
HALO = SUBLANES


def _sigmoid_t(v):
    return 0.5 + 0.5 * jnp.tanh(0.5 * v)


def _silu_t(v):
    h = 0.5 * v
    return h + h * jnp.tanh(h)


def _inproj_kernel(x_ref, mod_ref, nw_ref, pos_ref, invf_ref, sgn_ref, wa_ref, wb_ref, wdt_ref, convw_ref, convb_ref,
                   out_ref, dt_ref, h_ref, park0_ref, park1_ref, cos_ref, sin_ref, halo_ref,
                   *, tiles_per_seq, n_row_tiles):
    step_id = pl.program_id(0)
    i, j = step_id // N_COL, step_id % N_COL
    tm = x_ref.shape[0]
    parks = (park0_ref, park1_ref)

    @pl.when((j == 0) & (i < n_row_tiles))
    def _():
        mod = mod_ref[0]
        h = _rms(x_ref[...], nw_ref[...]) * (1.0 + mod[:, D_MODEL:2 * D_MODEL]) + mod[:, 0:D_MODEL]
        hb = h.astype(BF16)
        h_ref[...] = hb
        dt_ref[...] = _dot(hb, wdt_ref[...])

    def matmul_pieces(col):
        w_ref = wa_ref if COL_KINDS[col] in ("xbc", "z") else wb_ref
        park = parks[col % 2]

        def piece(rows, sl):
            def emit():
                park[rows, sl] = _dot(h_ref[rows, :], w_ref[:, sl])
            return emit

        return [piece(slice(m * MM_ROWS, (m + 1) * MM_ROWS), slice(n * MM_COLS, (n + 1) * MM_COLS))
                for n in range(PROJ_TN // MM_COLS) for m in range(tm // MM_ROWS)]

    def conv_piece(col, park, blk, slab):
        def emit():
            lo = blk * EPI_RB
            sl = slice(slab * LANES, (slab + 1) * LANES)
            cols = slice(col * PROJ_TN + slab * LANES, col * PROJ_TN + (slab + 1) * LANES)
            w = convw_ref[:, cols]
            b = convb_ref[:, cols]
            lead = jnp.where(first_tile, 0.0, halo_ref[:, cols]) if blk == 0 else park[lo - HALO:lo, sl]
            ext = jnp.concatenate([lead, park[lo:lo + EPI_RB, sl]], axis=0)
            conv = b + w[SSD_CONV - 1:SSD_CONV, :] * ext
            for s in range(1, SSD_CONV):
                conv = conv + w[SSD_CONV - 1 - s:SSD_CONV - s, :] * pltpu.roll(ext, s, 0)
            out_ref[lo:lo + EPI_RB, sl] = _silu_t(conv[HALO:, :]).astype(BF16)
            if blk == tm // EPI_RB - 1:
                halo_ref[:, cols] = park[tm - HALO:tm, sl]
        return emit

    def rope_piece(park, scale, blk, hd):
        def emit():
            lane = lax.broadcasted_iota(jnp.int32, (EPI_RB, HEAD_W), 1)
            first_half = (lane % DIFF_HEAD_DIM) < (DIFF_HEAD_DIM // 2)
            rows = slice(blk * EPI_RB, (blk + 1) * EPI_RB)
            sl = slice(hd * HEAD_W, (hd + 1) * HEAD_W)
            xh = park[rows, sl]
            partner = jnp.where(first_half, pltpu.roll(xh, HEAD_W - DIFF_HEAD_DIM // 2, 1),
                                pltpu.roll(xh, DIFF_HEAD_DIM // 2, 1))
            out_ref[rows, sl] = ((xh * cos_ref[rows, :] + partner * sin_ref[rows, :]) * scale).astype(BF16)
        return emit

    def map_piece(park, fn, blk, slab):
        def emit():
            rows = slice(blk * EPI_RB, (blk + 1) * EPI_RB)
            sl = slice(slab * LANES, (slab + 1) * LANES)
            out_ref[rows, sl] = fn(park[rows, sl]).astype(BF16)
        return emit

    def trig_piece(blk):
        def emit():
            quarter, run = tm // 4, DIFF_HEAD_DIM // 2
            lane = lax.broadcasted_iota(jnp.int32, (TRIG_RB, LANES), 1)
            pos = [pos_ref[g * quarter + blk * TRIG_RB:g * quarter + (blk + 1) * TRIG_RB, :].astype(F32)
                   for g in range(4)]
            compact = jnp.where(lane < run, pos[0], jnp.where(lane < 2 * run, pos[1],
                                                              jnp.where(lane < 3 * run, pos[2], pos[3])))
            ang = compact * invf_ref[...]
            for table, ref in ((jnp.cos(ang), cos_ref), (jnp.sin(ang), sin_ref)):
                for g in range(4):
                    base = table if g == 0 else pltpu.roll(table, LANES - g * run, 1)
                    t = jnp.where(lane < run, base, pltpu.roll(base, run, 1))
                    t = jnp.where(lane < 2 * run, t, pltpu.roll(t, 2 * run, 1))
                    rows = slice(g * quarter + blk * TRIG_RB, g * quarter + (blk + 1) * TRIG_RB)
                    ref[rows, :] = t * sgn_ref[...] if ref is sin_ref else t
        return emit

    def epilogue_pieces(col):
        kind, park = COL_KINDS[col], parks[col % 2]
        grid = [(blk, slab) for blk in range(tm // EPI_RB) for slab in range(PROJ_TN // LANES)]
        if kind == "xbc":
            return [conv_piece(col, park, blk, slab) for blk, slab in grid]
        if kind in ("k", "q"):
            return [rope_piece(park, Q_SCALE if kind == "q" else 1.0, blk, hd) for blk, hd in grid]
        fn = {"v": lambda r: r, "z": _silu_t, "gate": _sigmoid_t}[kind]
        return [map_piece(park, fn, blk, slab) for blk, slab in grid]

    def fused(mm_col, ep_col, trig=False):
        mm = matmul_pieces(mm_col) if mm_col is not None else []
        ep = epilogue_pieces(ep_col) if ep_col is not None else []
        if trig:
            ep = ep + [trig_piece(blk) for blk in range(tm // 4 // TRIG_RB)]
        n = max(len(mm), 1)
        for idx in range(n):
            for emit in mm[idx:idx + 1] + ep[idx * len(ep) // n:(idx + 1) * len(ep) // n]:
                emit()

    first_tile = i % tiles_per_seq == 0

    @pl.when(step_id == 0)
    def _():
        halo_ref[...] = jnp.zeros(halo_ref.shape, F32)
        fused(0, None)

    @pl.when((j == 0) & (i > 0) & (i < n_row_tiles))
    def _():
        fused(0, N_COL - 1)

    @pl.when(i == n_row_tiles)
    def _():
        fused(None, N_COL - 1)

    for col in range(1, N_COL):
        @pl.when(j == col)
        def _(col=col):
            fused(col, col - 1, trig=col == TRIG_COL)


def _in_proj(x2, mod3, nw, pos2, invf, sgn, w_all, w_tail, w_dt, convw_t, convb, seq):
    t = x2.shape[0]
    tm, tn = 1024, PROJ_TN
    n_rows = t // tm
    per_b = seq // tm
    row = lambda s: jnp.minimum(s // N_COL, n_rows - 1)
    wa_tile = lambda s: (0, jnp.where(s % N_COL < 3, s % N_COL + W_XBC_TILE0, jnp.clip(s % N_COL - 4, 0, 1)))
    wb_tile = lambda s: (0, jnp.where(s % N_COL <= 3, 2,
                                      jnp.where(s % N_COL <= 6, 3,
                                                jnp.where(s % N_COL == 7, 4, N_COL - 1 - s % N_COL))))
    lagged = lambda s: (jnp.maximum(s - 1, 0) // N_COL, jnp.maximum(s - 1, 0) % N_COL)
    return pl.pallas_call(
        functools.partial(_inproj_kernel, tiles_per_seq=per_b, n_row_tiles=n_rows),
        out_shape=(jax.ShapeDtypeStruct((t, PROJ_W), BF16), jax.ShapeDtypeStruct((t, DT_W), F32)),
        grid=(n_rows * N_COL + 1,),
        in_specs=[pl.BlockSpec((tm, D_MODEL), lambda s: (row(s), 0)),
                  pl.BlockSpec((1, 1, 6 * D_MODEL), lambda s: (row(s) // per_b, 0, 0)),
                  _const_spec((1, D_MODEL)),
                  pl.BlockSpec((tm, 1), lambda s: (row(s), 0)),
                  _const_spec((1, LANES)), _const_spec((1, LANES)),
                  pl.BlockSpec((D_MODEL, tn), wa_tile),
                  pl.BlockSpec((D_MODEL, tn), wb_tile),
                  _const_spec((D_MODEL, DT_W)),
                  _const_spec((SSD_CONV, CONV_DIM)),
                  _const_spec((1, CONV_DIM))],
        out_specs=(pl.BlockSpec((tm, tn), lagged),
                   pl.BlockSpec((tm, DT_W), lambda s: (row(s), 0))),
        scratch_shapes=[pltpu.VMEM((tm, D_MODEL), BF16),
                        pltpu.VMEM((tm, tn), F32),
                        pltpu.VMEM((tm, tn), F32),
                        pltpu.VMEM((tm, LANES), F32),
                        pltpu.VMEM((tm, LANES), F32),
                        pltpu.VMEM((HALO, CONV_DIM), F32)],
        compiler_params=_params(("arbitrary",)),
        name="in_proj",
    )(x2, mod3, nw, pos2, invf, sgn, w_all, w_tail, w_dt, convw_t, convb)


def _split2(v):
    hi = v.astype(BF16)
    lo = (v - hi.astype(F32)).astype(BF16)
    return hi, lo


def _split3(v):
    hi = v.astype(BF16)
    r = v - hi.astype(F32)
    mid = r.astype(BF16)
    lo = (r - mid.astype(F32)).astype(BF16)
    return hi, mid, lo


def _ssd_kernel(act_ref, zs_ref, dt_ref, dtb_ref, alog_ref, dskip_ref, normw_ref, expand_ref, tril_ref,
                y_ref, yacc_ref, state_ref):
    L = SSD_CHUNK
    H = L // 2

    @pl.when(pl.program_id(1) == 0)
    def _():
        state_ref[...] = jnp.zeros(state_ref.shape, F32)

    xs = act_ref[:, 0:D_INNER].astype(F32)
    expand = expand_ref[...]

    dt = dt_ref[...] + dtb_ref[...]
    dt = jnp.maximum(dt, 0.0) + jnp.log1p(jnp.exp(-jnp.abs(dt)))
    a = dt * (-LOG2E * jnp.exp(alog_ref[...]))
    tril = tril_ref[...]
    cs = sum(_dot(tril, p) for p in _split3(a))
    key_t = cs.T - jnp.log2(dt.T)
    cs_last = cs[L - 1:L, :]
    ecs_e = _dot(jnp.exp2(cs).astype(BF16), expand)
    dtw_e = _dot((dt * jnp.exp2(cs_last - cs)).astype(BF16), expand)
    cd16 = jnp.broadcast_to(jnp.exp2(cs_last), (2 * SUBLANES, DT_W))
    cd_e = sum(_dot(p, expand) for p in _split3(cd16))[0:1, :]

    xw = (xs * dtw_e).astype(BF16)

    tri = lax.broadcasted_iota(jnp.int32, (H, H), 1) <= lax.broadcasted_iota(jnp.int32, (H, H), 0)
    first_head = lax.broadcasted_iota(jnp.int32, (L, LANES), 1) < SSD_HEAD_DIM

    for g in range(SSD_GROUPS):
        gs = slice(g * GROUP_CH, (g + 1) * GROUP_CH)
        b_g = act_ref[:, D_INNER + g * SSD_STATE:D_INNER + (g + 1) * SSD_STATE]
        c_g = act_ref[:, D_INNER + (SSD_GROUPS + g) * SSD_STATE:D_INNER + (SSD_GROUPS + g + 1) * SSD_STATE]
        b_gt = b_g.astype(F32).T.astype(BF16)
        sc_top = _dot(c_g[0:H, :], b_gt[:, 0:H])
        sc_bot = _dot(c_g[H:L, :], b_gt)
        st = state_ref[g]
        y_off = _dot(c_g, st.astype(BF16)) * ecs_e[:, gs]
        state_ref[g] = st * cd_e[:, gs] + _dot(b_gt, xw[:, gs])
        for pair in range(GROUP_CH // LANES):
            h0 = g * (SSD_HEADS // SSD_GROUPS) + 2 * pair
            ps = slice(g * GROUP_CH + pair * LANES, g * GROUP_CH + (pair + 1) * LANES)
            xp = act_ref[:, ps]
            zero = jnp.zeros_like(xp)
            y_top = y_bot = None
            for sub, rhs in ((0, jnp.where(first_head, xp, zero)), (1, jnp.where(first_head, zero, xp))):
                h = h0 + sub
                col = cs[:, h:h + 1]
                rowv = key_t[h:h + 1, :]
                d_tl = jnp.exp2(jnp.where(tri, col[0:H] - rowv[:, 0:H], NEG_BIG))
                d_bl = jnp.exp2(col[H:L] - rowv[:, 0:H])
                d_br = jnp.exp2(jnp.where(tri, col[H:L] - rowv[:, H:L], NEG_BIG))
                m_top = (sc_top * d_tl).astype(BF16)
                m_bot = jnp.concatenate([(sc_bot[:, 0:H] * d_bl).astype(BF16),
                                         (sc_bot[:, H:L] * d_br).astype(BF16)], axis=1)
                p_top = _dot(m_top, rhs[0:H, :])
                p_bot = _dot(m_bot, rhs)
                y_top = p_top if y_top is None else y_top + p_top
                y_bot = p_bot if y_bot is None else y_bot + p_bot
            y_pair = jnp.concatenate([y_top, y_bot], axis=0)
            yacc_ref[:, ps] = (y_pair + y_off[:, pair * LANES:(pair + 1) * LANES]
                               + xs[:, ps] * dskip_ref[:, ps])

    for g in range(SSD_GROUPS):
        gs = slice(g * GROUP_CH, (g + 1) * GROUP_CH)
        yg = yacc_ref[:, gs] * zs_ref[:, gs].astype(F32)
        y_ref[:, gs] = _rms(yg, normw_ref[:, gs]).astype(y_ref.dtype)


def _ssd(proj, dt_raw, dtb, alog, dskip_e, normw, expand, tril, batch, seq):
    t = proj.shape[0]
    L = SSD_CHUNK
    nc = seq // L
    row = lambda b, c: b * nc + c
    return pl.pallas_call(
        _ssd_kernel,
        out_shape=jax.ShapeDtypeStruct((t, D_INNER), BF16),
        grid=(batch, nc),
        in_specs=[pl.BlockSpec((L, CONV_DIM), lambda b, c: (row(b, c), XBC_OFF // CONV_DIM)),
                  pl.BlockSpec((L, D_INNER), lambda b, c: (row(b, c), Z_OFF // D_INNER)),
                  pl.BlockSpec((L, DT_W), lambda b, c: (row(b, c), 0)),
                  _const_spec((1, DT_W)), _const_spec((1, DT_W)),
                  _const_spec((1, D_INNER)), _const_spec((1, D_INNER)),
                  _const_spec((DT_W, D_INNER)), _const_spec((L, L))],
        out_specs=pl.BlockSpec((L, D_INNER), lambda b, c: (row(b, c), 0)),
        scratch_shapes=[pltpu.VMEM((L, D_INNER), F32),
                        pltpu.VMEM((SSD_GROUPS, SSD_STATE, GROUP_CH), F32)],
        compiler_params=_params(("parallel", "arbitrary")),
        name="ssd",
    )(proj, proj, dt_raw, dtb, alog, dskip_e, normw, expand, tril)


def _attn_kernel(q_ref, k_ref, v_ref, lq1_ref, lk1_ref, lq2_ref, lk2_ref, subln_ref, o_ref,
                 q2_ref, vaug_ref, s0_ref, s1_ref, s2_ref, m_ref, acc_ref, *, tq, tk, rb, lambda_init):
    assert tq == tk and tq % rb == 0
    seq = q_ref.shape[0]
    nq = seq // tq

    lane = lax.broadcasted_iota(jnp.int32, (tq, HEAD_W), 1)
    comp0 = lane < DIFF_HEAD_DIM
    for t in range(nq):
        q = q_ref[t * tq:(t + 1) * tq, :]
        zero = jnp.zeros_like(q)
        q2_ref[t, 0:tq, :] = jnp.where(comp0, q, zero)
        q2_ref[t, tq:2 * tq, :] = jnp.where(comp0, zero, q)
    vaug_ref[:, 0:HEAD_W] = v_ref[...]
    vaug_ref[:, HEAD_W:2 * HEAD_W] = jnp.ones((seq, HEAD_W), BF16)

    lam = (jnp.exp(jnp.sum(lq1_ref[...] * lk1_ref[...], axis=-1, keepdims=True))
           - jnp.exp(jnp.sum(lq2_ref[...] * lk2_ref[...], axis=-1, keepdims=True)) + lambda_init)

    def scores_block(qi, j, s_ref, r):
        rows = slice(r * rb, (r + 1) * rb)
        start = pl.multiple_of(j * tk, tk)
        s_ref[rows, :] = lax.dot_general(q2_ref[qi, rows, :], k_ref[pl.ds(start, tk), :], (((1,), (1,)), ((), ())),
                                         preferred_element_type=F32)

    def softmax_block(j, s_ref, masked, r):
        rows = slice(r * rb, (r + 1) * rb)
        start = pl.multiple_of(j * tk, tk)
        s = s_ref[rows, :]
        if masked:
            qpos = (r * rb) % tq + lax.broadcasted_iota(jnp.int32, (rb, tk), 0)
            s = jnp.where(lax.broadcasted_iota(jnp.int32, (rb, tk), 1) <= qpos, s, NEG_BIG)
        m_prev = m_ref[rows, :]
        m_next = jnp.maximum(m_prev, jnp.max(s, axis=1, keepdims=True))
        alpha = jnp.exp2(m_prev - m_next)
        p = jnp.exp2(s - jnp.concatenate([m_next] * (tk // LANES), axis=1)).astype(BF16)
        acc_ref[rows, :] = (jnp.concatenate([alpha, alpha], axis=1) * acc_ref[rows, :]
                            + _dot(p, vaug_ref[pl.ds(start, tk), :]))
        m_ref[rows, :] = m_next

    def scores(qi, j, s_ref):
        for r in range(2 * tq // rb):
            scores_block(qi, j, s_ref, r)

    def scores_softmax(nxt, cur):
        for r in range(2 * tq // rb):
            scores_block(*nxt, r)
            softmax_block(*cur, r)

    def finish(qi):
        o = (acc_ref[0:tq, 0:HEAD_W] / acc_ref[0:tq, HEAD_W:2 * HEAD_W]
             - lam * (acc_ref[tq:2 * tq, 0:HEAD_W] / acc_ref[tq:2 * tq, HEAD_W:2 * HEAD_W]))
        start = pl.multiple_of(qi * tq, tq)
        o_ref[pl.ds(start, tq), :] = (_rms(o, subln_ref[...]) * (1.0 - lambda_init)).astype(o_ref.dtype)

    scores(0, 0, s2_ref)

    def q_tile(qi, carry):
        m_ref[...] = jnp.full(m_ref.shape, NEG_BIG, F32)
        acc_ref[...] = jnp.zeros(acc_ref.shape, F32)

        def pair(i, c):
            scores_softmax((qi, 2 * i + 1, s1_ref), (2 * i, s0_ref, False))
            scores_softmax((qi, 2 * i + 2, s0_ref), (2 * i + 1, s1_ref, False))
            return c

        n_pairs = jnp.maximum(qi - 1, 0) // 2
        lax.fori_loop(0, n_pairs, pair, 0)

        @pl.when(qi % 2 == 1)
        def _():
            scores_softmax((qi, qi, s2_ref), (qi - 1, s0_ref, False))

        @pl.when((qi % 2 == 0) & (qi > 0))
        def _():
            scores_softmax((qi, qi - 1, s1_ref), (qi - 2, s0_ref, False))
            scores_softmax((qi, qi, s2_ref), (qi - 1, s1_ref, False))

        scores_softmax((jnp.minimum(qi + 1, nq - 1), 0, s0_ref), (qi, s2_ref, True))
        finish(qi)
        return carry

    lax.fori_loop(0, nq, q_tile, 0)


def _attn(proj, lq1, lk1, lq2, lk2, subln, batch, seq, lambda_init):
    t = proj.shape[0]
    tq, tk, rb = 512, 512, 256
    nq = seq // tq
    lam_spec = _const_spec((1, DIFF_HEAD_DIM))
    head_cols = lambda off: (lambda b, h: (b, off // HEAD_W + h))
    return pl.pallas_call(
        functools.partial(_attn_kernel, tq=tq, tk=tk, rb=rb, lambda_init=lambda_init),
        out_shape=jax.ShapeDtypeStruct((t, DIFF_V), BF16),
        grid=(batch, DIFF_HEADS),
        in_specs=[pl.BlockSpec((seq, HEAD_W), head_cols(Q_OFF)),
                  pl.BlockSpec((seq, HEAD_W), head_cols(K_OFF)),
                  pl.BlockSpec((seq, HEAD_W), head_cols(V_OFF)),
                  lam_spec, lam_spec, lam_spec, lam_spec, _const_spec((1, HEAD_W))],
        out_specs=pl.BlockSpec((seq, HEAD_W), lambda b, h: (b, h)),
        scratch_shapes=[pltpu.VMEM((nq, 2 * tq, HEAD_W), BF16),
                        pltpu.VMEM((seq, 2 * HEAD_W), BF16),
                        pltpu.VMEM((2 * tq, tk), F32),
                        pltpu.VMEM((2 * tq, tk), F32),
                        pltpu.VMEM((2 * tq, tk), F32),
                        pltpu.VMEM((2 * tq, LANES), F32),
                        pltpu.VMEM((2 * tq, 2 * HEAD_W), F32)],
        compiler_params=_params(("parallel", "parallel")),
        name="diff_attn",
    )(proj, proj, proj, lq1, lk1, lq2, lk2, subln)


def _merge_kernel(y_ref, o_ref, gate_ref, x_ref, mod_ref, wos_ref, woa_ref, wout_ref, nw_ref, out_ref):
    y_ssd = _dot(y_ref[...], wos_ref[...])
    y_attn = _dot(o_ref[...], woa_ref[...])
    g_ssd = gate_ref[:, 0:D_MODEL].astype(F32)
    g_attn = gate_ref[:, D_MODEL:2 * D_MODEL].astype(F32)
    merged = (g_ssd * y_ssd + g_attn * y_attn).astype(BF16)
    mix = _dot(merged, wout_ref[...])
    g1 = mod_ref[0][:, 2 * D_MODEL:3 * D_MODEL]
    out_ref[...] = x_ref[...] + g1 * _rms(mix, nw_ref[...])


def _merge(y, o, proj, x2, mod3, wos, woa, wout, nw, seq):
    t = x2.shape[0]
    tm = 512
    per_b = seq // tm
    return pl.pallas_call(
        _merge_kernel,
        out_shape=jax.ShapeDtypeStruct((t, D_MODEL), F32),
        grid=(t // tm,),
        in_specs=[pl.BlockSpec((tm, D_INNER), lambda i: (i, 0)),
                  pl.BlockSpec((tm, DIFF_V), lambda i: (i, 0)),
                  pl.BlockSpec((tm, 2 * D_MODEL), lambda i: (i, GATE_OFF // (2 * D_MODEL))),
                  pl.BlockSpec((tm, D_MODEL), lambda i: (i, 0)),
                  pl.BlockSpec((1, 1, 6 * D_MODEL), lambda i: (i // per_b, 0, 0)),
                  _const_spec((D_INNER, D_MODEL)), _const_spec((DIFF_V, D_MODEL)),
                  _const_spec((D_MODEL, D_MODEL)), _const_spec((1, D_MODEL))],
        out_specs=pl.BlockSpec((tm, D_MODEL), lambda i: (i, 0)),
        compiler_params=_params(("parallel",)),
        name="merge",
    )(y, o, proj, x2, mod3, wos, woa, wout, nw)


def _ffn_kernel(x_ref, mod_ref, npre_ref, npost_ref, wg_ref, wu_ref, wd_ref, out_ref):
    mod = mod_ref[0]
    x = x_ref[...]
    h = (_rms(x, npre_ref[...]) * (1.0 + mod[:, 4 * D_MODEL:5 * D_MODEL]) + mod[:, 3 * D_MODEL:4 * D_MODEL]).astype(BF16)
    act = (_silu(_dot(h, wg_ref[...])) * _dot(h, wu_ref[...])).astype(BF16)
    f = _dot(act, wd_ref[...])
    out_ref[...] = x + mod[:, 5 * D_MODEL:6 * D_MODEL] * _rms(f, npost_ref[...])


def _ffn(x1, mod3, npre, npost, wg, wu, wd, seq):
    t = x1.shape[0]
    tm = 512
    per_b = seq // tm
    return pl.pallas_call(
        _ffn_kernel,
        out_shape=jax.ShapeDtypeStruct((t, D_MODEL), F32),
        grid=(t // tm,),
        in_specs=[pl.BlockSpec((tm, D_MODEL), lambda i: (i, 0)),
                  pl.BlockSpec((1, 1, 6 * D_MODEL), lambda i: (i // per_b, 0, 0)),
                  _const_spec((1, D_MODEL)), _const_spec((1, D_MODEL)),
                  _const_spec((D_MODEL, D_FF)), _const_spec((D_MODEL, D_FF)), _const_spec((D_FF, D_MODEL))],
        out_specs=pl.BlockSpec((tm, D_MODEL), lambda i: (i, 0)),
        compiler_params=_params(("parallel",)),
        name="ffn",
    )(x1, mod3, npre, npost, wg, wu, wd)


def _layer(x2, mod3, pos2, batch, seq, lambda_init, norm_pre_mix, norm_post_mix, norm_pre_ffn, norm_post_ffn,
           w_in, conv_w, conv_b, dt_bias, a_log, d_skip, ssd_norm, w_o_ssd,
           lambda_q1, lambda_k1, lambda_q2, lambda_k2, subln, w_o_attn, w_out, w_gate, w_up, w_down):
    row = lambda v: v.reshape(1, -1).astype(F32)
    w_all = w_in.astype(BF16)
    w_tail = w_in[:, W_TAIL_OFF:].astype(BF16)
    w_dt = jnp.pad(w_in[:, W_TAIL_OFF - SSD_HEADS:W_TAIL_OFF], ((0, 0), (0, DT_W - SSD_HEADS))).astype(BF16)
    inv_freq = 1.0 / (ROPE_THETA ** (jnp.arange(0, DIFF_HEAD_DIM, 2, dtype=F32) / DIFF_HEAD_DIM))
    invf = jnp.tile(inv_freq, LANES // inv_freq.shape[0]).reshape(1, LANES)
    half_lane = jnp.arange(LANES) % DIFF_HEAD_DIM < DIFF_HEAD_DIM // 2
    sgn = jnp.where(half_lane, -1.0, 1.0).astype(F32).reshape(1, LANES)

    proj, dt_raw = _in_proj(x2, mod3, row(norm_pre_mix), pos2, invf, sgn, w_all, w_tail, w_dt,
                            conv_w.T.astype(F32), row(conv_b), seq)

    pad_h = lambda v: jnp.pad(v.astype(F32), (0, DT_W - SSD_HEADS)).reshape(1, DT_W)
    expand = (jnp.arange(DT_W)[:, None] == (jnp.arange(D_INNER)[None, :] // SSD_HEAD_DIM)).astype(BF16)
    tril = (jnp.arange(SSD_CHUNK)[None, :] <= jnp.arange(SSD_CHUNK)[:, None]).astype(BF16)
    y = _ssd(proj, dt_raw, pad_h(dt_bias), pad_h(a_log), row(jnp.repeat(d_skip, SSD_HEAD_DIM)), row(ssd_norm),
             expand, tril, batch, seq)

    o = _attn(proj, row(lambda_q1), row(lambda_k1), row(lambda_q2), row(lambda_k2), row(subln),
              batch, seq, lambda_init)

    x1 = _merge(y, o, proj, x2, mod3, w_o_ssd.astype(BF16), w_o_attn.astype(BF16), w_out.astype(BF16),
                row(norm_post_mix), seq)
    return _ffn(x1, mod3, row(norm_pre_ffn), row(norm_post_ffn), w_gate.astype(BF16), w_up.astype(BF16),
                w_down.astype(BF16), seq)


def kernel(x, c, positions, w_ada, b_ada, norm_pre_mix, norm_post_mix, norm_pre_ffn, norm_post_ffn, w_in, conv_w, conv_b, dt_bias, a_log, d_skip, ssd_norm, w_o_ssd, lambda_q1, lambda_k1, lambda_q2, lambda_k2, subln, w_o_attn, w_out, w_gate, w_up, w_down):
    batch, seq, _ = x.shape
    depth = w_in.shape[0]
    x2 = x.reshape(batch * seq, D_MODEL)
    pos2 = positions.reshape(batch * seq, 1)
    c_pad = jnp.pad(c, ((0, SUBLANES - batch), (0, 0)))
    for layer in range(depth):
        lambda_init = 0.8 - 0.6 * math.exp(-0.3 * layer)
        mod3 = _ada(c_pad, w_ada[layer], b_ada[layer].reshape(1, -1))[:batch].reshape(batch, 1, 6 * D_MODEL)
        x2 = _layer(x2, mod3, pos2, batch, seq, lambda_init, norm_pre_mix[layer], norm_post_mix[layer],
                    norm_pre_ffn[layer], norm_post_ffn[layer], w_in[layer], conv_w[layer], conv_b[layer],
                    dt_bias[layer], a_log[layer], d_skip[layer], ssd_norm[layer], w_o_ssd[layer],
                    lambda_q1[layer], lambda_k1[layer], lambda_q2[layer], lambda_k2[layer], subln[layer],
                    w_o_attn[layer], w_out[layer], w_gate[layer], w_up[layer], w_down[layer])
    return x2.reshape(batch, seq, D_MODEL)
```

```python
import functools
import math

import jax
import jax.numpy as jnp
import numpy as np
from jax import lax
from jax.experimental import pallas as pl
from jax.experimental.pallas import tpu as pltpu

F32 = jnp.float32
BF16 = jnp.bfloat16

D_MODEL = 1024
D_INNER = 2048
SSD_HEAD_DIM = 64
SSD_HEADS = 32
SSD_GROUPS = 4
SSD_STATE = 128
SSD_CONV = 4
SSD_CHUNK = 256
CONV_DIM = D_INNER + 2 * SSD_GROUPS * SSD_STATE
GROUP_CH = D_INNER // SSD_GROUPS
DIFF_HEADS = 8
DIFF_HEAD_DIM = 64
DIFF_QK = 1024
DIFF_V = 1024
ROPE_THETA = 10000.0
D_FF = 2816
EPS = 1e-6
IN_SIZES = (D_INNER, CONV_DIM, SSD_HEADS, DIFF_QK, DIFF_QK, DIFF_V, 2 * D_MODEL)
IN_SPLITS = tuple(int(v) for v in np.cumsum(IN_SIZES)[:-1])

LANES = 128
SUBLANES = 8
HEAD_W = 2 * DIFF_HEAD_DIM

XBC_OFF, V_OFF, Z_OFF, GATE_OFF, K_OFF, Q_OFF = 0, 3072, 4096, 6144, 8192, 9216
PROJ_W = 10240
PROJ_TN = 1024
W_XBC_TILE0 = D_INNER // PROJ_TN
W_TAIL_OFF = D_INNER + CONV_DIM + SSD_HEADS
DT_W = LANES

VMEM_LIMIT = 56 * 1024 * 1024

LOG2E = math.log2(math.e)
Q_SCALE = DIFF_HEAD_DIM ** -0.5 * LOG2E
NEG_BIG = -1e30


def _sigmoid(v):
    return 1.0 / (1.0 + jnp.exp(-v))


def _silu(v):
    return v * _sigmoid(v)


def _rms(v, w):
    return v * lax.rsqrt(jnp.mean(v * v, axis=-1, keepdims=True) + EPS) * w


def _dot(a, b):
    return jnp.dot(a, b, preferred_element_type=F32)


def _params(sem):
    return pltpu.CompilerParams(dimension_semantics=sem, vmem_limit_bytes=VMEM_LIMIT)


def _const_spec(shape):
    nd = len(shape)
    return pl.BlockSpec(shape, lambda *_: (0,) * nd, pipeline_mode=pl.Buffered(1))


def _ada_kernel(c_ref, w_ref, b_ref, o_ref):
    cond = _silu(c_ref[...])
    o_ref[...] = _dot(cond.astype(BF16), w_ref[...].astype(BF16)) + b_ref[...]


def _ada(c_pad, w_ada, b_ada):
    n = w_ada.shape[1]
    tn = 1024
    return pl.pallas_call(
        _ada_kernel,
        out_shape=jax.ShapeDtypeStruct((c_pad.shape[0], n), F32),
        grid=(n // tn,),
        in_specs=[pl.BlockSpec(c_pad.shape, lambda j: (0, 0)),
                  pl.BlockSpec((D_MODEL, tn), lambda j: (0, j)),
                  pl.BlockSpec((1, tn), lambda j: (0, j))],
        out_specs=pl.BlockSpec((c_pad.shape[0], tn), lambda j: (0, j)),
        compiler_params=_params(("arbitrary",)),
        name="ada",
    )(c_pad, w_ada, b_ada)


COL_KINDS = ("xbc", "xbc", "xbc", "v", "z", "z", "gate", "gate", "k", "q")
N_COL = len(COL_KINDS)
TRIG_COL = 4
EPI_RB = 128
TRIG_RB = 64
MM_ROWS, MM_COLS = 512, 256
HALO = SUBLANES


def _sigmoid_t(v):
    return 0.5 + 0.5 * jnp.tanh(0.5 * v)


def _silu_t(v):
    h = 0.5 * v
    return h + h * jnp.tanh(h)


def _inproj_kernel(x_ref, mod_ref, nw_ref, pos_ref, invf_ref, sgn_ref, wa_ref, wb_ref, wdt_ref, convw_ref, convb_ref,
                   out_ref, dt_ref, h_ref, park0_ref, park1_ref, cos_ref, sin_ref, halo_ref,
                   *, tiles_per_seq, n_row_tiles):
    step_id = pl.program_id(0)
    i, j = step_id // N_COL, step_id % N_COL
    tm = x_ref.shape[0]
    parks = (park0_ref, park1_ref)

    @pl.when((j == 0) & (i < n_row_tiles))
    def _():
        mod = mod_ref[0]
        h = _rms(x_ref[...], nw_ref[...]) * (1.0 + mod[:, D_MODEL:2 * D_MODEL]) + mod[:, 0:D_MODEL]
        hb = h.astype(BF16)
        h_ref[...] = hb
        dt_ref[...] = _dot(hb, wdt_ref[...])

    def matmul_pieces(col):
        w_ref = wa_ref if COL_KINDS[col] in ("xbc", "z") else wb_ref
        park = parks[col % 2]

        def piece(rows, sl):
            def emit():
                park[rows, sl] = _dot(h_ref[rows, :], w_ref[:, sl].astype(BF16))
            return emit

        return [piece(slice(m * MM_ROWS, (m + 1) * MM_ROWS), slice(n * MM_COLS, (n + 1) * MM_COLS))
                for n in range(PROJ_TN // MM_COLS) for m in range(tm // MM_ROWS)]

    def conv_piece(col, park, blk, slab):
        def emit():
            lo = blk * EPI_RB
            sl = slice(slab * LANES, (slab + 1) * LANES)
            cols = slice(col * PROJ_TN + slab * LANES, col * PROJ_TN + (slab + 1) * LANES)
            w = convw_ref[:, cols]
            b = convb_ref[:, cols]
            lead = jnp.where(first_tile, 0.0, halo_ref[:, cols]) if blk == 0 else park[lo - HALO:lo, sl]
            ext = jnp.concatenate([lead, park[lo:lo + EPI_RB, sl]], axis=0)
            conv = b + w[SSD_CONV - 1:SSD_CONV, :] * ext
            for s in range(1, SSD_CONV):
                conv = conv + w[SSD_CONV - 1 - s:SSD_CONV - s, :] * pltpu.roll(ext, s, 0)
            out_ref[lo:lo + EPI_RB, sl] = _silu_t(conv[HALO:, :]).astype(BF16)
            if blk == tm // EPI_RB - 1:
                halo_ref[:, cols] = park[tm - HALO:tm, sl]
        return emit

    def rope_piece(park, scale, blk, hd):
        def emit():
            lane = lax.broadcasted_iota(jnp.int32, (EPI_RB, HEAD_W), 1)
            first_half = (lane % DIFF_HEAD_DIM) < (DIFF_HEAD_DIM // 2)
            rows = slice(blk * EPI_RB, (blk + 1) * EPI_RB)
            sl = slice(hd * HEAD_W, (hd + 1) * HEAD_W)
            xh = park[rows, sl]
            partner = jnp.where(first_half, pltpu.roll(xh, HEAD_W - DIFF_HEAD_DIM // 2, 1),
                                pltpu.roll(xh, DIFF_HEAD_DIM // 2, 1))
            out_ref[rows, sl] = ((xh * cos_ref[rows, :] + partner * sin_ref[rows, :]) * scale).astype(BF16)
        return emit

    def map_piece(park, fn, blk, slab):
        def emit():
            rows = slice(blk * EPI_RB, (blk + 1) * EPI_RB)
            sl = slice(slab * LANES, (slab + 1) * LANES)
            out_ref[rows, sl] = fn(park[rows, sl]).astype(BF16)
        return emit

    def trig_piece(blk):
        def emit():
            quarter, run = tm // 4, DIFF_HEAD_DIM // 2
            lane = lax.broadcasted_iota(jnp.int32, (TRIG_RB, LANES), 1)
            pos = [pos_ref[g * quarter + blk * TRIG_RB:g * quarter + (blk + 1) * TRIG_RB, :].astype(F32)
                   for g in range(4)]
            compact = jnp.where(lane < run, pos[0], jnp.where(lane < 2 * run, pos[1],
                                                              jnp.where(lane < 3 * run, pos[2], pos[3])))
            ang = compact * invf_ref[...]
            for table, ref in ((jnp.cos(ang), cos_ref), (jnp.sin(ang), sin_ref)):
                for g in range(4):
                    base = table if g == 0 else pltpu.roll(table, LANES - g * run, 1)
                    t = jnp.where(lane < run, base, pltpu.roll(base, run, 1))
                    t = jnp.where(lane < 2 * run, t, pltpu.roll(t, 2 * run, 1))
                    rows = slice(g * quarter + blk * TRIG_RB, g * quarter + (blk + 1) * TRIG_RB)
                    ref[rows, :] = t * sgn_ref[...] if ref is sin_ref else t
        return emit

    def epilogue_pieces(col):
        kind, park = COL_KINDS[col], parks[col % 2]
        grid = [(blk, slab) for blk in range(tm // EPI_RB) for slab in range(PROJ_TN // LANES)]
        if kind == "xbc":
            return [conv_piece(col, park, blk, slab) for blk, slab in grid]
        if kind in ("k", "q"):
            return [rope_piece(park, Q_SCALE if kind == "q" else 1.0, blk, hd) for blk, hd in grid]
        fn = {"v": lambda r: r, "z": _silu_t, "gate": _sigmoid_t}[kind]
        return [map_piece(park, fn, blk, slab) for blk, slab in grid]

    def fused(mm_col, ep_col, trig=False):
        mm = matmul_pieces(mm_col) if mm_col is not None else []
        ep = epilogue_pieces(ep_col) if ep_col is not None else []
        if trig:
            ep = ep + [trig_piece(blk) for blk in range(tm // 4 // TRIG_RB)]
        n = max(len(mm), 1)
        for idx in range(n):
            for emit in mm[idx:idx + 1] + ep[idx * len(ep) // n:(idx + 1) * len(ep) // n]:
                emit()

    first_tile = i % tiles_per_seq == 0

    @pl.when(step_id == 0)
    def _():
        halo_ref[...] = jnp.zeros(halo_ref.shape, F32)
        fused(0, None)

    @pl.when((j == 0) & (i > 0) & (i < n_row_tiles))
    def _():
        fused(0, N_COL - 1)

    @pl.when(i == n_row_tiles)
    def _():
        fused(None, N_COL - 1)

    for col in range(1, N_COL):
        @pl.when(j == col)
        def _(col=col):
            fused(col, col - 1, trig=col == TRIG_COL)


def _in_proj(x2, mod3, nw, pos2, invf, sgn, w_all, w_tail, w_dt, convw_t, convb, seq):
    t = x2.shape[0]
    tm, tn = 1024, PROJ_TN
    n_rows = t // tm
    per_b = seq // tm
    row = lambda s: jnp.minimum(s // N_COL, n_rows - 1)
    wa_tile = lambda s: (0, jnp.where(s % N_COL < 3, s % N_COL + W_XBC_TILE0, jnp.clip(s % N_COL - 4, 0, 1)))
    wb_tile = lambda s: (0, jnp.where(s % N_COL <= 3, 2,
                                      jnp.where(s % N_COL <= 6, 3,
                                                jnp.where(s % N_COL == 7, 4, N_COL - 1 - s % N_COL))))
    lagged = lambda s: (jnp.maximum(s - 1, 0) // N_COL, jnp.maximum(s - 1, 0) % N_COL)
    return pl.pallas_call(
        functools.partial(_inproj_kernel, tiles_per_seq=per_b, n_row_tiles=n_rows),
        out_shape=(jax.ShapeDtypeStruct((t, PROJ_W), BF16), jax.ShapeDtypeStruct((t, DT_W), F32)),
        grid=(n_rows * N_COL + 1,),
        in_specs=[pl.BlockSpec((tm, D_MODEL), lambda s: (row(s), 0)),
                  pl.BlockSpec((1, 1, 6 * D_MODEL), lambda s: (row(s) // per_b, 0, 0)),
                  _const_spec((1, D_MODEL)),
                  pl.BlockSpec((tm, 1), lambda s: (row(s), 0)),
                  _const_spec((1, LANES)), _const_spec((1, LANES)),
                  pl.BlockSpec((D_MODEL, tn), wa_tile),
                  pl.BlockSpec((D_MODEL, tn), wb_tile),
                  _const_spec((D_MODEL, DT_W)),
                  _const_spec((SSD_CONV, CONV_DIM)),
                  _const_spec((1, CONV_DIM))],
        out_specs=(pl.BlockSpec((tm, tn), lagged),
                   pl.BlockSpec((tm, DT_W), lambda s: (row(s), 0))),
        scratch_shapes=[pltpu.VMEM((tm, D_MODEL), BF16),
                        pltpu.VMEM((tm, tn), F32),
                        pltpu.VMEM((tm, tn), F32),
                        pltpu.VMEM((tm, LANES), F32),
                        pltpu.VMEM((tm, LANES), F32),
                        pltpu.VMEM((HALO, CONV_DIM), F32)],
        compiler_params=_params(("arbitrary",)),
        name="in_proj",
    )(x2, mod3, nw, pos2, invf, sgn, w_all, w_tail, w_dt, convw_t, convb)


def _split2(v):
    hi = v.astype(BF16)
    lo = (v - hi.astype(F32)).astype(BF16)
    return hi, lo


def _split3(v):
    hi = v.astype(BF16)
    r = v - hi.astype(F32)
    mid = r.astype(BF16)
    lo = (r - mid.astype(F32)).astype(BF16)
    return hi, mid, lo


def _ssd_kernel(act_ref, zs_ref, dt_ref, dtb_ref, alog_ref, dskip_ref, normw_ref, expand_ref, tril_ref,
                y_ref, yacc_ref, state_ref):
    L = SSD_CHUNK
    H = L // 2

    @pl.when(pl.program_id(1) == 0)
    def _():
        state_ref[...] = jnp.zeros(state_ref.shape, F32)

    dt = dt_ref[...] + dtb_ref[...]
    dt = jnp.maximum(dt, 0.0) + jnp.log1p(jnp.exp(-jnp.abs(dt)))
    a = dt * (-LOG2E * jnp.exp(alog_ref[...]))
    tril = tril_ref[...]
    cs = sum(_dot(tril, p) for p in _split3(a))
    key_t = cs.T - jnp.log2(dt.T)
    cs_last = cs[L - 1:L, :]
    ecs = jnp.exp2(cs).astype(BF16)
    dtw = (dt * jnp.exp2(cs_last - cs)).astype(BF16)
    cd16 = jnp.broadcast_to(jnp.exp2(cs_last), (2 * SUBLANES, DT_W))
    cd_e = sum(_dot(p, expand_ref[...]) for p in _split3(cd16))[0:1, :]

    tri = lax.broadcasted_iota(jnp.int32, (H, H), 1) <= lax.broadcasted_iota(jnp.int32, (H, H), 0)
    first_head = lax.broadcasted_iota(jnp.int32, (L, LANES), 1) < SSD_HEAD_DIM

    for g in range(SSD_GROUPS):
        gs = slice(g * GROUP_CH, (g + 1) * GROUP_CH)
        b_g = act_ref[:, D_INNER + g * SSD_STATE:D_INNER + (g + 1) * SSD_STATE]
        c_g = act_ref[:, D_INNER + (SSD_GROUPS + g) * SSD_STATE:D_INNER + (SSD_GROUPS + g + 1) * SSD_STATE]
        b_gt = b_g.astype(F32).T.astype(BF16)
        sc_top = _dot(c_g[0:H, :], b_gt[:, 0:H])
        sc_bot = _dot(c_g[H:L, :], b_gt)
        st = state_ref[g]
        y_off = _dot(c_g, st.astype(BF16)) * _dot(ecs, expand_ref[:, gs])
        xw = (act_ref[:, gs].astype(F32) * _dot(dtw, expand_ref[:, gs])).astype(BF16)
        state_ref[g] = st * cd_e[:, gs] + _dot(b_gt, xw)
        for pair in range(GROUP_CH // LANES):
            h0 = g * (SSD_HEADS // SSD_GROUPS) + 2 * pair
            ps = slice(g * GROUP_CH + pair * LANES, g * GROUP_CH + (pair + 1) * LANES)
            xp = act_ref[:, ps]
            zero = jnp.zeros_like(xp)
            y_top = y_bot = None
            for sub, rhs in ((0, jnp.where(first_head, xp, zero)), (1, jnp.where(first_head, zero, xp))):
                h = h0 + sub
                col = cs[:, h:h + 1]
                rowv = key_t[h:h + 1, :]
                d_tl = jnp.exp2(jnp.where(tri, col[0:H] - rowv[:, 0:H], NEG_BIG))
                d_bl = jnp.exp2(col[H:L] - rowv[:, 0:H])
                d_br = jnp.exp2(jnp.where(tri, col[H:L] - rowv[:, H:L], NEG_BIG))
                m_top = (sc_top * d_tl).astype(BF16)
                m_bot = jnp.concatenate([(sc_bot[:, 0:H] * d_bl).astype(BF16),
                                         (sc_bot[:, H:L] * d_br).astype(BF16)], axis=1)
                p_top = _dot(m_top, rhs[0:H, :])
                p_bot = _dot(m_bot, rhs)
                y_top = p_top if y_top is None else y_top + p_top
                y_bot = p_bot if y_bot is None else y_bot + p_bot
            y_pair = jnp.concatenate([y_top, y_bot], axis=0)
            yacc_ref[:, ps] = (y_pair + y_off[:, pair * LANES:(pair + 1) * LANES]
                               + xp.astype(F32) * dskip_ref[:, ps])

    for g in range(SSD_GROUPS):
        gs = slice(g * GROUP_CH, (g + 1) * GROUP_CH)
        yg = yacc_ref[:, gs] * zs_ref[:, gs].astype(F32)
        y_ref[:, gs] = _rms(yg, normw_ref[:, gs]).astype(y_ref.dtype)


def _ssd(proj, dt_raw, dtb, alog, dskip_e, normw, expand, tril, batch, seq):
    t = proj.shape[0]
    L = SSD_CHUNK
    nc = seq // L
    row = lambda b, c: b * nc + c
    return pl.pallas_call(
        _ssd_kernel,
        out_shape=jax.ShapeDtypeStruct((t, D_INNER), BF16),
        grid=(batch, nc),
        in_specs=[pl.BlockSpec((L, CONV_DIM), lambda b, c: (row(b, c), XBC_OFF // CONV_DIM)),
                  pl.BlockSpec((L, D_INNER), lambda b, c: (row(b, c), Z_OFF // D_INNER)),
                  pl.BlockSpec((L, DT_W), lambda b, c: (row(b, c), 0)),
                  _const_spec((1, DT_W)), _const_spec((1, DT_W)),
                  _const_spec((1, D_INNER)), _const_spec((1, D_INNER)),
                  _const_spec((DT_W, D_INNER)), _const_spec((L, L))],
        out_specs=pl.BlockSpec((L, D_INNER), lambda b, c: (row(b, c), 0)),
        scratch_shapes=[pltpu.VMEM((L, D_INNER), F32),
                        pltpu.VMEM((SSD_GROUPS, SSD_STATE, GROUP_CH), F32)],
        compiler_params=_params(("parallel", "arbitrary")),
        name="ssd",
    )(proj, proj, dt_raw, dtb, alog, dskip_e, normw, expand, tril)


def _attn_kernel(q_ref, k_ref, v_ref, lq1_ref, lk1_ref, lq2_ref, lk2_ref, subln_ref, o_ref,
                 q2_ref, vaug_ref, s0_ref, s1_ref, s2_ref, m_ref, acc_ref, *, tq, tk, rb, lambda_init):
    assert tq == tk and tq % rb == 0
    seq = q_ref.shape[0]
    nq = seq // tq

    lane = lax.broadcasted_iota(jnp.int32, (tq, HEAD_W), 1)
    comp0 = lane < DIFF_HEAD_DIM
    for t in range(nq):
        q = q_ref[t * tq:(t + 1) * tq, :]
        zero = jnp.zeros_like(q)
        q2_ref[t, 0:tq, :] = jnp.where(comp0, q, zero)
        q2_ref[t, tq:2 * tq, :] = jnp.where(comp0, zero, q)
    vaug_ref[:, 0:HEAD_W] = v_ref[...]
    vaug_ref[:, HEAD_W:2 * HEAD_W] = jnp.ones((seq, HEAD_W), BF16)

    lam = (jnp.exp(jnp.sum(lq1_ref[...] * lk1_ref[...], axis=-1, keepdims=True))
           - jnp.exp(jnp.sum(lq2_ref[...] * lk2_ref[...], axis=-1, keepdims=True)) + lambda_init)

    def keys_needed(diag, r):
        return (r * rb) % tq + rb if diag else tk

    def scores_block(qi, j, s_ref, diag, r):
        rows = slice(r * rb, (r + 1) * rb)
        nk = keys_needed(diag, r)
        start = pl.multiple_of(j * tk, tk)
        s_ref[rows, 0:nk] = lax.dot_general(q2_ref[qi, rows, :], k_ref[pl.ds(start, nk), :],
                                            (((1,), (1,)), ((), ())), preferred_element_type=F32)

    def softmax_block(j, s_ref, diag, r):
        rows = slice(r * rb, (r + 1) * rb)
        nk = keys_needed(diag, r)
        start = pl.multiple_of(j * tk, tk)
        s = s_ref[rows, 0:nk]
        if diag:
            qpos = (r * rb) % tq + lax.broadcasted_iota(jnp.int32, (rb, nk), 0)
            s = jnp.where(lax.broadcasted_iota(jnp.int32, (rb, nk), 1) <= qpos, s, NEG_BIG)
        m_prev = m_ref[rows, :]
        m_next = jnp.maximum(m_prev, jnp.max(s, axis=1, keepdims=True))
        alpha = jnp.exp2(m_prev - m_next)
        p = jnp.exp2(s - jnp.concatenate([m_next] * (nk // LANES), axis=1)).astype(BF16)
        acc_ref[rows, :] = (jnp.concatenate([alpha, alpha], axis=1) * acc_ref[rows, :]
                            + _dot(p, vaug_ref[pl.ds(start, nk), :]))
        m_ref[rows, :] = m_next

    def scores(qi, j, s_ref, diag):
        for r in range(2 * tq // rb):
            scores_block(qi, j, s_ref, diag, r)

    def scores_softmax(nxt, cur):
        for r in range(2 * tq // rb):
            scores_block(*nxt, r)
            softmax_block(*cur, r)

    def finish(qi):
        o = (acc_ref[0:tq, 0:HEAD_W] / acc_ref[0:tq, HEAD_W:2 * HEAD_W]
             - lam * (acc_ref[tq:2 * tq, 0:HEAD_W] / acc_ref[tq:2 * tq, HEAD_W:2 * HEAD_W]))
        start = pl.multiple_of(qi * tq, tq)
        o_ref[pl.ds(start, tq), :] = (_rms(o, subln_ref[...]) * (1.0 - lambda_init)).astype(o_ref.dtype)

    scores(0, 0, s2_ref, True)

    def q_tile(qi, carry):
        m_ref[...] = jnp.full(m_ref.shape, NEG_BIG, F32)
        acc_ref[...] = jnp.zeros(acc_ref.shape, F32)

        def pair(i, c):
            scores_softmax((qi, 2 * i + 1, s1_ref, False), (2 * i, s0_ref, False))
            scores_softmax((qi, 2 * i + 2, s0_ref, False), (2 * i + 1, s1_ref, False))
            return c

        n_pairs = jnp.maximum(qi - 1, 0) // 2
        lax.fori_loop(0, n_pairs, pair, 0)

        @pl.when(qi % 2 == 1)
        def _():
            scores_softmax((qi, qi, s2_ref, True), (qi - 1, s0_ref, False))

        @pl.when((qi % 2 == 0) & (qi > 0))
        def _():
            scores_softmax((qi, qi - 1, s1_ref, False), (qi - 2, s0_ref, False))
            scores_softmax((qi, qi, s2_ref, True), (qi - 1, s1_ref, False))

        scores_softmax((jnp.minimum(qi + 1, nq - 1), 0, s0_ref, False), (qi, s2_ref, True))
        finish(qi)
        return carry

    lax.fori_loop(0, nq, q_tile, 0)


def _attn(proj, lq1, lk1, lq2, lk2, subln, batch, seq, lambda_init):
    t = proj.shape[0]
    tq, tk, rb = 512, 512, 256
    nq = seq // tq
    lam_spec = _const_spec((1, DIFF_HEAD_DIM))
    head_cols = lambda off: (lambda b, h: (b, off // HEAD_W + h))
    return pl.pallas_call(
        functools.partial(_attn_kernel, tq=tq, tk=tk, rb=rb, lambda_init=lambda_init),
        out_shape=jax.ShapeDtypeStruct((t, DIFF_V), BF16),
        grid=(batch, DIFF_HEADS),
        in_specs=[pl.BlockSpec((seq, HEAD_W), head_cols(Q_OFF)),
                  pl.BlockSpec((seq, HEAD_W), head_cols(K_OFF)),
                  pl.BlockSpec((seq, HEAD_W), head_cols(V_OFF)),
                  lam_spec, lam_spec, lam_spec, lam_spec, _const_spec((1, HEAD_W))],
        out_specs=pl.BlockSpec((seq, HEAD_W), lambda b, h: (b, h)),
        scratch_shapes=[pltpu.VMEM((nq, 2 * tq, HEAD_W), BF16),
                        pltpu.VMEM((seq, 2 * HEAD_W), BF16),
                        pltpu.VMEM((2 * tq, tk), F32),
                        pltpu.VMEM((2 * tq, tk), F32),
                        pltpu.VMEM((2 * tq, tk), F32),
                        pltpu.VMEM((2 * tq, LANES), F32),
                        pltpu.VMEM((2 * tq, 2 * HEAD_W), F32)],
        compiler_params=_params(("parallel", "parallel")),
        name="diff_attn",
    )(proj, proj, proj, lq1, lk1, lq2, lk2, subln)


def _merge_kernel(y_ref, o_ref, gate_ref, x_ref, mod_ref, wos_ref, woa_ref, wout_ref, nw_ref, out_ref):
    y_ssd = _dot(y_ref[...], wos_ref[...])
    y_attn = _dot(o_ref[...], woa_ref[...])
    g_ssd = gate_ref[:, 0:D_MODEL].astype(F32)
    g_attn = gate_ref[:, D_MODEL:2 * D_MODEL].astype(F32)
    merged = (g_ssd * y_ssd + g_attn * y_attn).astype(BF16)
    mix = _dot(merged, wout_ref[...])
    g1 = mod_ref[0][:, 2 * D_MODEL:3 * D_MODEL]
    out_ref[...] = x_ref[...] + g1 * _rms(mix, nw_ref[...])


def _merge(y, o, proj, x2, mod3, wos, woa, wout, nw, seq):
    t = x2.shape[0]
    tm = 512
    per_b = seq // tm
    return pl.pallas_call(
        _merge_kernel,
        out_shape=jax.ShapeDtypeStruct((t, D_MODEL), F32),
        grid=(t // tm,),
        in_specs=[pl.BlockSpec((tm, D_INNER), lambda i: (i, 0)),
                  pl.BlockSpec((tm, DIFF_V), lambda i: (i, 0)),
                  pl.BlockSpec((tm, 2 * D_MODEL), lambda i: (i, GATE_OFF // (2 * D_MODEL))),
                  pl.BlockSpec((tm, D_MODEL), lambda i: (i, 0)),
                  pl.BlockSpec((1, 1, 6 * D_MODEL), lambda i: (i // per_b, 0, 0)),
                  _const_spec((D_INNER, D_MODEL)), _const_spec((DIFF_V, D_MODEL)),
                  _const_spec((D_MODEL, D_MODEL)), _const_spec((1, D_MODEL))],
        out_specs=pl.BlockSpec((tm, D_MODEL), lambda i: (i, 0)),
        compiler_params=_params(("parallel",)),
        name="merge",
    )(y, o, proj, x2, mod3, wos, woa, wout, nw)


def _ffn_kernel(x_ref, mod_ref, npre_ref, npost_ref, wg_ref, wu_ref, wd_ref, out_ref):
    mod = mod_ref[0]
    x = x_ref[...]
    h = (_rms(x, npre_ref[...]) * (1.0 + mod[:, 4 * D_MODEL:5 * D_MODEL]) + mod[:, 3 * D_MODEL:4 * D_MODEL]).astype(BF16)
    act = (_silu(_dot(h, wg_ref[...])) * _dot(h, wu_ref[...])).astype(BF16)
    f = _dot(act, wd_ref[...])
    out_ref[...] = x + mod[:, 5 * D_MODEL:6 * D_MODEL] * _rms(f, npost_ref[...])


def _ffn(x1, mod3, npre, npost, wg, wu, wd, seq):
    t = x1.shape[0]
    tm = 512
    per_b = seq // tm
    return pl.pallas_call(
        _ffn_kernel,
        out_shape=jax.ShapeDtypeStruct((t, D_MODEL), F32),
        grid=(t // tm,),
        in_specs=[pl.BlockSpec((tm, D_MODEL), lambda i: (i, 0)),
                  pl.BlockSpec((1, 1, 6 * D_MODEL), lambda i: (i // per_b, 0, 0)),
                  _const_spec((1, D_MODEL)), _const_spec((1, D_MODEL)),
                  _const_spec((D_MODEL, D_FF)), _const_spec((D_MODEL, D_FF)), _const_spec((D_FF, D_MODEL))],
        out_specs=pl.BlockSpec((tm, D_MODEL), lambda i: (i, 0)),
        compiler_params=_params(("parallel",)),
        name="ffn",
    )(x1, mod3, npre, npost, wg, wu, wd)


def _layer(x2, mod3, pos2, batch, seq, lambda_init, norm_pre_mix, norm_post_mix, norm_pre_ffn, norm_post_ffn,
           w_in, conv_w, conv_b, dt_bias, a_log, d_skip, ssd_norm, w_o_ssd,
           lambda_q1, lambda_k1, lambda_q2, lambda_k2, subln, w_o_attn, w_out, w_gate, w_up, w_down):
    row = lambda v: v.reshape(1, -1).astype(F32)
    w_tail = w_in[:, W_TAIL_OFF:].astype(BF16)
    w_dt = jnp.pad(w_in[:, W_TAIL_OFF - SSD_HEADS:W_TAIL_OFF], ((0, 0), (0, DT_W - SSD_HEADS))).astype(BF16)
    inv_freq = 1.0 / (ROPE_THETA ** (jnp.arange(0, DIFF_HEAD_DIM, 2, dtype=F32) / DIFF_HEAD_DIM))
    invf = jnp.tile(inv_freq, LANES // inv_freq.shape[0]).reshape(1, LANES)
    half_lane = jnp.arange(LANES) % DIFF_HEAD_DIM < DIFF_HEAD_DIM // 2
    sgn = jnp.where(half_lane, -1.0, 1.0).astype(F32).reshape(1, LANES)

    proj, dt_raw = _in_proj(x2, mod3, row(norm_pre_mix), pos2, invf, sgn, w_in, w_tail, w_dt,
                            conv_w.T.astype(F32), row(conv_b), seq)

    pad_h = lambda v: jnp.pad(v.astype(F32), (0, DT_W - SSD_HEADS)).reshape(1, DT_W)
    expand = (jnp.arange(DT_W)[:, None] == (jnp.arange(D_INNER)[None, :] // SSD_HEAD_DIM)).astype(BF16)
    tril = (jnp.arange(SSD_CHUNK)[None, :] <= jnp.arange(SSD_CHUNK)[:, None]).astype(BF16)
    y = _ssd(proj, dt_raw, pad_h(dt_bias), pad_h(a_log), row(jnp.repeat(d_skip, SSD_HEAD_DIM)), row(ssd_norm),
             expand, tril, batch, seq)

    o = _attn(proj, row(lambda_q1), row(lambda_k1), row(lambda_q2), row(lambda_k2), row(subln),
              batch, seq, lambda_init)

    x1 = _merge(y, o, proj, x2, mod3, w_o_ssd.astype(BF16), w_o_attn.astype(BF16), w_out.astype(BF16),
                row(norm_post_mix), seq)
    return _ffn(x1, mod3, row(norm_pre_ffn), row(norm_post_ffn), w_gate.astype(BF16), w_up.astype(BF16),
                w_down.astype(BF16), seq)


def kernel(x, c, positions, w_ada, b_ada, norm_pre_mix, norm_post_mix, norm_pre_ffn, norm_post_ffn, w_in, conv_w, conv_b, dt_bias, a_log, d_skip, ssd_norm, w_o_ssd, lambda_q1, lambda_k1, lambda_q2, lambda_k2, subln, w_o_attn, w_out, w_gate, w_up, w_down):
    batch, seq, _ = x.shape
    depth = w_in.shape[0]
    x2 = x.reshape(batch * seq, D_MODEL)
    pos2 = positions.reshape(batch * seq, 1)
    c_pad = jnp.pad(c, ((0, SUBLANES - batch), (0, 0)))
    for layer in range(depth):
        lambda_init = 0.8 - 0.6 * math.exp(-0.3 * layer)
        mod3 = _ada(c_pad, w_ada[layer], b_ada[layer].reshape(1, -1))[:batch].reshape(batch, 1, 6 * D_MODEL)
        x2 = _layer(x2, mod3, pos2, batch, seq, lambda_init, norm_pre_mix[layer], norm_post_mix[layer],
                    norm_pre_ffn[layer], norm_post_ffn[layer], w_in[layer], conv_w[layer], conv_b[layer],
                    dt_bias[layer], a_log[layer], d_skip[layer], ssd_norm[layer], w_o_ssd[layer],
                    lambda_q1[layer], lambda_k1[layer], lambda_q2[layer], lambda_k2[layer], subln[layer],
                    w_o_attn[layer], w_out[layer], w_gate[layer], w_up[layer], w_down[layer])
    return x2.reshape(batch, seq, D_MODEL)
```

```python
import functools
import math

import jax
import jax.numpy as jnp
import numpy as np
from jax import lax
from jax.experimental import pallas as pl
from jax.experimental.pallas import tpu as pltpu

F32 = jnp.float32
BF16 = jnp.bfloat16

D_MODEL = 1024
D_INNER = 2048
SSD_HEAD_DIM = 64
SSD_HEADS = 32
SSD_GROUPS = 4
SSD_STATE = 128
SSD_CONV = 4
SSD_CHUNK = 256
CONV_DIM = D_INNER + 2 * SSD_GROUPS * SSD_STATE
GROUP_CH = D_INNER // SSD_GROUPS
DIFF_HEADS = 8
DIFF_HEAD_DIM = 64
DIFF_QK = 1024
DIFF_V = 1024
ROPE_THETA = 10000.0
D_FF = 2816
EPS = 1e-6
IN_SIZES = (D_INNER, CONV_DIM, SSD_HEADS, DIFF_QK, DIFF_QK, DIFF_V, 2 * D_MODEL)
IN_SPLITS = tuple(int(v) for v in np.cumsum(IN_SIZES)[:-1])

LANES = 128
SUBLANES = 8
HEAD_W = 2 * DIFF_HEAD_DIM

XBC_OFF, V_OFF, Z_OFF, GATE_OFF, K_OFF, Q_OFF = 0, 3072, 4096, 6144, 8192, 9216
PROJ_W = 10240
PROJ_TN = 1024
DT_W = LANES

VMEM_LIMIT = 56 * 1024 * 1024

LOG2E = math.log2(math.e)
Q_SCALE = DIFF_HEAD_DIM ** -0.5 * LOG2E
NEG_BIG = -1e30


def _sigmoid(v):
    return 1.0 / (1.0 + jnp.exp(-v))


def _silu(v):
    return v * _sigmoid(v)


def _rms(v, w):
    return v * lax.rsqrt(jnp.mean(v * v, axis=-1, keepdims=True) + EPS) * w


def _dot(a, b):
    return jnp.dot(a, b, preferred_element_type=F32)


def _params(sem):
    return pltpu.CompilerParams(dimension_semantics=sem, vmem_limit_bytes=VMEM_LIMIT)


def _const_spec(shape):
    nd = len(shape)
    return pl.BlockSpec(shape, lambda *_: (0,) * nd, pipeline_mode=pl.Buffered(1))


def _ada_kernel(c_ref, w_ref, b_ref, o_ref):
    cond = _silu(c_ref[...])
    o_ref[...] = _dot(cond.astype(BF16), w_ref[...].astype(BF16)) + b_ref[...]


def _ada(c_pad, w_ada, b_ada):
    n = w_ada.shape[1]
    tn = 1024
    return pl.pallas_call(
        _ada_kernel,
        out_shape=jax.ShapeDtypeStruct((c_pad.shape[0], n), F32),
        grid=(n // tn,),
        in_specs=[pl.BlockSpec(c_pad.shape, lambda j: (0, 0)),
                  pl.BlockSpec((D_MODEL, tn), lambda j: (0, j)),
                  pl.BlockSpec((1, tn), lambda j: (0, j))],
        out_specs=pl.BlockSpec((c_pad.shape[0], tn), lambda j: (0, j)),
        compiler_params=_params(("arbitrary",)),
        name="ada",
    )(c_pad, w_ada, b_ada)


COL_KINDS = ("xbc", "xbc", "xbc", "v", "z", "z", "gate", "gate", "k", "q")
N_COL = len(COL_KINDS)
TRIG_COL = 4
EPI_RB = 128
TRIG_RB = 64
MM_ROWS, MM_COLS = 512, 256
HALO = SUBLANES


def _sigmoid_t(v):
    return 0.5 + 0.5 * jnp.tanh(0.5 * v)


def _silu_t(v):
    h = 0.5 * v
    return h + h * jnp.tanh(h)


def _inproj_kernel(x_ref, mod_ref, nw_ref, pos_ref, invf_ref, sgn_ref, w_ref, wdt_ref, convw_ref, convb_ref,
                   out_ref, dt_ref, h_ref, park0_ref, park1_ref, cos_ref, sin_ref, halo_ref,
                   *, tiles_per_seq, n_row_tiles):
    i, j = pl.program_id(0), pl.program_id(1)
    tm = x_ref.shape[0]
    parks = (park0_ref, park1_ref)

    @pl.when((j == 0) & (i < n_row_tiles))
    def _():
        mod = mod_ref[0]
        h = _rms(x_ref[...], nw_ref[...]) * (1.0 + mod[:, D_MODEL:2 * D_MODEL]) + mod[:, 0:D_MODEL]
        hb = h.astype(BF16)
        h_ref[...] = hb
        dt_ref[...] = _dot(hb, wdt_ref[...])

    def matmul_pieces(col):
        park = parks[col % 2]

        def piece(rows, sl):
            def emit():
                park[rows, sl] = _dot(h_ref[rows, :], w_ref[:, col * PROJ_TN + sl.start:col * PROJ_TN + sl.stop])
            return emit

        return [piece(slice(m * MM_ROWS, (m + 1) * MM_ROWS), slice(n * MM_COLS, (n + 1) * MM_COLS))
                for n in range(PROJ_TN // MM_COLS) for m in range(tm // MM_ROWS)]

    def conv_piece(col, park, blk, slab):
        def emit():
            lo = blk * EPI_RB
            sl = slice(slab * LANES, (slab + 1) * LANES)
            cols = slice(col * PROJ_TN + slab * LANES, col * PROJ_TN + (slab + 1) * LANES)
            w = convw_ref[:, cols]
            b = convb_ref[:, cols]
            lead = jnp.where(first_tile, 0.0, halo_ref[:, cols]) if blk == 0 else park[lo - HALO:lo, sl]
            ext = jnp.concatenate([lead, park[lo:lo + EPI_RB, sl]], axis=0)
            conv = b + w[SSD_CONV - 1:SSD_CONV, :] * ext
            for s in range(1, SSD_CONV):
                conv = conv + w[SSD_CONV - 1 - s:SSD_CONV - s, :] * pltpu.roll(ext, s, 0)
            out_ref[lo:lo + EPI_RB, sl] = _silu_t(conv[HALO:, :]).astype(BF16)
            if blk == tm // EPI_RB - 1:
                halo_ref[:, cols] = park[tm - HALO:tm, sl]
        return emit

    def rope_piece(park, scale, blk, hd):
        def emit():
            lane = lax.broadcasted_iota(jnp.int32, (EPI_RB, HEAD_W), 1)
            first_half = (lane % DIFF_HEAD_DIM) < (DIFF_HEAD_DIM // 2)
            rows = slice(blk * EPI_RB, (blk + 1) * EPI_RB)
            sl = slice(hd * HEAD_W, (hd + 1) * HEAD_W)
            xh = park[rows, sl]
            partner = jnp.where(first_half, pltpu.roll(xh, HEAD_W - DIFF_HEAD_DIM // 2, 1),
                                pltpu.roll(xh, DIFF_HEAD_DIM // 2, 1))
            out_ref[rows, sl] = ((xh * cos_ref[rows, :] + partner * sin_ref[rows, :]) * scale).astype(BF16)
        return emit

    def map_piece(park, fn, blk, slab):
        def emit():
            rows = slice(blk * EPI_RB, (blk + 1) * EPI_RB)
            sl = slice(slab * LANES, (slab + 1) * LANES)
            out_ref[rows, sl] = fn(park[rows, sl]).astype(BF16)
        return emit

    def trig_piece(blk):
        def emit():
            quarter, run = tm // 4, DIFF_HEAD_DIM // 2
            lane = lax.broadcasted_iota(jnp.int32, (TRIG_RB, LANES), 1)
            pos = [pos_ref[g * quarter + blk * TRIG_RB:g * quarter + (blk + 1) * TRIG_RB, :].astype(F32)
                   for g in range(4)]
            compact = jnp.where(lane < run, pos[0], jnp.where(lane < 2 * run, pos[1],
                                                              jnp.where(lane < 3 * run, pos[2], pos[3])))
            ang = compact * invf_ref[...]
            for table, ref in ((jnp.cos(ang), cos_ref), (jnp.sin(ang), sin_ref)):
                for g in range(4):
                    base = table if g == 0 else pltpu.roll(table, LANES - g * run, 1)
                    t = jnp.where(lane < run, base, pltpu.roll(base, run, 1))
                    t = jnp.where(lane < 2 * run, t, pltpu.roll(t, 2 * run, 1))
                    rows = slice(g * quarter + blk * TRIG_RB, g * quarter + (blk + 1) * TRIG_RB)
                    ref[rows, :] = t * sgn_ref[...] if ref is sin_ref else t
        return emit

    def epilogue_pieces(col):
        kind, park = COL_KINDS[col], parks[col % 2]
        grid = [(blk, slab) for blk in range(tm // EPI_RB) for slab in range(PROJ_TN // LANES)]
        if kind == "xbc":
            return [conv_piece(col, park, blk, slab) for blk, slab in grid]
        if kind in ("k", "q"):
            return [rope_piece(park, Q_SCALE if kind == "q" else 1.0, blk, hd) for blk, hd in grid]
        fn = {"v": lambda r: r, "z": _silu_t, "gate": _sigmoid_t}[kind]
        return [map_piece(park, fn, blk, slab) for blk, slab in grid]

    def fused(mm_col, ep_col, trig=False):
        mm = matmul_pieces(mm_col) if mm_col is not None else []
        ep = epilogue_pieces(ep_col) if ep_col is not None else []
        if trig:
            ep = ep + [trig_piece(blk) for blk in range(tm // 4 // TRIG_RB)]
        n = max(len(mm), 1)
        for idx in range(n):
            for emit in mm[idx:idx + 1] + ep[idx * len(ep) // n:(idx + 1) * len(ep) // n]:
                emit()

    first_tile = i % tiles_per_seq == 0

    @pl.when((i == 0) & (j == 0))
    def _():
        halo_ref[...] = jnp.zeros(halo_ref.shape, F32)
        fused(0, None)

    @pl.when((j == 0) & (i > 0) & (i < n_row_tiles))
    def _():
        fused(0, N_COL - 1)

    @pl.when((j == 0) & (i == n_row_tiles))
    def _():
        fused(None, N_COL - 1)

    for col in range(1, N_COL):
        @pl.when((j == col) & (i < n_row_tiles))
        def _(col=col):
            fused(col, col - 1, trig=col == TRIG_COL)


def _in_proj(x2, mod3, nw, pos2, invf, sgn, w_wide, w_dt, convw_t, convb, seq):
    t = x2.shape[0]
    tm, tn = 1024, PROJ_TN
    n_rows = t // tm
    per_b = seq // tm
    row = lambda i, j: jnp.minimum(i, n_rows - 1)
    lagged = lambda i, j: (jnp.clip(jnp.where(j == 0, i - 1, i), 0, n_rows - 1),
                           jnp.where((j == 0) | (i == n_rows), jnp.where(i == 0, 0, N_COL - 1), j - 1))
    return pl.pallas_call(
        functools.partial(_inproj_kernel, tiles_per_seq=per_b, n_row_tiles=n_rows),
        out_shape=(jax.ShapeDtypeStruct((t, PROJ_W), BF16), jax.ShapeDtypeStruct((t, DT_W), F32)),
        grid=(n_rows + 1, N_COL),
        in_specs=[pl.BlockSpec((tm, D_MODEL), lambda i, j: (row(i, j), 0)),
                  pl.BlockSpec((1, 1, 6 * D_MODEL), lambda i, j: (row(i, j) // per_b, 0, 0)),
                  _const_spec((1, D_MODEL)),
                  pl.BlockSpec((tm, 1), lambda i, j: (row(i, j), 0)),
                  _const_spec((1, LANES)), _const_spec((1, LANES)),
                  _const_spec((D_MODEL, PROJ_W)),
                  _const_spec((D_MODEL, DT_W)),
                  _const_spec((SSD_CONV, CONV_DIM)),
                  _const_spec((1, CONV_DIM))],
        out_specs=(pl.BlockSpec((tm, tn), lagged),
                   pl.BlockSpec((tm, DT_W), lambda i, j: (row(i, j), 0))),
        scratch_shapes=[pltpu.VMEM((tm, D_MODEL), BF16),
                        pltpu.VMEM((tm, tn), F32),
                        pltpu.VMEM((tm, tn), F32),
                        pltpu.VMEM((tm, LANES), F32),
                        pltpu.VMEM((tm, LANES), F32),
                        pltpu.VMEM((HALO, CONV_DIM), F32)],
        compiler_params=_params(("arbitrary", "arbitrary")),
        name="in_proj",
    )(x2, mod3, nw, pos2, invf, sgn, w_wide, w_dt, convw_t, convb)


def _split2(v):
    hi = v.astype(BF16)
    lo = (v - hi.astype(F32)).astype(BF16)
    return hi, lo


def _split3(v):
    hi = v.astype(BF16)
    r = v - hi.astype(F32)
    mid = r.astype(BF16)
    lo = (r - mid.astype(F32)).astype(BF16)
    return hi, mid, lo


def _ssd_kernel(act_ref, zs_ref, dt_ref, dtb_ref, alog_ref, dskip_ref, normw_ref, expand_ref, tril_ref,
                y_ref, yacc_ref, state_ref):
    L = SSD_CHUNK
    H = L // 2

    @pl.when(pl.program_id(1) == 0)
    def _():
        state_ref[...] = jnp.zeros(state_ref.shape, F32)

    dt = dt_ref[...] + dtb_ref[...]
    dt = jnp.maximum(dt, 0.0) + jnp.log1p(jnp.exp(-jnp.abs(dt)))
    a = dt * (-LOG2E * jnp.exp(alog_ref[...]))
    tril = tril_ref[...]
    cs = sum(_dot(tril, p) for p in _split3(a))
    key_t = cs.T - jnp.log2(dt.T)
    cs_last = cs[L - 1:L, :]
    ecs = jnp.exp2(cs).astype(BF16)
    dtw = (dt * jnp.exp2(cs_last - cs)).astype(BF16)
    cd16 = jnp.broadcast_to(jnp.exp2(cs_last), (2 * SUBLANES, DT_W))
    cd_e = sum(_dot(p, expand_ref[...]) for p in _split3(cd16))[0:1, :]

    tri = lax.broadcasted_iota(jnp.int32, (H, H), 1) <= lax.broadcasted_iota(jnp.int32, (H, H), 0)
    first_head = lax.broadcasted_iota(jnp.int32, (L, LANES), 1) < SSD_HEAD_DIM

    for g in range(SSD_GROUPS):
        gs = slice(g * GROUP_CH, (g + 1) * GROUP_CH)
        b_g = act_ref[:, D_INNER + g * SSD_STATE:D_INNER + (g + 1) * SSD_STATE]
        c_g = act_ref[:, D_INNER + (SSD_GROUPS + g) * SSD_STATE:D_INNER + (SSD_GROUPS + g + 1) * SSD_STATE]
        b_gt = b_g.astype(F32).T.astype(BF16)
        sc_top = _dot(c_g[0:H, :], b_gt[:, 0:H])
        sc_bot = _dot(c_g[H:L, :], b_gt)
        st = state_ref[g]
        y_off = _dot(c_g, st.astype(BF16)) * _dot(ecs, expand_ref[:, gs])
        xw = (act_ref[:, gs].astype(F32) * _dot(dtw, expand_ref[:, gs])).astype(BF16)
        state_ref[g] = st * cd_e[:, gs] + _dot(b_gt, xw)
        for pair in range(GROUP_CH // LANES):
            h0 = g * (SSD_HEADS // SSD_GROUPS) + 2 * pair
            ps = slice(g * GROUP_CH + pair * LANES, g * GROUP_CH + (pair + 1) * LANES)
            xp = act_ref[:, ps]
            zero = jnp.zeros_like(xp)
            y_top = y_bot = None
            for sub, rhs in ((0, jnp.where(first_head, xp, zero)), (1, jnp.where(first_head, zero, xp))):
                h = h0 + sub
                col = cs[:, h:h + 1]
                rowv = key_t[h:h + 1, :]
                d_tl = jnp.exp2(jnp.where(tri, col[0:H] - rowv[:, 0:H], NEG_BIG))
                d_bl = jnp.exp2(col[H:L] - rowv[:, 0:H])
                d_br = jnp.exp2(jnp.where(tri, col[H:L] - rowv[:, H:L], NEG_BIG))
                m_top = (sc_top * d_tl).astype(BF16)
                m_bot = jnp.concatenate([(sc_bot[:, 0:H] * d_bl).astype(BF16),
                                         (sc_bot[:, H:L] * d_br).astype(BF16)], axis=1)
                p_top = _dot(m_top, rhs[0:H, :])
                p_bot = _dot(m_bot, rhs)
                y_top = p_top if y_top is None else y_top + p_top
                y_bot = p_bot if y_bot is None else y_bot + p_bot
            y_pair = jnp.concatenate([y_top, y_bot], axis=0)
            yacc_ref[:, ps] = (y_pair + y_off[:, pair * LANES:(pair + 1) * LANES]
                               + xp.astype(F32) * dskip_ref[:, ps])

    for g in range(SSD_GROUPS):
        gs = slice(g * GROUP_CH, (g + 1) * GROUP_CH)
        yg = yacc_ref[:, gs] * zs_ref[:, gs].astype(F32)
        y_ref[:, gs] = _rms(yg, normw_ref[:, gs]).astype(y_ref.dtype)


def _ssd(proj, dt_raw, dtb, alog, dskip_e, normw, expand, tril, batch, seq):
    t = proj.shape[0]
    L = SSD_CHUNK
    nc = seq // L
    row = lambda b, c: b * nc + c
    return pl.pallas_call(
        _ssd_kernel,
        out_shape=jax.ShapeDtypeStruct((t, D_INNER), BF16),
        grid=(batch, nc),
        in_specs=[pl.BlockSpec((L, CONV_DIM), lambda b, c: (row(b, c), XBC_OFF // CONV_DIM)),
                  pl.BlockSpec((L, D_INNER), lambda b, c: (row(b, c), Z_OFF // D_INNER)),
                  pl.BlockSpec((L, DT_W), lambda b, c: (row(b, c), 0)),
                  _const_spec((1, DT_W)), _const_spec((1, DT_W)),
                  _const_spec((1, D_INNER)), _const_spec((1, D_INNER)),
                  _const_spec((DT_W, D_INNER)), _const_spec((L, L))],
        out_specs=pl.BlockSpec((L, D_INNER), lambda b, c: (row(b, c), 0)),
        scratch_shapes=[pltpu.VMEM((L, D_INNER), F32),
                        pltpu.VMEM((SSD_GROUPS, SSD_STATE, GROUP_CH), F32)],
        compiler_params=_params(("parallel", "arbitrary")),
        name="ssd",
    )(proj, proj, dt_raw, dtb, alog, dskip_e, normw, expand, tril)


def _attn_kernel(q_ref, k_ref, v_ref, lq1_ref, lk1_ref, lq2_ref, lk2_ref, subln_ref, o_ref,
                 q2_ref, vaug_ref, s0_ref, s1_ref, s2_ref, m_ref, acc_ref, *, tq, tk, rb, lambda_init):
    assert tq == tk and tq % rb == 0
    seq = q_ref.shape[0]
    nq = seq // tq

    lane = lax.broadcasted_iota(jnp.int32, (tq, HEAD_W), 1)
    comp0 = lane < DIFF_HEAD_DIM
    for t in range(nq):
        q = q_ref[t * tq:(t + 1) * tq, :]
        zero = jnp.zeros_like(q)
        q2_ref[t, 0:tq, :] = jnp.where(comp0, q, zero)
        q2_ref[t, tq:2 * tq, :] = jnp.where(comp0, zero, q)
    vaug_ref[:, 0:HEAD_W] = v_ref[...]
    vaug_ref[:, HEAD_W:2 * HEAD_W] = jnp.ones((seq, HEAD_W), BF16)

    lam = (jnp.exp(jnp.sum(lq1_ref[...] * lk1_ref[...], axis=-1, keepdims=True))
           - jnp.exp(jnp.sum(lq2_ref[...] * lk2_ref[...], axis=-1, keepdims=True)) + lambda_init)

    def keys_needed(diag, r):
        return (r * rb) % tq + rb if diag else tk

    def scores_block(qi, j, s_ref, diag, r):
        rows = slice(r * rb, (r + 1) * rb)
        nk = keys_needed(diag, r)
        start = pl.multiple_of(j * tk, tk)
        s_ref[rows, 0:nk] = lax.dot_general(q2_ref[qi, rows, :], k_ref[pl.ds(start, nk), :],
                                            (((1,), (1,)), ((), ())), preferred_element_type=F32)

    def softmax_block(j, s_ref, diag, r):
        rows = slice(r * rb, (r + 1) * rb)
        nk = keys_needed(diag, r)
        start = pl.multiple_of(j * tk, tk)
        s = s_ref[rows, 0:nk]
        if diag:
            qpos = (r * rb) % tq + lax.broadcasted_iota(jnp.int32, (rb, nk), 0)
            s = jnp.where(lax.broadcasted_iota(jnp.int32, (rb, nk), 1) <= qpos, s, NEG_BIG)
        m_prev = m_ref[rows, :]
        m_next = jnp.maximum(m_prev, jnp.max(s, axis=1, keepdims=True))
        alpha = jnp.exp2(m_prev - m_next)
        p = jnp.exp2(s - jnp.concatenate([m_next] * (nk // LANES), axis=1)).astype(BF16)
        acc_ref[rows, :] = (jnp.concatenate([alpha, alpha], axis=1) * acc_ref[rows, :]
                            + _dot(p, vaug_ref[pl.ds(start, nk), :]))
        m_ref[rows, :] = m_next

    def scores(qi, j, s_ref, diag):
        for r in range(2 * tq // rb):
            scores_block(qi, j, s_ref, diag, r)

    def scores_softmax(nxt, cur):
        for r in range(2 * tq // rb):
            scores_block(*nxt, r)
            softmax_block(*cur, r)

    def finish(qi):
        o = (acc_ref[0:tq, 0:HEAD_W] / acc_ref[0:tq, HEAD_W:2 * HEAD_W]
             - lam * (acc_ref[tq:2 * tq, 0:HEAD_W] / acc_ref[tq:2 * tq, HEAD_W:2 * HEAD_W]))
        start = pl.multiple_of(qi * tq, tq)
        o_ref[pl.ds(start, tq), :] = (_rms(o, subln_ref[...]) * (1.0 - lambda_init)).astype(o_ref.dtype)

    scores(0, 0, s2_ref, True)

    def q_tile(qi, carry):
        m_ref[...] = jnp.full(m_ref.shape, NEG_BIG, F32)
        acc_ref[...] = jnp.zeros(acc_ref.shape, F32)

        def pair(i, c):
            scores_softmax((qi, 2 * i + 1, s1_ref, False), (2 * i, s0_ref, False))
            scores_softmax((qi, 2 * i + 2, s0_ref, False), (2 * i + 1, s1_ref, False))
            return c

        n_pairs = jnp.maximum(qi - 1, 0) // 2
        lax.fori_loop(0, n_pairs, pair, 0)

        @pl.when(qi % 2 == 1)
        def _():
            scores_softmax((qi, qi, s2_ref, True), (qi - 1, s0_ref, False))

        @pl.when((qi % 2 == 0) & (qi > 0))
        def _():
            scores_softmax((qi, qi - 1, s1_ref, False), (qi - 2, s0_ref, False))
            scores_softmax((qi, qi, s2_ref, True), (qi - 1, s1_ref, False))

        scores_softmax((jnp.minimum(qi + 1, nq - 1), 0, s0_ref, False), (qi, s2_ref, True))
        finish(qi)
        return carry

    lax.fori_loop(0, nq, q_tile, 0)


def _attn(proj, lq1, lk1, lq2, lk2, subln, batch, seq, lambda_init):
    t = proj.shape[0]
    tq, tk, rb = 512, 512, 256
    nq = seq // tq
    lam_spec = _const_spec((1, DIFF_HEAD_DIM))
    head_cols = lambda off: (lambda b, h: (b, off // HEAD_W + h))
    return pl.pallas_call(
        functools.partial(_attn_kernel, tq=tq, tk=tk, rb=rb, lambda_init=lambda_init),
        out_shape=jax.ShapeDtypeStruct((t, DIFF_V), BF16),
        grid=(batch, DIFF_HEADS),
        in_specs=[pl.BlockSpec((seq, HEAD_W), head_cols(Q_OFF)),
                  pl.BlockSpec((seq, HEAD_W), head_cols(K_OFF)),
                  pl.BlockSpec((seq, HEAD_W), head_cols(V_OFF)),
                  lam_spec, lam_spec, lam_spec, lam_spec, _const_spec((1, HEAD_W))],
        out_specs=pl.BlockSpec((seq, HEAD_W), lambda b, h: (b, h)),
        scratch_shapes=[pltpu.VMEM((nq, 2 * tq, HEAD_W), BF16),
                        pltpu.VMEM((seq, 2 * HEAD_W), BF16),
                        pltpu.VMEM((2 * tq, tk), F32),
                        pltpu.VMEM((2 * tq, tk), F32),
                        pltpu.VMEM((2 * tq, tk), F32),
                        pltpu.VMEM((2 * tq, LANES), F32),
                        pltpu.VMEM((2 * tq, 2 * HEAD_W), F32)],
        compiler_params=_params(("parallel", "parallel")),
        name="diff_attn",
    )(proj, proj, proj, lq1, lk1, lq2, lk2, subln)


def _merge_kernel(y_ref, o_ref, gate_ref, x_ref, mod_ref, wos_ref, woa_ref, wout_ref, nw_ref, out_ref):
    y_ssd = _dot(y_ref[...], wos_ref[...])
    y_attn = _dot(o_ref[...], woa_ref[...])
    g_ssd = gate_ref[:, 0:D_MODEL].astype(F32)
    g_attn = gate_ref[:, D_MODEL:2 * D_MODEL].astype(F32)
    merged = (g_ssd * y_ssd + g_attn * y_attn).astype(BF16)
    mix = _dot(merged, wout_ref[...])
    g1 = mod_ref[0][:, 2 * D_MODEL:3 * D_MODEL]
    out_ref[...] = x_ref[...] + g1 * _rms(mix, nw_ref[...])


def _merge(y, o, proj, x2, mod3, wos, woa, wout, nw, seq):
    t = x2.shape[0]
    tm = 512
    per_b = seq // tm
    return pl.pallas_call(
        _merge_kernel,
        out_shape=jax.ShapeDtypeStruct((t, D_MODEL), F32),
        grid=(t // tm,),
        in_specs=[pl.BlockSpec((tm, D_INNER), lambda i: (i, 0)),
                  pl.BlockSpec((tm, DIFF_V), lambda i: (i, 0)),
                  pl.BlockSpec((tm, 2 * D_MODEL), lambda i: (i, GATE_OFF // (2 * D_MODEL))),
                  pl.BlockSpec((tm, D_MODEL), lambda i: (i, 0)),
                  pl.BlockSpec((1, 1, 6 * D_MODEL), lambda i: (i // per_b, 0, 0)),
                  _const_spec((D_INNER, D_MODEL)), _const_spec((DIFF_V, D_MODEL)),
                  _const_spec((D_MODEL, D_MODEL)), _const_spec((1, D_MODEL))],
        out_specs=pl.BlockSpec((tm, D_MODEL), lambda i: (i, 0)),
        compiler_params=_params(("parallel",)),
        name="merge",
    )(y, o, proj, x2, mod3, wos, woa, wout, nw)


def _ffn_kernel(x_ref, mod_ref, npre_ref, npost_ref, wg_ref, wu_ref, wd_ref, out_ref):
    mod = mod_ref[0]
    x = x_ref[...]
    h = (_rms(x, npre_ref[...]) * (1.0 + mod[:, 4 * D_MODEL:5 * D_MODEL]) + mod[:, 3 * D_MODEL:4 * D_MODEL]).astype(BF16)
    act = (_silu(_dot(h, wg_ref[...])) * _dot(h, wu_ref[...])).astype(BF16)
    f = _dot(act, wd_ref[...])
    out_ref[...] = x + mod[:, 5 * D_MODEL:6 * D_MODEL] * _rms(f, npost_ref[...])


def _ffn(x1, mod3, npre, npost, wg, wu, wd, seq):
    t = x1.shape[0]
    tm = 512
    per_b = seq // tm
    return pl.pallas_call(
        _ffn_kernel,
        out_shape=jax.ShapeDtypeStruct((t, D_MODEL), F32),
        grid=(t // tm,),
        in_specs=[pl.BlockSpec((tm, D_MODEL), lambda i: (i, 0)),
                  pl.BlockSpec((1, 1, 6 * D_MODEL), lambda i: (i // per_b, 0, 0)),
                  _const_spec((1, D_MODEL)), _const_spec((1, D_MODEL)),
                  _const_spec((D_MODEL, D_FF)), _const_spec((D_MODEL, D_FF)), _const_spec((D_FF, D_MODEL))],
        out_specs=pl.BlockSpec((tm, D_MODEL), lambda i: (i, 0)),
        compiler_params=_params(("parallel",)),
        name="ffn",
    )(x1, mod3, npre, npost, wg, wu, wd)


def _layer(x2, mod3, pos2, batch, seq, lambda_init, norm_pre_mix, norm_post_mix, norm_pre_ffn, norm_post_ffn,
           w_in, conv_w, conv_b, dt_bias, a_log, d_skip, ssd_norm, w_o_ssd,
           lambda_q1, lambda_k1, lambda_q2, lambda_k2, subln, w_o_attn, w_out, w_gate, w_up, w_down):
    row = lambda v: v.reshape(1, -1).astype(F32)
    w_z, w_xbc, w_dt, w_q, w_k, w_v, w_g = jnp.split(w_in, IN_SPLITS, axis=1)
    w_wide = jnp.concatenate([w_xbc, w_v, w_z, w_g, w_k, w_q], axis=1).astype(BF16)
    w_dt = jnp.pad(w_dt, ((0, 0), (0, DT_W - SSD_HEADS))).astype(BF16)
    inv_freq = 1.0 / (ROPE_THETA ** (jnp.arange(0, DIFF_HEAD_DIM, 2, dtype=F32) / DIFF_HEAD_DIM))
    invf = jnp.tile(inv_freq, LANES // inv_freq.shape[0]).reshape(1, LANES)
    half_lane = jnp.arange(LANES) % DIFF_HEAD_DIM < DIFF_HEAD_DIM // 2
    sgn = jnp.where(half_lane, -1.0, 1.0).astype(F32).reshape(1, LANES)

    proj, dt_raw = _in_proj(x2, mod3, row(norm_pre_mix), pos2, invf, sgn, w_wide, w_dt,
                            conv_w.T.astype(F32), row(conv_b), seq)

    pad_h = lambda v: jnp.pad(v.astype(F32), (0, DT_W - SSD_HEADS)).reshape(1, DT_W)
    expand = (jnp.arange(DT_W)[:, None] == (jnp.arange(D_INNER)[None, :] // SSD_HEAD_DIM)).astype(BF16)
    tril = (jnp.arange(SSD_CHUNK)[None, :] <= jnp.arange(SSD_CHUNK)[:, None]).astype(BF16)
    y = _ssd(proj, dt_raw, pad_h(dt_bias), pad_h(a_log), row(jnp.repeat(d_skip, SSD_HEAD_DIM)), row(ssd_norm),
             expand, tril, batch, seq)

    o = _attn(proj, row(lambda_q1), row(lambda_k1), row(lambda_q2), row(lambda_k2), row(subln),
              batch, seq, lambda_init)

    x1 = _merge(y, o, proj, x2, mod3, w_o_ssd.astype(BF16), w_o_attn.astype(BF16), w_out.astype(BF16),
                row(norm_post_mix), seq)
    return _ffn(x1, mod3, row(norm_pre_ffn), row(norm_post_ffn), w_gate.astype(BF16), w_up.astype(BF16),
                w_down.astype(BF16), seq)


def kernel(x, c, positions, w_ada, b_ada, norm_pre_mix, norm_post_mix, norm_pre_ffn, norm_post_ffn, w_in, conv_w, conv_b, dt_bias, a_log, d_skip, ssd_norm, w_o_ssd, lambda_q1, lambda_k1, lambda_q2, lambda_k2, subln, w_o_attn, w_out, w_gate, w_up, w_down):
    batch, seq, _ = x.shape
    depth = w_in.shape[0]
    x2 = x.reshape(batch * seq, D_MODEL)
    pos2 = positions.reshape(batch * seq, 1)
    c_pad = jnp.pad(c, ((0, SUBLANES - batch), (0, 0)))
    for layer in range(depth):
        lambda_init = 0.8 - 0.6 * math.exp(-0.3 * layer)
        mod3 = _ada(c_pad, w_ada[layer], b_ada[layer].reshape(1, -1))[:batch].reshape(batch, 1, 6 * D_MODEL)
        x2 = _layer(x2, mod3, pos2, batch, seq, lambda_init, norm_pre_mix[layer], norm_post_mix[layer],
                    norm_pre_ffn[layer], norm_post_ffn[layer], w_in[layer], conv_w[layer], conv_b[layer],
                    dt_bias[layer], a_log[layer], d_skip[layer], ssd_norm[layer], w_o_ssd[layer],
                    lambda_q1[layer], lambda_k1[layer], lambda_q2[layer], lambda_k2[layer], subln[layer],
                    w_o_attn[layer], w_out[layer], w_gate[layer], w_up[layer], w_down[layer])
    return x2.reshape(batch, seq, D_MODEL)
```

```python
import functools
import math

import jax
import jax.numpy as jnp
import numpy as np
from jax import lax
from jax.experimental import pallas as pl
from jax.experimental.pallas import tpu as pltpu

F32 = jnp.float32
BF16 = jnp.bfloat16

D_MODEL = 1024
D_INNER = 2048
SSD_HEAD_DIM = 64
SSD_HEADS = 32
SSD_GROUPS = 4
SSD_STATE = 128
SSD_CONV = 4
SSD_CHUNK = 256
CONV_DIM = D_INNER + 2 * SSD_GROUPS * SSD_STATE
GROUP_CH = D_INNER // SSD_GROUPS
DIFF_HEADS = 8
DIFF_HEAD_DIM = 64
DIFF_QK = 1024
DIFF_V = 1024
ROPE_THETA = 10000.0
D_FF = 2816
EPS = 1e-6
IN_SIZES = (D_INNER, CONV_DIM, SSD_HEADS, DIFF_QK, DIFF_QK, DIFF_V, 2 * D_MODEL)
IN_SPLITS = tuple(int(v) for v in np.cumsum(IN_SIZES)[:-1])

LANES = 128
SUBLANES = 8
HEAD_W = 2 * DIFF_HEAD_DIM

XBC_OFF, V_OFF, Z_OFF, GATE_OFF, K_OFF, Q_OFF = 0, 3072, 4096, 6144, 8192, 9216
PROJ_W = 10240
PROJ_TN = 1024
DT_W = LANES

VMEM_LIMIT = 56 * 1024 * 1024

LOG2E = math.log2(math.e)
Q_SCALE = DIFF_HEAD_DIM ** -0.5 * LOG2E
NEG_BIG = -1e30


def _sigmoid(v):
    return 1.0 / (1.0 + jnp.exp(-v))


def _silu(v):
    return v * _sigmoid(v)


def _rms(v, w):
    return v * lax.rsqrt(jnp.mean(v * v, axis=-1, keepdims=True) + EPS) * w


def _dot(a, b):
    return jnp.dot(a, b, preferred_element_type=F32)


def _params(sem):
    return pltpu.CompilerParams(dimension_semantics=sem, vmem_limit_bytes=VMEM_LIMIT)


def _const_spec(shape):
    nd = len(shape)
    return pl.BlockSpec(shape, lambda *_: (0,) * nd, pipeline_mode=pl.Buffered(1))


def _ada_kernel(c_ref, w_ref, b_ref, o_ref):
    cond = _silu(c_ref[...])
    o_ref[...] = _dot(cond.astype(BF16), w_ref[...].astype(BF16)) + b_ref[...]


def _ada(c_pad, w_ada, b_ada):
    n = w_ada.shape[1]
    tn = 1024
    return pl.pallas_call(
        _ada_kernel,
        out_shape=jax.ShapeDtypeStruct((c_pad.shape[0], n), F32),
        grid=(n // tn,),
        in_specs=[pl.BlockSpec(c_pad.shape, lambda j: (0, 0)),
                  pl.BlockSpec((D_MODEL, tn), lambda j: (0, j)),
                  pl.BlockSpec((1, tn), lambda j: (0, j))],
        out_specs=pl.BlockSpec((c_pad.shape[0], tn), lambda j: (0, j)),
        compiler_params=_params(("arbitrary",)),
        name="ada",
    )(c_pad, w_ada, b_ada)


COL_KINDS = ("xbc", "xbc", "xbc", "v", "z", "z", "gate", "gate", "k", "q")
N_COL = len(COL_KINDS)
TRIG_COL = 4
EPI_RB = 128
TRIG_RB = 64
MM_ROWS, MM_COLS = 256, 256
HALO = SUBLANES


def _sigmoid_t(v):
    return 0.5 + 0.5 * jnp.tanh(0.5 * v)


def _silu_t(v):
    h = 0.5 * v
    return h + h * jnp.tanh(h)


def _inproj_kernel(x_ref, mod_ref, nw_ref, pos_ref, invf_ref, sgn_ref, w_ref, wdt_ref, convw_ref, convb_ref,
                   out_ref, dt_ref, h_ref, park0_ref, park1_ref, cos_ref, sin_ref, halo_ref,
                   *, tiles_per_seq, n_row_tiles):
    i, j = pl.program_id(0), pl.program_id(1)
    tm = x_ref.shape[0]
    parks = (park0_ref, park1_ref)

    @pl.when((j == 0) & (i < n_row_tiles))
    def _():
        mod = mod_ref[0]
        h = _rms(x_ref[...], nw_ref[...]) * (1.0 + mod[:, D_MODEL:2 * D_MODEL]) + mod[:, 0:D_MODEL]
        hb = h.astype(BF16)
        h_ref[...] = hb
        dt_ref[...] = _dot(hb, wdt_ref[...])

    def matmul_pieces(col, row0):
        park = parks[col % 2]
        rows = pl.ds(row0, MM_ROWS)

        def piece(n):
            def emit():
                park[rows, n * MM_COLS:(n + 1) * MM_COLS] = _dot(
                    h_ref[rows, :], w_ref[:, col * PROJ_TN + n * MM_COLS:col * PROJ_TN + (n + 1) * MM_COLS])
            return emit

        return [piece(n) for n in range(PROJ_TN // MM_COLS)]

    def conv_piece(col, park, lo, slab):
        def emit():
            sl = slice(slab * LANES, (slab + 1) * LANES)
            cols = slice(col * PROJ_TN + slab * LANES, col * PROJ_TN + (slab + 1) * LANES)
            w = convw_ref[:, cols]
            b = convb_ref[:, cols]
            lead = jnp.where(lo == 0, jnp.where(first_tile, 0.0, halo_ref[:, cols]),
                             park[pl.ds(pl.multiple_of(jnp.maximum(lo - HALO, 0), HALO), HALO), sl])
            ext = jnp.concatenate([lead, park[pl.ds(lo, EPI_RB), sl]], axis=0)
            conv = b + w[SSD_CONV - 1:SSD_CONV, :] * ext
            for s in range(1, SSD_CONV):
                conv = conv + w[SSD_CONV - 1 - s:SSD_CONV - s, :] * pltpu.roll(ext, s, 0)
            out_ref[pl.ds(lo, EPI_RB), sl] = _silu_t(conv[HALO:, :]).astype(BF16)
        return emit

    def rope_piece(park, scale, lo, hd):
        def emit():
            lane = lax.broadcasted_iota(jnp.int32, (EPI_RB, HEAD_W), 1)
            first_half = (lane % DIFF_HEAD_DIM) < (DIFF_HEAD_DIM // 2)
            rows = pl.ds(lo, EPI_RB)
            sl = slice(hd * HEAD_W, (hd + 1) * HEAD_W)
            xh = park[rows, sl]
            partner = jnp.where(first_half, pltpu.roll(xh, HEAD_W - DIFF_HEAD_DIM // 2, 1),
                                pltpu.roll(xh, DIFF_HEAD_DIM // 2, 1))
            out_ref[rows, sl] = ((xh * cos_ref[rows, :] + partner * sin_ref[rows, :]) * scale).astype(BF16)
        return emit

    def map_piece(park, fn, lo, slab):
        def emit():
            rows = pl.ds(lo, EPI_RB)
            sl = slice(slab * LANES, (slab + 1) * LANES)
            out_ref[rows, sl] = fn(park[rows, sl]).astype(BF16)
        return emit

    def trig_piece(lo):
        def emit():
            quarter, run = tm // 4, DIFF_HEAD_DIM // 2
            lane = lax.broadcasted_iota(jnp.int32, (TRIG_RB, LANES), 1)
            pos = [pos_ref[pl.ds(g * quarter + lo, TRIG_RB), :].astype(F32) for g in range(4)]
            compact = jnp.where(lane < run, pos[0], jnp.where(lane < 2 * run, pos[1],
                                                              jnp.where(lane < 3 * run, pos[2], pos[3])))
            ang = compact * invf_ref[...]
            for table, ref in ((jnp.cos(ang), cos_ref), (jnp.sin(ang), sin_ref)):
                for g in range(4):
                    base = table if g == 0 else pltpu.roll(table, LANES - g * run, 1)
                    t = jnp.where(lane < run, base, pltpu.roll(base, run, 1))
                    t = jnp.where(lane < 2 * run, t, pltpu.roll(t, 2 * run, 1))
                    ref[pl.ds(g * quarter + lo, TRIG_RB), :] = t * sgn_ref[...] if ref is sin_ref else t
        return emit

    def epilogue_pieces(col, row0):
        kind, park = COL_KINDS[col], parks[col % 2]
        grid = [(pl.multiple_of(row0 + blk * EPI_RB, EPI_RB), slab)
                for blk in range(MM_ROWS // EPI_RB) for slab in range(PROJ_TN // LANES)]
        if kind == "xbc":
            return [conv_piece(col, park, lo, slab) for lo, slab in grid]
        if kind in ("k", "q"):
            return [rope_piece(park, Q_SCALE if kind == "q" else 1.0, lo, hd) for lo, hd in grid]
        fn = {"v": lambda r: r, "z": _silu_t, "gate": _sigmoid_t}[kind]
        return [map_piece(park, fn, lo, slab) for lo, slab in grid]

    def fused(mm_col, ep_col, trig=False):
        n_bands = tm // MM_ROWS

        def band(m, carry):
            row0 = pl.multiple_of(m * MM_ROWS, MM_ROWS)
            mm = matmul_pieces(mm_col, row0) if mm_col is not None else []
            ep = epilogue_pieces(ep_col, row0) if ep_col is not None else []
            if trig:
                ep = ep + [trig_piece(pl.multiple_of(m * TRIG_RB, TRIG_RB))]
            n = max(len(mm), 1)
            for idx in range(n):
                for emit in mm[idx:idx + 1] + ep[idx * len(ep) // n:(idx + 1) * len(ep) // n]:
                    emit()
            return carry

        lax.fori_loop(0, n_bands, band, 0)
        if ep_col is not None and COL_KINDS[ep_col] == "xbc":
            cols = slice(ep_col * PROJ_TN, (ep_col + 1) * PROJ_TN)
            halo_ref[:, cols] = parks[ep_col % 2][tm - HALO:tm, :]

    first_tile = i % tiles_per_seq == 0

    @pl.when((i == 0) & (j == 0))
    def _():
        halo_ref[...] = jnp.zeros(halo_ref.shape, F32)
        fused(0, None)

    @pl.when((j == 0) & (i > 0) & (i < n_row_tiles))
    def _():
        fused(0, N_COL - 1)

    @pl.when((j == 0) & (i == n_row_tiles))
    def _():
        fused(None, N_COL - 1)

    for col in range(1, N_COL):
        @pl.when((j == col) & (i < n_row_tiles))
        def _(col=col):
            fused(col, col - 1, trig=col == TRIG_COL)


def _in_proj(x2, mod3, nw, pos2, invf, sgn, w_wide, w_dt, convw_t, convb, seq):
    t = x2.shape[0]
    tm, tn = 1024, PROJ_TN
    n_rows = t // tm
    per_b = seq // tm
    row = lambda i, j: jnp.minimum(i, n_rows - 1)
    lagged = lambda i, j: (jnp.clip(jnp.where(j == 0, i - 1, i), 0, n_rows - 1),
                           jnp.where((j == 0) | (i == n_rows), jnp.where(i == 0, 0, N_COL - 1), j - 1))
    return pl.pallas_call(
        functools.partial(_inproj_kernel, tiles_per_seq=per_b, n_row_tiles=n_rows),
        out_shape=(jax.ShapeDtypeStruct((t, PROJ_W), BF16), jax.ShapeDtypeStruct((t, DT_W), F32)),
        grid=(n_rows + 1, N_COL),
        in_specs=[pl.BlockSpec((tm, D_MODEL), lambda i, j: (row(i, j), 0)),
                  pl.BlockSpec((1, 1, 6 * D_MODEL), lambda i, j: (row(i, j) // per_b, 0, 0)),
                  _const_spec((1, D_MODEL)),
                  pl.BlockSpec((tm, 1), lambda i, j: (row(i, j), 0)),
                  _const_spec((1, LANES)), _const_spec((1, LANES)),
                  _const_spec((D_MODEL, PROJ_W)),
                  _const_spec((D_MODEL, DT_W)),
                  _const_spec((SSD_CONV, CONV_DIM)),
                  _const_spec((1, CONV_DIM))],
        out_specs=(pl.BlockSpec((tm, tn), lagged),
                   pl.BlockSpec((tm, DT_W), lambda i, j: (row(i, j), 0))),
        scratch_shapes=[pltpu.VMEM((tm, D_MODEL), BF16),
                        pltpu.VMEM((tm, tn), F32),
                        pltpu.VMEM((tm, tn), F32),
                        pltpu.VMEM((tm, LANES), F32),
                        pltpu.VMEM((tm, LANES), F32),
                        pltpu.VMEM((HALO, CONV_DIM), F32)],
        compiler_params=_params(("arbitrary", "arbitrary")),
        name="in_proj",
    )(x2, mod3, nw, pos2, invf, sgn, w_wide, w_dt, convw_t, convb)


def _split2(v):
    hi = v.astype(BF16)
    lo = (v - hi.astype(F32)).astype(BF16)
    return hi, lo


def _split3(v):
    hi = v.astype(BF16)
    r = v - hi.astype(F32)
    mid = r.astype(BF16)
    lo = (r - mid.astype(F32)).astype(BF16)
    return hi, mid, lo


def _ssd_kernel(act_ref, zs_ref, dt_ref, dtb_ref, alog_ref, dskip_ref, normw_ref, expand_ref, tril_ref,
                y_ref, yacc_ref, state_ref):
    L = SSD_CHUNK
    H = L // 2

    @pl.when(pl.program_id(1) == 0)
    def _():
        state_ref[...] = jnp.zeros(state_ref.shape, F32)

    dt = dt_ref[...] + dtb_ref[...]
    dt = jnp.maximum(dt, 0.0) + jnp.log1p(jnp.exp(-jnp.abs(dt)))
    a = dt * (-LOG2E * jnp.exp(alog_ref[...]))
    tril = tril_ref[...]
    cs = sum(_dot(tril, p) for p in _split3(a))
    key_t = cs.T - jnp.log2(dt.T)
    cs_last = cs[L - 1:L, :]
    ecs = jnp.exp2(cs).astype(BF16)
    dtw = (dt * jnp.exp2(cs_last - cs)).astype(BF16)
    cd16 = jnp.broadcast_to(jnp.exp2(cs_last), (2 * SUBLANES, DT_W))
    cd_e = sum(_dot(p, expand_ref[...]) for p in _split3(cd16))[0:1, :]

    tri = lax.broadcasted_iota(jnp.int32, (H, H), 1) <= lax.broadcasted_iota(jnp.int32, (H, H), 0)
    first_head = lax.broadcasted_iota(jnp.int32, (L, LANES), 1) < SSD_HEAD_DIM

    for g in range(SSD_GROUPS):
        gs = slice(g * GROUP_CH, (g + 1) * GROUP_CH)
        b_g = act_ref[:, D_INNER + g * SSD_STATE:D_INNER + (g + 1) * SSD_STATE]
        c_g = act_ref[:, D_INNER + (SSD_GROUPS + g) * SSD_STATE:D_INNER + (SSD_GROUPS + g + 1) * SSD_STATE]
        b_gt = b_g.astype(F32).T.astype(BF16)
        sc_top = _dot(c_g[0:H, :], b_gt[:, 0:H])
        sc_bot = _dot(c_g[H:L, :], b_gt)
        st = state_ref[g]
        y_off = _dot(c_g, st.astype(BF16)) * _dot(ecs, expand_ref[:, gs])
        xw = (act_ref[:, gs].astype(F32) * _dot(dtw, expand_ref[:, gs])).astype(BF16)
        state_ref[g] = st * cd_e[:, gs] + _dot(b_gt, xw)
        for pair in range(GROUP_CH // LANES):
            h0 = g * (SSD_HEADS // SSD_GROUPS) + 2 * pair
            ps = slice(g * GROUP_CH + pair * LANES, g * GROUP_CH + (pair + 1) * LANES)
            xp = act_ref[:, ps]
            zero = jnp.zeros_like(xp)
            y_top = y_bot = None
            for sub, rhs in ((0, jnp.where(first_head, xp, zero)), (1, jnp.where(first_head, zero, xp))):
                h = h0 + sub
                col = cs[:, h:h + 1]
                rowv = key_t[h:h + 1, :]
                d_tl = jnp.exp2(jnp.where(tri, col[0:H] - rowv[:, 0:H], NEG_BIG))
                d_bl = jnp.exp2(col[H:L] - rowv[:, 0:H])
                d_br = jnp.exp2(jnp.where(tri, col[H:L] - rowv[:, H:L], NEG_BIG))
                m_top = (sc_top * d_tl).astype(BF16)
                m_bot = jnp.concatenate([(sc_bot[:, 0:H] * d_bl).astype(BF16),
                                         (sc_bot[:, H:L] * d_br).astype(BF16)], axis=1)
                p_top = _dot(m_top, rhs[0:H, :])
                p_bot = _dot(m_bot, rhs)
                y_top = p_top if y_top is None else y_top + p_top
                y_bot = p_bot if y_bot is None else y_bot + p_bot
            y_pair = jnp.concatenate([y_top, y_bot], axis=0)
            yacc_ref[:, ps] = (y_pair + y_off[:, pair * LANES:(pair + 1) * LANES]
                               + xp.astype(F32) * dskip_ref[:, ps])

    for g in range(SSD_GROUPS):
        gs = slice(g * GROUP_CH, (g + 1) * GROUP_CH)
        yg = yacc_ref[:, gs] * zs_ref[:, gs].astype(F32)
        y_ref[:, gs] = _rms(yg, normw_ref[:, gs]).astype(y_ref.dtype)


def _ssd(proj, dt_raw, dtb, alog, dskip_e, normw, expand, tril, batch, seq):
    t = proj.shape[0]
    L = SSD_CHUNK
    nc = seq // L
    row = lambda b, c: b * nc + c
    return pl.pallas_call(
        _ssd_kernel,
        out_shape=jax.ShapeDtypeStruct((t, D_INNER), BF16),
        grid=(batch, nc),
        in_specs=[pl.BlockSpec((L, CONV_DIM), lambda b, c: (row(b, c), XBC_OFF // CONV_DIM)),
                  pl.BlockSpec((L, D_INNER), lambda b, c: (row(b, c), Z_OFF // D_INNER)),
                  pl.BlockSpec((L, DT_W), lambda b, c: (row(b, c), 0)),
                  _const_spec((1, DT_W)), _const_spec((1, DT_W)),
                  _const_spec((1, D_INNER)), _const_spec((1, D_INNER)),
                  _const_spec((DT_W, D_INNER)), _const_spec((L, L))],
        out_specs=pl.BlockSpec((L, D_INNER), lambda b, c: (row(b, c), 0)),
        scratch_shapes=[pltpu.VMEM((L, D_INNER), F32),
                        pltpu.VMEM((SSD_GROUPS, SSD_STATE, GROUP_CH), F32)],
        compiler_params=_params(("parallel", "arbitrary")),
        name="ssd",
    )(proj, proj, dt_raw, dtb, alog, dskip_e, normw, expand, tril)


def _attn_kernel(q_ref, k_ref, v_ref, lq1_ref, lk1_ref, lq2_ref, lk2_ref, subln_ref, o_ref,
                 q2_ref, vaug_ref, s0_ref, s1_ref, s2_ref, m_ref, acc_ref, *, tq, tk, rb, lambda_init):
    assert tq == tk and tq % rb == 0
    seq = q_ref.shape[0]
    nq = seq // tq

    lane = lax.broadcasted_iota(jnp.int32, (tq, HEAD_W), 1)
    comp0 = lane < DIFF_HEAD_DIM
    for t in range(nq):
        q = q_ref[t * tq:(t + 1) * tq, :]
        zero = jnp.zeros_like(q)
        q2_ref[t, 0:tq, :] = jnp.where(comp0, q, zero)
        q2_ref[t, tq:2 * tq, :] = jnp.where(comp0, zero, q)
    vaug_ref[:, 0:HEAD_W] = v_ref[...]
    vaug_ref[:, HEAD_W:2 * HEAD_W] = jnp.ones((seq, HEAD_W), BF16)

    lam = (jnp.exp(jnp.sum(lq1_ref[...] * lk1_ref[...], axis=-1, keepdims=True))
           - jnp.exp(jnp.sum(lq2_ref[...] * lk2_ref[...], axis=-1, keepdims=True)) + lambda_init)

    def keys_needed(diag, r):
        return (r * rb) % tq + rb if diag else tk

    def scores_block(qi, j, s_ref, diag, r):
        rows = slice(r * rb, (r + 1) * rb)
        nk = keys_needed(diag, r)
        start = pl.multiple_of(j * tk, tk)
        s_ref[rows, 0:nk] = lax.dot_general(q2_ref[qi, rows, :], k_ref[pl.ds(start, nk), :],
                                            (((1,), (1,)), ((), ())), preferred_element_type=F32)

    def softmax_block(j, s_ref, diag, r):
        rows = slice(r * rb, (r + 1) * rb)
        nk = keys_needed(diag, r)
        start = pl.multiple_of(j * tk, tk)
        s = s_ref[rows, 0:nk]
        if diag:
            qpos = (r * rb) % tq + lax.broadcasted_iota(jnp.int32, (rb, nk), 0)
            s = jnp.where(lax.broadcasted_iota(jnp.int32, (rb, nk), 1) <= qpos, s, NEG_BIG)
        m_prev = m_ref[rows, :]
        m_next = jnp.maximum(m_prev, jnp.max(s, axis=1, keepdims=True))
        alpha = jnp.exp2(m_prev - m_next)
        p = jnp.exp2(s - jnp.concatenate([m_next] * (nk // LANES), axis=1)).astype(BF16)
        acc_ref[rows, :] = (jnp.concatenate([alpha, alpha], axis=1) * acc_ref[rows, :]
                            + _dot(p, vaug_ref[pl.ds(start, nk), :]))
        m_ref[rows, :] = m_next

    def scores(qi, j, s_ref, diag):
        for r in range(2 * tq // rb):
            scores_block(qi, j, s_ref, diag, r)

    def scores_softmax(nxt, cur):
        for r in range(2 * tq // rb):
            scores_block(*nxt, r)
            softmax_block(*cur, r)

    def finish(qi):
        o = (acc_ref[0:tq, 0:HEAD_W] / acc_ref[0:tq, HEAD_W:2 * HEAD_W]
             - lam * (acc_ref[tq:2 * tq, 0:HEAD_W] / acc_ref[tq:2 * tq, HEAD_W:2 * HEAD_W]))
        start = pl.multiple_of(qi * tq, tq)
        o_ref[pl.ds(start, tq), :] = (_rms(o, subln_ref[...]) * (1.0 - lambda_init)).astype(o_ref.dtype)

    scores(0, 0, s2_ref, True)

    def q_tile(qi, carry):
        m_ref[...] = jnp.full(m_ref.shape, NEG_BIG, F32)
        acc_ref[...] = jnp.zeros(acc_ref.shape, F32)

        def pair(i, c):
            scores_softmax((qi, 2 * i + 1, s1_ref, False), (2 * i, s0_ref, False))
            scores_softmax((qi, 2 * i + 2, s0_ref, False), (2 * i + 1, s1_ref, False))
            return c

        n_pairs = jnp.maximum(qi - 1, 0) // 2
        lax.fori_loop(0, n_pairs, pair, 0)

        @pl.when(qi % 2 == 1)
        def _():
            scores_softmax((qi, qi, s2_ref, True), (qi - 1, s0_ref, False))

        @pl.when((qi % 2 == 0) & (qi > 0))
        def _():
            scores_softmax((qi, qi - 1, s1_ref, False), (qi - 2, s0_ref, False))
            scores_softmax((qi, qi, s2_ref, True), (qi - 1, s1_ref, False))

        scores_softmax((jnp.minimum(qi + 1, nq - 1), 0, s0_ref, False), (qi, s2_ref, True))
        finish(qi)
        return carry

    lax.fori_loop(0, nq, q_tile, 0)


def _attn(proj, lq1, lk1, lq2, lk2, subln, batch, seq, lambda_init):
    t = proj.shape[0]
    tq, tk, rb = 512, 512, 256
    nq = seq // tq
    lam_spec = _const_spec((1, DIFF_HEAD_DIM))
    head_cols = lambda off: (lambda b, h: (b, off // HEAD_W + h))
    return pl.pallas_call(
        functools.partial(_attn_kernel, tq=tq, tk=tk, rb=rb, lambda_init=lambda_init),
        out_shape=jax.ShapeDtypeStruct((t, DIFF_V), BF16),
        grid=(batch, DIFF_HEADS),
        in_specs=[pl.BlockSpec((seq, HEAD_W), head_cols(Q_OFF)),
                  pl.BlockSpec((seq, HEAD_W), head_cols(K_OFF)),
                  pl.BlockSpec((seq, HEAD_W), head_cols(V_OFF)),
                  lam_spec, lam_spec, lam_spec, lam_spec, _const_spec((1, HEAD_W))],
        out_specs=pl.BlockSpec((seq, HEAD_W), lambda b, h: (b, h)),
        scratch_shapes=[pltpu.VMEM((nq, 2 * tq, HEAD_W), BF16),
                        pltpu.VMEM((seq, 2 * HEAD_W), BF16),
                        pltpu.VMEM((2 * tq, tk), F32),
                        pltpu.VMEM((2 * tq, tk), F32),
                        pltpu.VMEM((2 * tq, tk), F32),
                        pltpu.VMEM((2 * tq, LANES), F32),
                        pltpu.VMEM((2 * tq, 2 * HEAD_W), F32)],
        compiler_params=_params(("parallel", "parallel")),
        name="diff_attn",
    )(proj, proj, proj, lq1, lk1, lq2, lk2, subln)


def _merge_kernel(y_ref, o_ref, gate_ref, x_ref, mod_ref, wos_ref, woa_ref, wout_ref, nw_ref, out_ref):
    y_ssd = _dot(y_ref[...], wos_ref[...])
    y_attn = _dot(o_ref[...], woa_ref[...])
    g_ssd = gate_ref[:, 0:D_MODEL].astype(F32)
    g_attn = gate_ref[:, D_MODEL:2 * D_MODEL].astype(F32)
    merged = (g_ssd * y_ssd + g_attn * y_attn).astype(BF16)
    mix = _dot(merged, wout_ref[...])
    g1 = mod_ref[0][:, 2 * D_MODEL:3 * D_MODEL]
    out_ref[...] = x_ref[...] + g1 * _rms(mix, nw_ref[...])


def _merge(y, o, proj, x2, mod3, wos, woa, wout, nw, seq):
    t = x2.shape[0]
    tm = 512
    per_b = seq // tm
    return pl.pallas_call(
        _merge_kernel,
        out_shape=jax.ShapeDtypeStruct((t, D_MODEL), F32),
        grid=(t // tm,),
        in_specs=[pl.BlockSpec((tm, D_INNER), lambda i: (i, 0)),
                  pl.BlockSpec((tm, DIFF_V), lambda i: (i, 0)),
                  pl.BlockSpec((tm, 2 * D_MODEL), lambda i: (i, GATE_OFF // (2 * D_MODEL))),
                  pl.BlockSpec((tm, D_MODEL), lambda i: (i, 0)),
                  pl.BlockSpec((1, 1, 6 * D_MODEL), lambda i: (i // per_b, 0, 0)),
                  _const_spec((D_INNER, D_MODEL)), _const_spec((DIFF_V, D_MODEL)),
                  _const_spec((D_MODEL, D_MODEL)), _const_spec((1, D_MODEL))],
        out_specs=pl.BlockSpec((tm, D_MODEL), lambda i: (i, 0)),
        compiler_params=_params(("parallel",)),
        name="merge",
    )(y, o, proj, x2, mod3, wos, woa, wout, nw)


def _ffn_kernel(x_ref, mod_ref, npre_ref, npost_ref, wg_ref, wu_ref, wd_ref, out_ref):
    mod = mod_ref[0]
    x = x_ref[...]
    h = (_rms(x, npre_ref[...]) * (1.0 + mod[:, 4 * D_MODEL:5 * D_MODEL]) + mod[:, 3 * D_MODEL:4 * D_MODEL]).astype(BF16)
    act = (_silu(_dot(h, wg_ref[...])) * _dot(h, wu_ref[...])).astype(BF16)
    f = _dot(act, wd_ref[...])
    out_ref[...] = x + mod[:, 5 * D_MODEL:6 * D_MODEL] * _rms(f, npost_ref[...])


def _ffn(x1, mod3, npre, npost, wg, wu, wd, seq):
    t = x1.shape[0]
    tm = 512
    per_b = seq // tm
    return pl.pallas_call(
        _ffn_kernel,
        out_shape=jax.ShapeDtypeStruct((t, D_MODEL), F32),
        grid=(t // tm,),
        in_specs=[pl.BlockSpec((tm, D_MODEL), lambda i: (i, 0)),
                  pl.BlockSpec((1, 1, 6 * D_MODEL), lambda i: (i // per_b, 0, 0)),
                  _const_spec((1, D_MODEL)), _const_spec((1, D_MODEL)),
                  _const_spec((D_MODEL, D_FF)), _const_spec((D_MODEL, D_FF)), _const_spec((D_FF, D_MODEL))],
        out_specs=pl.BlockSpec((tm, D_MODEL), lambda i: (i, 0)),
        compiler_params=_params(("parallel",)),
        name="ffn",
    )(x1, mod3, npre, npost, wg, wu, wd)


def _layer(x2, mod3, pos2, batch, seq, lambda_init, norm_pre_mix, norm_post_mix, norm_pre_ffn, norm_post_ffn,
           w_in, conv_w, conv_b, dt_bias, a_log, d_skip, ssd_norm, w_o_ssd,
           lambda_q1, lambda_k1, lambda_q2, lambda_k2, subln, w_o_attn, w_out, w_gate, w_up, w_down):
    row = lambda v: v.reshape(1, -1).astype(F32)
    w_z, w_xbc, w_dt, w_q, w_k, w_v, w_g = jnp.split(w_in, IN_SPLITS, axis=1)
    w_wide = jnp.concatenate([w_xbc, w_v, w_z, w_g, w_k, w_q], axis=1).astype(BF16)
    w_dt = jnp.pad(w_dt, ((0, 0), (0, DT_W - SSD_HEADS))).astype(BF16)
    inv_freq = 1.0 / (ROPE_THETA ** (jnp.arange(0, DIFF_HEAD_DIM, 2, dtype=F32) / DIFF_HEAD_DIM))
    invf = jnp.tile(inv_freq, LANES // inv_freq.shape[0]).reshape(1, LANES)
    half_lane = jnp.arange(LANES) % DIFF_HEAD_DIM < DIFF_HEAD_DIM // 2
    sgn = jnp.where(half_lane, -1.0, 1.0).astype(F32).reshape(1, LANES)

    proj, dt_raw = _in_proj(x2, mod3, row(norm_pre_mix), pos2, invf, sgn, w_wide, w_dt,
                            conv_w.T.astype(F32), row(conv_b), seq)

    pad_h = lambda v: jnp.pad(v.astype(F32), (0, DT_W - SSD_HEADS)).reshape(1, DT_W)
    expand = (jnp.arange(DT_W)[:, None] == (jnp.arange(D_INNER)[None, :] // SSD_HEAD_DIM)).astype(BF16)
    tril = (jnp.arange(SSD_CHUNK)[None, :] <= jnp.arange(SSD_CHUNK)[:, None]).astype(BF16)
    y = _ssd(proj, dt_raw, pad_h(dt_bias), pad_h(a_log), row(jnp.repeat(d_skip, SSD_HEAD_DIM)), row(ssd_norm),
             expand, tril, batch, seq)

    o = _attn(proj, row(lambda_q1), row(lambda_k1), row(lambda_q2), row(lambda_k2), row(subln),
              batch, seq, lambda_init)

    x1 = _merge(y, o, proj, x2, mod3, w_o_ssd.astype(BF16), w_o_attn.astype(BF16), w_out.astype(BF16),
                row(norm_post_mix), seq)
    return _ffn(x1, mod3, row(norm_pre_ffn), row(norm_post_ffn), w_gate.astype(BF16), w_up.astype(BF16),
                w_down.astype(BF16), seq)


def kernel(x, c, positions, w_ada, b_ada, norm_pre_mix, norm_post_mix, norm_pre_ffn, norm_post_ffn, w_in, conv_w, conv_b, dt_bias, a_log, d_skip, ssd_norm, w_o_ssd, lambda_q1, lambda_k1, lambda_q2, lambda_k2, subln, w_o_attn, w_out, w_gate, w_up, w_down):
    batch, seq, _ = x.shape
    depth = w_in.shape[0]
    x2 = x.reshape(batch * seq, D_MODEL)
    pos2 = positions.reshape(batch * seq, 1)
    c_pad = jnp.pad(c, ((0, SUBLANES - batch), (0, 0)))
    for layer in range(depth):
        lambda_init = 0.8 - 0.6 * math.exp(-0.3 * layer)
        mod3 = _ada(c_pad, w_ada[layer], b_ada[layer].reshape(1, -1))[:batch].reshape(batch, 1, 6 * D_MODEL)
        x2 = _layer(x2, mod3, pos2, batch, seq, lambda_init, norm_pre_mix[layer], norm_post_mix[layer],
                    norm_pre_ffn[layer], norm_post_ffn[layer], w_in[layer], conv_w[layer], conv_b[layer],
                    dt_bias[layer], a_log[layer], d_skip[layer], ssd_norm[layer], w_o_ssd[layer],
                    lambda_q1[layer], lambda_k1[layer], lambda_q2[layer], lambda_k2[layer], subln[layer],
                    w_o_attn[layer], w_out[layer], w_gate[layer], w_up[layer], w_down[layer])
    return x2.reshape(batch, seq, D_MODEL)
```

```python
import functools
import math

import jax
import jax.numpy as jnp
import numpy as np
from jax import lax
from jax.experimental import pallas as pl
from jax.experimental.pallas import tpu as pltpu

F32 = jnp.float32
BF16 = jnp.bfloat16

D_MODEL = 1024
D_INNER = 2048
SSD_HEAD_DIM = 64
SSD_HEADS = 32
SSD_GROUPS = 4
SSD_STATE = 128
SSD_CONV = 4
SSD_CHUNK = 256
CONV_DIM = D_INNER + 2 * SSD_GROUPS * SSD_STATE
GROUP_CH = D_INNER // SSD_GROUPS
DIFF_HEADS = 8
DIFF_HEAD_DIM = 64
DIFF_QK = 1024
DIFF_V = 1024
ROPE_THETA = 10000.0
D_FF = 2816
EPS = 1e-6
IN_SIZES = (D_INNER, CONV_DIM, SSD_HEADS, DIFF_QK, DIFF_QK, DIFF_V, 2 * D_MODEL)
IN_SPLITS = tuple(int(v) for v in np.cumsum(IN_SIZES)[:-1])

LANES = 128
SUBLANES = 8
HEAD_W = 2 * DIFF_HEAD_DIM

XBC_OFF, V_OFF, Z_OFF, GATE_OFF, K_OFF, Q_OFF = 0, 3072, 4096, 6144, 8192, 9216
PROJ_W = 10240
PROJ_TN = 1024
DT_W = LANES

VMEM_LIMIT = 56 * 1024 * 1024

LOG2E = math.log2(math.e)
Q_SCALE = DIFF_HEAD_DIM ** -0.5 * LOG2E
NEG_BIG = -1e30


def _sigmoid(v):
    return 1.0 / (1.0 + jnp.exp(-v))


def _silu(v):
    return v * _sigmoid(v)


def _rms(v, w):
    return v * lax.rsqrt(jnp.mean(v * v, axis=-1, keepdims=True) + EPS) * w


def _dot(a, b):
    return jnp.dot(a, b, preferred_element_type=F32)


def _params(sem):
    return pltpu.CompilerParams(dimension_semantics=sem, vmem_limit_bytes=VMEM_LIMIT)


def _const_spec(shape):
    nd = len(shape)
    return pl.BlockSpec(shape, lambda *_: (0,) * nd, pipeline_mode=pl.Buffered(1))


def _ada_kernel(c_ref, w_ref, b_ref, o_ref):
    cond = _silu(c_ref[...])
    o_ref[...] = _dot(cond.astype(BF16), w_ref[...].astype(BF16)) + b_ref[...]


def _ada(c_pad, w_ada, b_ada):
    n = w_ada.shape[1]
    tn = 1024
    return pl.pallas_call(
        _ada_kernel,
        out_shape=jax.ShapeDtypeStruct((c_pad.shape[0], n), F32),
        grid=(n // tn,),
        in_specs=[pl.BlockSpec(c_pad.shape, lambda j: (0, 0)),
                  pl.BlockSpec((D_MODEL, tn), lambda j: (0, j)),
                  pl.BlockSpec((1, tn), lambda j: (0, j))],
        out_specs=pl.BlockSpec((c_pad.shape[0], tn), lambda j: (0, j)),
        compiler_params=_params(("arbitrary",)),
        name="ada",
    )(c_pad, w_ada, b_ada)


COL_KINDS = ("xbc", "xbc", "xbc", "v", "z", "z", "gate", "gate", "k", "q")
N_COL = len(COL_KINDS)
INPROJ_TM = 1024
TRIG_COL = 4
EPI_RB = 128
TRIG_RB = 64
MM_ROWS, MM_COLS = 256, 256
HALO = SUBLANES


def _sigmoid_t(v):
    return 0.5 + 0.5 * jnp.tanh(0.5 * v)


def _silu_t(v):
    h = 0.5 * v
    return h + h * jnp.tanh(h)


def _inproj_kernel(x_ref, mod_ref, nw_ref, posc_ref, invf_ref, sgn_ref, w_ref, wdt_ref, convw_ref, convb_ref,
                   out_ref, dt_ref, h_ref, park0_ref, park1_ref, cos_ref, sin_ref, halo_ref,
                   *, tiles_per_seq, n_row_tiles):
    i, j = pl.program_id(0), pl.program_id(1)
    tm = x_ref.shape[0]
    parks = (park0_ref, park1_ref)

    @pl.when((j == 0) & (i < n_row_tiles))
    def _():
        mod = mod_ref[0]
        h = _rms(x_ref[...], nw_ref[...]) * (1.0 + mod[:, D_MODEL:2 * D_MODEL]) + mod[:, 0:D_MODEL]
        hb = h.astype(BF16)
        h_ref[...] = hb
        dt_ref[...] = _dot(hb, wdt_ref[...])

    def matmul_pieces(col):
        park = parks[col % 2]

        def piece(rows, sl):
            def emit():
                park[rows, sl] = _dot(h_ref[rows, :], w_ref[:, col * PROJ_TN + sl.start:col * PROJ_TN + sl.stop])
            return emit

        return [piece(slice(m * MM_ROWS, (m + 1) * MM_ROWS), slice(n * MM_COLS, (n + 1) * MM_COLS))
                for n in range(PROJ_TN // MM_COLS) for m in range(tm // MM_ROWS)]

    def conv_piece(col, park, blk, slab):
        def emit():
            lo = blk * EPI_RB
            sl = slice(slab * LANES, (slab + 1) * LANES)
            cols = slice(col * PROJ_TN + slab * LANES, col * PROJ_TN + (slab + 1) * LANES)
            w = convw_ref[:, cols]
            b = convb_ref[:, cols]
            lead = jnp.where(first_tile, 0.0, halo_ref[:, cols]) if blk == 0 else park[lo - HALO:lo, sl]
            ext = jnp.concatenate([lead, park[lo:lo + EPI_RB, sl]], axis=0)
            conv = b + w[SSD_CONV - 1:SSD_CONV, :] * ext
            for s in range(1, SSD_CONV):
                conv = conv + w[SSD_CONV - 1 - s:SSD_CONV - s, :] * pltpu.roll(ext, s, 0)
            out_ref[lo:lo + EPI_RB, sl] = _silu_t(conv[HALO:, :]).astype(BF16)
            if blk == tm // EPI_RB - 1:
                halo_ref[:, cols] = park[tm - HALO:tm, sl]
        return emit

    def rope_piece(park, scale, blk, hd):
        def emit():
            lane = lax.broadcasted_iota(jnp.int32, (EPI_RB, HEAD_W), 1)
            first_half = (lane % DIFF_HEAD_DIM) < (DIFF_HEAD_DIM // 2)
            rows = slice(blk * EPI_RB, (blk + 1) * EPI_RB)
            sl = slice(hd * HEAD_W, (hd + 1) * HEAD_W)
            xh = park[rows, sl]
            partner = jnp.where(first_half, pltpu.roll(xh, HEAD_W - DIFF_HEAD_DIM // 2, 1),
                                pltpu.roll(xh, DIFF_HEAD_DIM // 2, 1))
            out_ref[rows, sl] = ((xh * cos_ref[rows, :] + partner * sin_ref[rows, :]) * scale).astype(BF16)
        return emit

    def map_piece(park, fn, blk, slab):
        def emit():
            rows = slice(blk * EPI_RB, (blk + 1) * EPI_RB)
            sl = slice(slab * LANES, (slab + 1) * LANES)
            out_ref[rows, sl] = fn(park[rows, sl]).astype(BF16)
        return emit

    def trig_piece(blk):
        def emit():
            quarter, run = tm // 4, DIFF_HEAD_DIM // 2
            lane = lax.broadcasted_iota(jnp.int32, (TRIG_RB, LANES), 1)
            ang = posc_ref[blk * TRIG_RB:(blk + 1) * TRIG_RB, :] * invf_ref[...]
            for table, ref in ((jnp.cos(ang), cos_ref), (jnp.sin(ang), sin_ref)):
                for g in range(4):
                    base = table if g == 0 else pltpu.roll(table, LANES - g * run, 1)
                    t = jnp.where(lane < run, base, pltpu.roll(base, run, 1))
                    t = jnp.where(lane < 2 * run, t, pltpu.roll(t, 2 * run, 1))
                    rows = slice(g * quarter + blk * TRIG_RB, g * quarter + (blk + 1) * TRIG_RB)
                    ref[rows, :] = t * sgn_ref[...] if ref is sin_ref else t
        return emit

    def epilogue_pieces(col):
        kind, park = COL_KINDS[col], parks[col % 2]
        grid = [(blk, slab) for blk in range(tm // EPI_RB) for slab in range(PROJ_TN // LANES)]
        if kind == "xbc":
            return [conv_piece(col, park, blk, slab) for blk, slab in grid]
        if kind in ("k", "q"):
            return [rope_piece(park, Q_SCALE if kind == "q" else 1.0, blk, hd) for blk, hd in grid]
        fn = {"v": lambda r: r, "z": _silu_t, "gate": _sigmoid_t}[kind]
        return [map_piece(park, fn, blk, slab) for blk, slab in grid]

    def fused(mm_col, ep_col, trig=False):
        mm = matmul_pieces(mm_col) if mm_col is not None else []
        ep = epilogue_pieces(ep_col) if ep_col is not None else []
        if trig:
            ep = ep + [trig_piece(blk) for blk in range(tm // 4 // TRIG_RB)]
        n = max(len(mm), 1)
        for idx in range(n):
            for emit in mm[idx:idx + 1] + ep[idx * len(ep) // n:(idx + 1) * len(ep) // n]:
                emit()

    first_tile = i % tiles_per_seq == 0

    @pl.when((i == 0) & (j == 0))
    def _():
        halo_ref[...] = jnp.zeros(halo_ref.shape, F32)
        fused(0, None)

    @pl.when((j == 0) & (i > 0) & (i < n_row_tiles))
    def _():
        fused(0, N_COL - 1)

    @pl.when((j == 0) & (i == n_row_tiles))
    def _():
        fused(None, N_COL - 1)

    for col in range(1, N_COL):
        @pl.when((j == col) & (i < n_row_tiles))
        def _(col=col):
            fused(col, col - 1, trig=col == TRIG_COL)


def _in_proj(x2, mod3, nw, pos2, invf, sgn, w_wide, w_dt, convw_t, convb, seq):
    t = x2.shape[0]
    tm, tn = INPROJ_TM, PROJ_TN
    n_rows = t // tm
    per_b = seq // tm
    row = lambda i, j: jnp.minimum(i, n_rows - 1)
    lagged = lambda i, j: (jnp.clip(jnp.where(j == 0, i - 1, i), 0, n_rows - 1),
                           jnp.where((j == 0) | (i == n_rows), jnp.where(i == 0, 0, N_COL - 1), j - 1))
    return pl.pallas_call(
        functools.partial(_inproj_kernel, tiles_per_seq=per_b, n_row_tiles=n_rows),
        out_shape=(jax.ShapeDtypeStruct((t, PROJ_W), BF16), jax.ShapeDtypeStruct((t, DT_W), F32)),
        grid=(n_rows + 1, N_COL),
        in_specs=[pl.BlockSpec((tm, D_MODEL), lambda i, j: (row(i, j), 0)),
                  pl.BlockSpec((1, 1, 6 * D_MODEL), lambda i, j: (row(i, j) // per_b, 0, 0)),
                  _const_spec((1, D_MODEL)),
                  pl.BlockSpec((tm // 4, LANES), lambda i, j: (row(i, j), 0)),
                  _const_spec((1, LANES)), _const_spec((1, LANES)),
                  _const_spec((D_MODEL, PROJ_W)),
                  _const_spec((D_MODEL, DT_W)),
                  _const_spec((SSD_CONV, CONV_DIM)),
                  _const_spec((1, CONV_DIM))],
        out_specs=(pl.BlockSpec((tm, tn), lagged),
                   pl.BlockSpec((tm, DT_W), lambda i, j: (row(i, j), 0))),
        scratch_shapes=[pltpu.VMEM((tm, D_MODEL), BF16),
                        pltpu.VMEM((tm, tn), F32),
                        pltpu.VMEM((tm, tn), F32),
                        pltpu.VMEM((tm, LANES), F32),
                        pltpu.VMEM((tm, LANES), F32),
                        pltpu.VMEM((HALO, CONV_DIM), F32)],
        compiler_params=_params(("arbitrary", "arbitrary")),
        name="in_proj",
    )(x2, mod3, nw, pos2, invf, sgn, w_wide, w_dt, convw_t, convb)


def _split2(v):
    hi = v.astype(BF16)
    lo = (v - hi.astype(F32)).astype(BF16)
    return hi, lo


def _split3(v):
    hi = v.astype(BF16)
    r = v - hi.astype(F32)
    mid = r.astype(BF16)
    lo = (r - mid.astype(F32)).astype(BF16)
    return hi, mid, lo


def _ssd_kernel(act_ref, zs_ref, dt_ref, dtb_ref, alog_ref, dskip_ref, normw_ref, expand_ref, tril_ref,
                y_ref, yacc_ref, state_ref):
    L = SSD_CHUNK
    H = L // 2

    @pl.when(pl.program_id(1) == 0)
    def _():
        state_ref[...] = jnp.zeros(state_ref.shape, F32)

    dt = dt_ref[...] + dtb_ref[...]
    dt = jnp.maximum(dt, 0.0) + jnp.log1p(jnp.exp(-jnp.abs(dt)))
    a = dt * (-LOG2E * jnp.exp(alog_ref[...]))
    tril = tril_ref[...]
    cs = sum(_dot(tril, p) for p in _split3(a))
    key_t = cs.T - jnp.log2(dt.T)
    cs_last = cs[L - 1:L, :]
    ecs = jnp.exp2(cs).astype(BF16)
    dtw = (dt * jnp.exp2(cs_last - cs)).astype(BF16)
    cd16 = jnp.broadcast_to(jnp.exp2(cs_last), (2 * SUBLANES, DT_W))
    cd_e = sum(_dot(p, expand_ref[...]) for p in _split3(cd16))[0:1, :]

    tri = lax.broadcasted_iota(jnp.int32, (H, H), 1) <= lax.broadcasted_iota(jnp.int32, (H, H), 0)
    first_head = lax.broadcasted_iota(jnp.int32, (L, LANES), 1) < SSD_HEAD_DIM

    for g in range(SSD_GROUPS):
        gs = slice(g * GROUP_CH, (g + 1) * GROUP_CH)
        b_g = act_ref[:, D_INNER + g * SSD_STATE:D_INNER + (g + 1) * SSD_STATE]
        c_g = act_ref[:, D_INNER + (SSD_GROUPS + g) * SSD_STATE:D_INNER + (SSD_GROUPS + g + 1) * SSD_STATE]
        b_gt = b_g.astype(F32).T.astype(BF16)
        sc_top = _dot(c_g[0:H, :], b_gt[:, 0:H])
        sc_bot = _dot(c_g[H:L, :], b_gt)
        st = state_ref[g]
        y_off = _dot(c_g, st.astype(BF16)) * _dot(ecs, expand_ref[:, gs])
        xw = (act_ref[:, gs].astype(F32) * _dot(dtw, expand_ref[:, gs])).astype(BF16)
        state_ref[g] = st * cd_e[:, gs] + _dot(b_gt, xw)
        for pair in range(GROUP_CH // LANES):
            h0 = g * (SSD_HEADS // SSD_GROUPS) + 2 * pair
            ps = slice(g * GROUP_CH + pair * LANES, g * GROUP_CH + (pair + 1) * LANES)
            xp = act_ref[:, ps]
            zero = jnp.zeros_like(xp)
            y_top = y_bot = None
            for sub, rhs in ((0, jnp.where(first_head, xp, zero)), (1, jnp.where(first_head, zero, xp))):
                h = h0 + sub
                col = cs[:, h:h + 1]
                rowv = key_t[h:h + 1, :]
                d_tl = jnp.exp2(jnp.where(tri, col[0:H] - rowv[:, 0:H], NEG_BIG))
                d_bl = jnp.exp2(col[H:L] - rowv[:, 0:H])
                d_br = jnp.exp2(jnp.where(tri, col[H:L] - rowv[:, H:L], NEG_BIG))
                m_top = (sc_top * d_tl).astype(BF16)
                m_bot = jnp.concatenate([(sc_bot[:, 0:H] * d_bl).astype(BF16),
                                         (sc_bot[:, H:L] * d_br).astype(BF16)], axis=1)
                p_top = _dot(m_top, rhs[0:H, :])
                p_bot = _dot(m_bot, rhs)
                y_top = p_top if y_top is None else y_top + p_top
                y_bot = p_bot if y_bot is None else y_bot + p_bot
            y_pair = jnp.concatenate([y_top, y_bot], axis=0)
            yacc_ref[:, ps] = (y_pair + y_off[:, pair * LANES:(pair + 1) * LANES]
                               + xp.astype(F32) * dskip_ref[:, ps])

    for g in range(SSD_GROUPS):
        gs = slice(g * GROUP_CH, (g + 1) * GROUP_CH)
        yg = yacc_ref[:, gs] * zs_ref[:, gs].astype(F32)
        y_ref[:, gs] = _rms(yg, normw_ref[:, gs]).astype(y_ref.dtype)


def _ssd(proj, dt_raw, dtb, alog, dskip_e, normw, expand, tril, batch, seq):
    t = proj.shape[0]
    L = SSD_CHUNK
    nc = seq // L
    row = lambda b, c: b * nc + c
    return pl.pallas_call(
        _ssd_kernel,
        out_shape=jax.ShapeDtypeStruct((t, D_INNER), BF16),
        grid=(batch, nc),
        in_specs=[pl.BlockSpec((L, CONV_DIM), lambda b, c: (row(b, c), XBC_OFF // CONV_DIM)),
                  pl.BlockSpec((L, D_INNER), lambda b, c: (row(b, c), Z_OFF // D_INNER)),
                  pl.BlockSpec((L, DT_W), lambda b, c: (row(b, c), 0)),
                  _const_spec((1, DT_W)), _const_spec((1, DT_W)),
                  _const_spec((1, D_INNER)), _const_spec((1, D_INNER)),
                  _const_spec((DT_W, D_INNER)), _const_spec((L, L))],
        out_specs=pl.BlockSpec((L, D_INNER), lambda b, c: (row(b, c), 0)),
        scratch_shapes=[pltpu.VMEM((L, D_INNER), F32),
                        pltpu.VMEM((SSD_GROUPS, SSD_STATE, GROUP_CH), F32)],
        compiler_params=_params(("parallel", "arbitrary")),
        name="ssd",
    )(proj, proj, dt_raw, dtb, alog, dskip_e, normw, expand, tril)


def _attn_kernel(q_ref, k_ref, v_ref, lq1_ref, lk1_ref, lq2_ref, lk2_ref, subln_ref, o_ref,
                 q2_ref, vaug_ref, s0_ref, s1_ref, s2_ref, m_ref, acc_ref, *, tq, tk, rb, lambda_init):
    assert tq == tk and tq % rb == 0
    seq = q_ref.shape[0]
    nq = seq // tq

    lane = lax.broadcasted_iota(jnp.int32, (tq, HEAD_W), 1)
    comp0 = lane < DIFF_HEAD_DIM
    for t in range(nq):
        q = q_ref[t * tq:(t + 1) * tq, :]
        zero = jnp.zeros_like(q)
        q2_ref[t, 0:tq, :] = jnp.where(comp0, q, zero)
        q2_ref[t, tq:2 * tq, :] = jnp.where(comp0, zero, q)
    vaug_ref[:, 0:HEAD_W] = v_ref[...]
    vaug_ref[:, HEAD_W:2 * HEAD_W] = jnp.ones((seq, HEAD_W), BF16)

    lam = (jnp.exp(jnp.sum(lq1_ref[...] * lk1_ref[...], axis=-1, keepdims=True))
           - jnp.exp(jnp.sum(lq2_ref[...] * lk2_ref[...], axis=-1, keepdims=True)) + lambda_init)

    def keys_needed(diag, r):
        return (r * rb) % tq + rb if diag else tk

    def scores_block(qi, j, s_ref, diag, r):
        rows = slice(r * rb, (r + 1) * rb)
        nk = keys_needed(diag, r)
        start = pl.multiple_of(j * tk, tk)
        s_ref[rows, 0:nk] = lax.dot_general(q2_ref[qi, rows, :], k_ref[pl.ds(start, nk), :],
                                            (((1,), (1,)), ((), ())), preferred_element_type=F32)

    def softmax_block(j, s_ref, diag, r):
        rows = slice(r * rb, (r + 1) * rb)
        nk = keys_needed(diag, r)
        start = pl.multiple_of(j * tk, tk)
        s = s_ref[rows, 0:nk]
        if diag:
            qpos = (r * rb) % tq + lax.broadcasted_iota(jnp.int32, (rb, nk), 0)
            s = jnp.where(lax.broadcasted_iota(jnp.int32, (rb, nk), 1) <= qpos, s, NEG_BIG)
        m_prev = m_ref[rows, :]
        m_next = jnp.maximum(m_prev, jnp.max(s, axis=1, keepdims=True))
        alpha = jnp.exp2(m_prev - m_next)
        p = jnp.exp2(s - jnp.concatenate([m_next] * (nk // LANES), axis=1)).astype(BF16)
        acc_ref[rows, :] = (jnp.concatenate([alpha, alpha], axis=1) * acc_ref[rows, :]
                            + _dot(p, vaug_ref[pl.ds(start, nk), :]))
        m_ref[rows, :] = m_next

    def scores(qi, j, s_ref, diag):
        for r in range(2 * tq // rb):
            scores_block(qi, j, s_ref, diag, r)

    def scores_softmax(nxt, cur):
        for r in range(2 * tq // rb):
            scores_block(*nxt, r)
            softmax_block(*cur, r)

    def finish(qi):
        o = (acc_ref[0:tq, 0:HEAD_W] / acc_ref[0:tq, HEAD_W:2 * HEAD_W]
             - lam * (acc_ref[tq:2 * tq, 0:HEAD_W] / acc_ref[tq:2 * tq, HEAD_W:2 * HEAD_W]))
        start = pl.multiple_of(qi * tq, tq)
        o_ref[pl.ds(start, tq), :] = (_rms(o, subln_ref[...]) * (1.0 - lambda_init)).astype(o_ref.dtype)

    scores(0, 0, s2_ref, True)

    def q_tile(qi, carry):
        m_ref[...] = jnp.full(m_ref.shape, NEG_BIG, F32)
        acc_ref[...] = jnp.zeros(acc_ref.shape, F32)

        def pair(i, c):
            scores_softmax((qi, 2 * i + 1, s1_ref, False), (2 * i, s0_ref, False))
            scores_softmax((qi, 2 * i + 2, s0_ref, False), (2 * i + 1, s1_ref, False))
            return c

        n_pairs = jnp.maximum(qi - 1, 0) // 2
        lax.fori_loop(0, n_pairs, pair, 0)

        @pl.when(qi % 2 == 1)
        def _():
            scores_softmax((qi, qi, s2_ref, True), (qi - 1, s0_ref, False))

        @pl.when((qi % 2 == 0) & (qi > 0))
        def _():
            scores_softmax((qi, qi - 1, s1_ref, False), (qi - 2, s0_ref, False))
            scores_softmax((qi, qi, s2_ref, True), (qi - 1, s1_ref, False))

        scores_softmax((jnp.minimum(qi + 1, nq - 1), 0, s0_ref, False), (qi, s2_ref, True))
        finish(qi)
        return carry

    lax.fori_loop(0, nq, q_tile, 0)


def _attn(proj, lq1, lk1, lq2, lk2, subln, batch, seq, lambda_init):
    t = proj.shape[0]
    tq, tk, rb = 512, 512, 256
    nq = seq // tq
    lam_spec = _const_spec((1, DIFF_HEAD_DIM))
    head_cols = lambda off: (lambda b, h: (b, off // HEAD_W + h))
    return pl.pallas_call(
        functools.partial(_attn_kernel, tq=tq, tk=tk, rb=rb, lambda_init=lambda_init),
        out_shape=jax.ShapeDtypeStruct((t, DIFF_V), BF16),
        grid=(batch, DIFF_HEADS),
        in_specs=[pl.BlockSpec((seq, HEAD_W), head_cols(Q_OFF)),
                  pl.BlockSpec((seq, HEAD_W), head_cols(K_OFF)),
                  pl.BlockSpec((seq, HEAD_W), head_cols(V_OFF)),
                  lam_spec, lam_spec, lam_spec, lam_spec, _const_spec((1, HEAD_W))],
        out_specs=pl.BlockSpec((seq, HEAD_W), lambda b, h: (b, h)),
        scratch_shapes=[pltpu.VMEM((nq, 2 * tq, HEAD_W), BF16),
                        pltpu.VMEM((seq, 2 * HEAD_W), BF16),
                        pltpu.VMEM((2 * tq, tk), F32),
                        pltpu.VMEM((2 * tq, tk), F32),
                        pltpu.VMEM((2 * tq, tk), F32),
                        pltpu.VMEM((2 * tq, LANES), F32),
                        pltpu.VMEM((2 * tq, 2 * HEAD_W), F32)],
        compiler_params=_params(("parallel", "parallel")),
        name="diff_attn",
    )(proj, proj, proj, lq1, lk1, lq2, lk2, subln)


def _merge_kernel(y_ref, o_ref, gate_ref, x_ref, mod_ref, wos_ref, woa_ref, wout_ref, nw_ref, out_ref):
    y_ssd = _dot(y_ref[...], wos_ref[...])
    y_attn = _dot(o_ref[...], woa_ref[...])
    g_ssd = gate_ref[:, 0:D_MODEL].astype(F32)
    g_attn = gate_ref[:, D_MODEL:2 * D_MODEL].astype(F32)
    merged = (g_ssd * y_ssd + g_attn * y_attn).astype(BF16)
    mix = _dot(merged, wout_ref[...])
    g1 = mod_ref[0][:, 2 * D_MODEL:3 * D_MODEL]
    out_ref[...] = x_ref[...] + g1 * _rms(mix, nw_ref[...])


def _merge(y, o, proj, x2, mod3, wos, woa, wout, nw, seq):
    t = x2.shape[0]
    tm = 512
    per_b = seq // tm
    return pl.pallas_call(
        _merge_kernel,
        out_shape=jax.ShapeDtypeStruct((t, D_MODEL), F32),
        grid=(t // tm,),
        in_specs=[pl.BlockSpec((tm, D_INNER), lambda i: (i, 0)),
                  pl.BlockSpec((tm, DIFF_V), lambda i: (i, 0)),
                  pl.BlockSpec((tm, 2 * D_MODEL), lambda i: (i, GATE_OFF // (2 * D_MODEL))),
                  pl.BlockSpec((tm, D_MODEL), lambda i: (i, 0)),
                  pl.BlockSpec((1, 1, 6 * D_MODEL), lambda i: (i // per_b, 0, 0)),
                  _const_spec((D_INNER, D_MODEL)), _const_spec((DIFF_V, D_MODEL)),
                  _const_spec((D_MODEL, D_MODEL)), _const_spec((1, D_MODEL))],
        out_specs=pl.BlockSpec((tm, D_MODEL), lambda i: (i, 0)),
        compiler_params=_params(("parallel",)),
        name="merge",
    )(y, o, proj, x2, mod3, wos, woa, wout, nw)


def _ffn_kernel(x_ref, mod_ref, npre_ref, npost_ref, wg_ref, wu_ref, wd_ref, out_ref):
    mod = mod_ref[0]
    x = x_ref[...]
    h = (_rms(x, npre_ref[...]) * (1.0 + mod[:, 4 * D_MODEL:5 * D_MODEL]) + mod[:, 3 * D_MODEL:4 * D_MODEL]).astype(BF16)
    act = (_silu(_dot(h, wg_ref[...])) * _dot(h, wu_ref[...])).astype(BF16)
    f = _dot(act, wd_ref[...])
    out_ref[...] = x + mod[:, 5 * D_MODEL:6 * D_MODEL] * _rms(f, npost_ref[...])


def _ffn(x1, mod3, npre, npost, wg, wu, wd, seq):
    t = x1.shape[0]
    tm = 512
    per_b = seq // tm
    return pl.pallas_call(
        _ffn_kernel,
        out_shape=jax.ShapeDtypeStruct((t, D_MODEL), F32),
        grid=(t // tm,),
        in_specs=[pl.BlockSpec((tm, D_MODEL), lambda i: (i, 0)),
                  pl.BlockSpec((1, 1, 6 * D_MODEL), lambda i: (i // per_b, 0, 0)),
                  _const_spec((1, D_MODEL)), _const_spec((1, D_MODEL)),
                  _const_spec((D_MODEL, D_FF)), _const_spec((D_MODEL, D_FF)), _const_spec((D_FF, D_MODEL))],
        out_specs=pl.BlockSpec((tm, D_MODEL), lambda i: (i, 0)),
        compiler_params=_params(("parallel",)),
        name="ffn",
    )(x1, mod3, npre, npost, wg, wu, wd)


def _layer(x2, mod3, pos2, batch, seq, lambda_init, norm_pre_mix, norm_post_mix, norm_pre_ffn, norm_post_ffn,
           w_in, conv_w, conv_b, dt_bias, a_log, d_skip, ssd_norm, w_o_ssd,
           lambda_q1, lambda_k1, lambda_q2, lambda_k2, subln, w_o_attn, w_out, w_gate, w_up, w_down):
    row = lambda v: v.reshape(1, -1).astype(F32)
    w_z, w_xbc, w_dt, w_q, w_k, w_v, w_g = jnp.split(w_in, IN_SPLITS, axis=1)
    w_wide = jnp.concatenate([w_xbc, w_v, w_z, w_g, w_k, w_q], axis=1).astype(BF16)
    w_dt = jnp.pad(w_dt, ((0, 0), (0, DT_W - SSD_HEADS))).astype(BF16)
    inv_freq = 1.0 / (ROPE_THETA ** (jnp.arange(0, DIFF_HEAD_DIM, 2, dtype=F32) / DIFF_HEAD_DIM))
    invf = jnp.tile(inv_freq, LANES // inv_freq.shape[0]).reshape(1, LANES)
    half_lane = jnp.arange(LANES) % DIFF_HEAD_DIM < DIFF_HEAD_DIM // 2
    sgn = jnp.where(half_lane, -1.0, 1.0).astype(F32).reshape(1, LANES)

    proj, dt_raw = _in_proj(x2, mod3, row(norm_pre_mix), pos2, invf, sgn, w_wide, w_dt,
                            conv_w.T.astype(F32), row(conv_b), seq)

    pad_h = lambda v: jnp.pad(v.astype(F32), (0, DT_W - SSD_HEADS)).reshape(1, DT_W)
    expand = (jnp.arange(DT_W)[:, None] == (jnp.arange(D_INNER)[None, :] // SSD_HEAD_DIM)).astype(BF16)
    tril = (jnp.arange(SSD_CHUNK)[None, :] <= jnp.arange(SSD_CHUNK)[:, None]).astype(BF16)
    y = _ssd(proj, dt_raw, pad_h(dt_bias), pad_h(a_log), row(jnp.repeat(d_skip, SSD_HEAD_DIM)), row(ssd_norm),
             expand, tril, batch, seq)

    o = _attn(proj, row(lambda_q1), row(lambda_k1), row(lambda_q2), row(lambda_k2), row(subln),
              batch, seq, lambda_init)

    x1 = _merge(y, o, proj, x2, mod3, w_o_ssd.astype(BF16), w_o_attn.astype(BF16), w_out.astype(BF16),
                row(norm_post_mix), seq)
    return _ffn(x1, mod3, row(norm_pre_ffn), row(norm_post_ffn), w_gate.astype(BF16), w_up.astype(BF16),
                w_down.astype(BF16), seq)


def kernel(x, c, positions, w_ada, b_ada, norm_pre_mix, norm_post_mix, norm_pre_ffn, norm_post_ffn, w_in, conv_w, conv_b, dt_bias, a_log, d_skip, ssd_norm, w_o_ssd, lambda_q1, lambda_k1, lambda_q2, lambda_k2, subln, w_o_attn, w_out, w_gate, w_up, w_down):
    batch, seq, _ = x.shape
    depth = w_in.shape[0]
    x2 = x.reshape(batch * seq, D_MODEL)
    run = DIFF_HEAD_DIM // 2
    pos2 = jnp.repeat(positions.reshape(-1, LANES // run, INPROJ_TM // 4).astype(F32).transpose(0, 2, 1), run,
                      axis=2).reshape(-1, LANES)
    c_pad = jnp.pad(c, ((0, SUBLANES - batch), (0, 0)))
    for layer in range(depth):
        lambda_init = 0.8 - 0.6 * math.exp(-0.3 * layer)
        mod3 = _ada(c_pad, w_ada[layer], b_ada[layer].reshape(1, -1))[:batch].reshape(batch, 1, 6 * D_MODEL)
        x2 = _layer(x2, mod3, pos2, batch, seq, lambda_init, norm_pre_mix[layer], norm_post_mix[layer],
                    norm_pre_ffn[layer], norm_post_ffn[layer], w_in[layer], conv_w[layer], conv_b[layer],
                    dt_bias[layer], a_log[layer], d_skip[layer], ssd_norm[layer], w_o_ssd[layer],
                    lambda_q1[layer], lambda_k1[layer], lambda_q2[layer], lambda_k2[layer], subln[layer],
                    w_o_attn[layer], w_out[layer], w_gate[layer], w_up[layer], w_down[layer])
    return x2.reshape(batch, seq, D_MODEL)
```

```python
import functools
import math

import jax
import jax.numpy as jnp
import numpy as np
from jax import lax
from jax.experimental import pallas as pl
from jax.experimental.pallas import tpu as pltpu

F32 = jnp.float32
BF16 = jnp.bfloat16

D_MODEL = 1024
D_INNER = 2048
SSD_HEAD_DIM = 64
SSD_HEADS = 32
SSD_GROUPS = 4
SSD_STATE = 128
SSD_CONV = 4
SSD_CHUNK = 256
CONV_DIM = D_INNER + 2 * SSD_GROUPS * SSD_STATE
GROUP_CH = D_INNER // SSD_GROUPS
DIFF_HEADS = 8
DIFF_HEAD_DIM = 64
DIFF_QK = 1024
DIFF_V = 1024
ROPE_THETA = 10000.0
D_FF = 2816
EPS = 1e-6
IN_SIZES = (D_INNER, CONV_DIM, SSD_HEADS, DIFF_QK, DIFF_QK, DIFF_V, 2 * D_MODEL)
IN_SPLITS = tuple(int(v) for v in np.cumsum(IN_SIZES)[:-1])

LANES = 128
SUBLANES = 8
HEAD_W = 2 * DIFF_HEAD_DIM

XBC_OFF, V_OFF, Z_OFF, GATE_OFF, K_OFF, Q_OFF = 0, 3072, 4096, 6144, 8192, 9216
PROJ_W = 10240
PROJ_TN = 1024
DT_W = LANES

VMEM_LIMIT = 56 * 1024 * 1024

LOG2E = math.log2(math.e)
Q_SCALE = DIFF_HEAD_DIM ** -0.5 * LOG2E
NEG_BIG = -1e30


def _sigmoid(v):
    return 1.0 / (1.0 + jnp.exp(-v))


def _silu(v):
    return v * _sigmoid(v)


def _rms(v, w):
    return v * lax.rsqrt(jnp.mean(v * v, axis=-1, keepdims=True) + EPS) * w


def _dot(a, b):
    return jnp.dot(a, b, preferred_element_type=F32)


def _params(sem):
    return pltpu.CompilerParams(dimension_semantics=sem, vmem_limit_bytes=VMEM_LIMIT)


def _const_spec(shape):
    nd = len(shape)
    return pl.BlockSpec(shape, lambda *_: (0,) * nd, pipeline_mode=pl.Buffered(1))


def _ada_kernel(c_ref, w_ref, b_ref, o_ref):
    cond = _silu(c_ref[...])
    o_ref[...] = _dot(cond.astype(BF16), w_ref[...].astype(BF16)) + b_ref[...]


def _ada(c_pad, w_ada, b_ada):
    n = w_ada.shape[1]
    tn = 1024
    return pl.pallas_call(
        _ada_kernel,
        out_shape=jax.ShapeDtypeStruct((c_pad.shape[0], n), F32),
        grid=(n // tn,),
        in_specs=[pl.BlockSpec(c_pad.shape, lambda j: (0, 0)),
                  pl.BlockSpec((D_MODEL, tn), lambda j: (0, j)),
                  pl.BlockSpec((1, tn), lambda j: (0, j))],
        out_specs=pl.BlockSpec((c_pad.shape[0], tn), lambda j: (0, j)),
        compiler_params=_params(("arbitrary",)),
        name="ada",
    )(c_pad, w_ada, b_ada)


COL_KINDS = ("xbc", "xbc", "xbc", "v", "z", "z", "gate", "gate", "k", "q")
N_COL = len(COL_KINDS)
INPROJ_TM = 1024
TRIG_COL = 4
EPI_RB = 128
TRIG_RB = 64
MM_ROWS, MM_COLS = 256, 256
HALO = SUBLANES


def _sigmoid_t(v):
    return 0.5 + 0.5 * jnp.tanh(0.5 * v)


def _silu_t(v):
    h = 0.5 * v
    return h + h * jnp.tanh(h)


def _inproj_kernel(x_ref, mod_ref, nw_ref, posc_ref, invf_ref, sgn_ref, w_ref, wdt_ref, convw_ref, convb_ref,
                   out_ref, dt_ref, h_ref, park0_ref, park1_ref, cos_ref, sin_ref, halo_ref,
                   *, tiles_per_seq, n_row_tiles):
    i, j = pl.program_id(0), pl.program_id(1)
    tm = x_ref.shape[0]
    parks = (park0_ref, park1_ref)

    @pl.when((j == 0) & (i < n_row_tiles))
    def _():
        mod = mod_ref[0]
        h = _rms(x_ref[...], nw_ref[...]) * (1.0 + mod[:, D_MODEL:2 * D_MODEL]) + mod[:, 0:D_MODEL]
        hb = h.astype(BF16)
        h_ref[...] = hb
        dt_ref[...] = _dot(hb, wdt_ref[...])

    def matmul_pieces(col):
        park = parks[col % 2]

        def piece(rows, sl):
            def emit():
                park[rows, sl] = _dot(h_ref[rows, :], w_ref[:, col * PROJ_TN + sl.start:col * PROJ_TN + sl.stop])
            return emit

        return [piece(slice(m * MM_ROWS, (m + 1) * MM_ROWS), slice(n * MM_COLS, (n + 1) * MM_COLS))
                for n in range(PROJ_TN // MM_COLS) for m in range(tm // MM_ROWS)]

    def conv_piece(col, park, blk, slab):
        def emit():
            lo = blk * EPI_RB
            sl = slice(slab * LANES, (slab + 1) * LANES)
            cols = slice(col * PROJ_TN + slab * LANES, col * PROJ_TN + (slab + 1) * LANES)
            w = convw_ref[:, cols]
            b = convb_ref[:, cols]
            lead = jnp.where(first_tile, 0.0, halo_ref[:, cols]) if blk == 0 else park[lo - HALO:lo, sl]
            ext = jnp.concatenate([lead, park[lo:lo + EPI_RB, sl]], axis=0)
            conv = b + w[SSD_CONV - 1:SSD_CONV, :] * ext
            for s in range(1, SSD_CONV):
                conv = conv + w[SSD_CONV - 1 - s:SSD_CONV - s, :] * pltpu.roll(ext, s, 0)
            out_ref[lo:lo + EPI_RB, sl] = _silu_t(conv[HALO:, :]).astype(BF16)
            if blk == tm // EPI_RB - 1:
                halo_ref[:, cols] = park[tm - HALO:tm, sl]
        return emit

    def rope_piece(park, scale, blk, hd):
        def emit():
            lane = lax.broadcasted_iota(jnp.int32, (EPI_RB, HEAD_W), 1)
            first_half = (lane % DIFF_HEAD_DIM) < (DIFF_HEAD_DIM // 2)
            rows = slice(blk * EPI_RB, (blk + 1) * EPI_RB)
            sl = slice(hd * HEAD_W, (hd + 1) * HEAD_W)
            xh = park[rows, sl]
            partner = jnp.where(first_half, pltpu.roll(xh, HEAD_W - DIFF_HEAD_DIM // 2, 1),
                                pltpu.roll(xh, DIFF_HEAD_DIM // 2, 1))
            out_ref[rows, sl] = ((xh * cos_ref[rows, :] + partner * sin_ref[rows, :]) * scale).astype(BF16)
        return emit

    def map_piece(park, fn, blk, slab):
        def emit():
            rows = slice(blk * EPI_RB, (blk + 1) * EPI_RB)
            sl = slice(slab * LANES, (slab + 1) * LANES)
            out_ref[rows, sl] = fn(park[rows, sl]).astype(BF16)
        return emit

    def trig_piece(blk):
        def emit():
            quarter, run = tm // 4, DIFF_HEAD_DIM // 2
            lane = lax.broadcasted_iota(jnp.int32, (TRIG_RB, LANES), 1)
            ang = posc_ref[blk * TRIG_RB:(blk + 1) * TRIG_RB, :] * invf_ref[...]
            for table, ref in ((jnp.cos(ang), cos_ref), (jnp.sin(ang), sin_ref)):
                for g in range(4):
                    base = table if g == 0 else pltpu.roll(table, LANES - g * run, 1)
                    t = jnp.where(lane < run, base, pltpu.roll(base, run, 1))
                    t = jnp.where(lane < 2 * run, t, pltpu.roll(t, 2 * run, 1))
                    rows = slice(g * quarter + blk * TRIG_RB, g * quarter + (blk + 1) * TRIG_RB)
                    ref[rows, :] = t * sgn_ref[...] if ref is sin_ref else t
        return emit

    def epilogue_pieces(col):
        kind, park = COL_KINDS[col], parks[col % 2]
        grid = [(blk, slab) for blk in range(tm // EPI_RB) for slab in range(PROJ_TN // LANES)]
        if kind == "xbc":
            return [conv_piece(col, park, blk, slab) for blk, slab in grid]
        if kind in ("k", "q"):
            return [rope_piece(park, Q_SCALE if kind == "q" else 1.0, blk, hd) for blk, hd in grid]
        fn = {"v": lambda r: r, "z": _silu_t, "gate": _sigmoid_t}[kind]
        return [map_piece(park, fn, blk, slab) for blk, slab in grid]

    def fused(mm_col, ep_col, trig=False):
        mm = matmul_pieces(mm_col) if mm_col is not None else []
        ep = epilogue_pieces(ep_col) if ep_col is not None else []
        if trig:
            ep = ep + [trig_piece(blk) for blk in range(tm // 4 // TRIG_RB)]
        n = max(len(mm), 1)
        for idx in range(n):
            for emit in mm[idx:idx + 1] + ep[idx * len(ep) // n:(idx + 1) * len(ep) // n]:
                emit()

    first_tile = i % tiles_per_seq == 0

    @pl.when((i == 0) & (j == 0))
    def _():
        halo_ref[...] = jnp.zeros(halo_ref.shape, F32)
        fused(0, None)

    @pl.when((j == 0) & (i > 0) & (i < n_row_tiles))
    def _():
        fused(0, N_COL - 1)

    @pl.when((j == 0) & (i == n_row_tiles))
    def _():
        fused(None, N_COL - 1)

    for col in range(1, N_COL):
        @pl.when((j == col) & (i < n_row_tiles))
        def _(col=col):
            fused(col, col - 1, trig=col == TRIG_COL)


def _in_proj(x2, mod3, nw, pos2, invf, sgn, w_wide, w_dt, convw_t, convb, seq):
    t = x2.shape[0]
    tm, tn = INPROJ_TM, PROJ_TN
    n_rows = t // tm
    per_b = seq // tm
    row = lambda i, j: jnp.minimum(i, n_rows - 1)
    lagged = lambda i, j: (jnp.clip(jnp.where(j == 0, i - 1, i), 0, n_rows - 1),
                           jnp.where((j == 0) | (i == n_rows), jnp.where(i == 0, 0, N_COL - 1), j - 1))
    return pl.pallas_call(
        functools.partial(_inproj_kernel, tiles_per_seq=per_b, n_row_tiles=n_rows),
        out_shape=(jax.ShapeDtypeStruct((t, PROJ_W), BF16), jax.ShapeDtypeStruct((t, DT_W), F32)),
        grid=(n_rows + 1, N_COL),
        in_specs=[pl.BlockSpec((tm, D_MODEL), lambda i, j: (row(i, j), 0)),
                  pl.BlockSpec((1, 1, 6 * D_MODEL), lambda i, j: (row(i, j) // per_b, 0, 0)),
                  _const_spec((1, D_MODEL)),
                  pl.BlockSpec((tm // 4, LANES), lambda i, j: (row(i, j), 0)),
                  _const_spec((1, LANES)), _const_spec((1, LANES)),
                  _const_spec((D_MODEL, PROJ_W)),
                  _const_spec((D_MODEL, DT_W)),
                  _const_spec((SSD_CONV, CONV_DIM)),
                  _const_spec((1, CONV_DIM))],
        out_specs=(pl.BlockSpec((tm, tn), lagged),
                   pl.BlockSpec((tm, DT_W), lambda i, j: (row(i, j), 0))),
        scratch_shapes=[pltpu.VMEM((tm, D_MODEL), BF16),
                        pltpu.VMEM((tm, tn), F32),
                        pltpu.VMEM((tm, tn), F32),
                        pltpu.VMEM((tm, LANES), F32),
                        pltpu.VMEM((tm, LANES), F32),
                        pltpu.VMEM((HALO, CONV_DIM), F32)],
        compiler_params=_params(("arbitrary", "arbitrary")),
        name="in_proj",
    )(x2, mod3, nw, pos2, invf, sgn, w_wide, w_dt, convw_t, convb)


def _split2(v):
    hi = v.astype(BF16)
    lo = (v - hi.astype(F32)).astype(BF16)
    return hi, lo


def _split3(v):
    hi = v.astype(BF16)
    r = v - hi.astype(F32)
    mid = r.astype(BF16)
    lo = (r - mid.astype(F32)).astype(BF16)
    return hi, mid, lo


def _ssd_kernel(act_ref, zs_ref, dt_ref, dtb_ref, alog_ref, dskip_ref, normw_ref, expand_ref, tril_ref,
                y_ref, yacc_ref, state_ref):
    L = SSD_CHUNK
    H = L // 2

    @pl.when(pl.program_id(1) == 0)
    def _():
        state_ref[...] = jnp.zeros(state_ref.shape, F32)

    dt = dt_ref[...] + dtb_ref[...]
    dt = jnp.maximum(dt, 0.0) + jnp.log1p(jnp.exp(-jnp.abs(dt)))
    a = dt * (-LOG2E * jnp.exp(alog_ref[...]))
    tril = tril_ref[...]
    cs = sum(_dot(tril, p) for p in _split3(a))
    key_t = cs.T - jnp.log2(dt.T)
    cs_last = cs[L - 1:L, :]
    ecs = jnp.exp2(cs).astype(BF16)
    dtw = (dt * jnp.exp2(cs_last - cs)).astype(BF16)
    cd16 = jnp.broadcast_to(jnp.exp2(cs_last), (2 * SUBLANES, DT_W))
    cd_e = sum(_dot(p, expand_ref[...]) for p in _split3(cd16))[0:1, :]

    tri = lax.broadcasted_iota(jnp.int32, (H, H), 1) <= lax.broadcasted_iota(jnp.int32, (H, H), 0)
    first_head = lax.broadcasted_iota(jnp.int32, (L, LANES), 1) < SSD_HEAD_DIM

    for g in range(SSD_GROUPS):
        gs = slice(g * GROUP_CH, (g + 1) * GROUP_CH)
        b_g = act_ref[:, D_INNER + g * SSD_STATE:D_INNER + (g + 1) * SSD_STATE]
        c_g = act_ref[:, D_INNER + (SSD_GROUPS + g) * SSD_STATE:D_INNER + (SSD_GROUPS + g + 1) * SSD_STATE]
        b_gt = b_g.astype(F32).T.astype(BF16)
        sc_top = _dot(c_g[0:H, :], b_gt[:, 0:H])
        sc_bot = _dot(c_g[H:L, :], b_gt)
        st = state_ref[g]
        y_off = _dot(c_g, st.astype(BF16)) * _dot(ecs, expand_ref[:, gs])
        xw = (act_ref[:, gs].astype(F32) * _dot(dtw, expand_ref[:, gs])).astype(BF16)
        state_ref[g] = st * cd_e[:, gs] + _dot(b_gt, xw)
        for pair in range(GROUP_CH // LANES):
            h0 = g * (SSD_HEADS // SSD_GROUPS) + 2 * pair
            ps = slice(g * GROUP_CH + pair * LANES, g * GROUP_CH + (pair + 1) * LANES)
            xp = act_ref[:, ps]
            zero = jnp.zeros_like(xp)
            y_top = y_bot = None
            for sub, rhs in ((0, jnp.where(first_head, xp, zero)), (1, jnp.where(first_head, zero, xp))):
                h = h0 + sub
                col = cs[:, h:h + 1]
                rowv = key_t[h:h + 1, :]
                d_tl = jnp.exp2(jnp.where(tri, col[0:H] - rowv[:, 0:H], NEG_BIG))
                d_bl = jnp.exp2(col[H:L] - rowv[:, 0:H])
                d_br = jnp.exp2(jnp.where(tri, col[H:L] - rowv[:, H:L], NEG_BIG))
                m_top = (sc_top * d_tl).astype(BF16)
                m_bot = jnp.concatenate([(sc_bot[:, 0:H] * d_bl).astype(BF16),
                                         (sc_bot[:, H:L] * d_br).astype(BF16)], axis=1)
                p_top = _dot(m_top, rhs[0:H, :])
                p_bot = _dot(m_bot, rhs)
                y_top = p_top if y_top is None else y_top + p_top
                y_bot = p_bot if y_bot is None else y_bot + p_bot
            y_pair = jnp.concatenate([y_top, y_bot], axis=0)
            yacc_ref[:, ps] = (y_pair + y_off[:, pair * LANES:(pair + 1) * LANES]
                               + xp.astype(F32) * dskip_ref[:, ps])

    for g in range(SSD_GROUPS):
        gs = slice(g * GROUP_CH, (g + 1) * GROUP_CH)
        yg = yacc_ref[:, gs] * zs_ref[:, gs].astype(F32)
        y_ref[:, gs] = _rms(yg, normw_ref[:, gs]).astype(y_ref.dtype)


def _ssd(proj, dt_raw, dtb, alog, dskip_e, normw, expand, tril, batch, seq):
    t = proj.shape[0]
    L = SSD_CHUNK
    nc = seq // L
    row = lambda b, c: b * nc + c
    return pl.pallas_call(
        _ssd_kernel,
        out_shape=jax.ShapeDtypeStruct((t, D_INNER), BF16),
        grid=(batch, nc),
        in_specs=[pl.BlockSpec((L, CONV_DIM), lambda b, c: (row(b, c), XBC_OFF // CONV_DIM)),
                  pl.BlockSpec((L, D_INNER), lambda b, c: (row(b, c), Z_OFF // D_INNER)),
                  pl.BlockSpec((L, DT_W), lambda b, c: (row(b, c), 0)),
                  _const_spec((1, DT_W)), _const_spec((1, DT_W)),
                  _const_spec((1, D_INNER)), _const_spec((1, D_INNER)),
                  _const_spec((DT_W, D_INNER)), _const_spec((L, L))],
        out_specs=pl.BlockSpec((L, D_INNER), lambda b, c: (row(b, c), 0)),
        scratch_shapes=[pltpu.VMEM((L, D_INNER), F32),
                        pltpu.VMEM((SSD_GROUPS, SSD_STATE, GROUP_CH), F32)],
        compiler_params=_params(("parallel", "arbitrary")),
        name="ssd",
    )(proj, proj, dt_raw, dtb, alog, dskip_e, normw, expand, tril)


def _attn_kernel(q_ref, k_ref, v_ref, lq1_ref, lk1_ref, lq2_ref, lk2_ref, subln_ref, o_ref,
                 q2_ref, vaug_ref, s0_ref, s1_ref, s2_ref, m_ref, acc_ref, *, tq, tk, rb, lambda_init):
    assert tq == tk and tq % rb == 0
    seq = q_ref.shape[0]
    nq = seq // tq

    lane = lax.broadcasted_iota(jnp.int32, (tq, HEAD_W), 1)
    comp0 = lane < DIFF_HEAD_DIM
    for t in range(nq):
        q = q_ref[t * tq:(t + 1) * tq, :]
        zero = jnp.zeros_like(q)
        q2_ref[t, 0:tq, :] = jnp.where(comp0, q, zero)
        q2_ref[t, tq:2 * tq, :] = jnp.where(comp0, zero, q)
    vaug_ref[:, 0:HEAD_W] = v_ref[...]
    vaug_ref[:, HEAD_W:2 * HEAD_W] = jnp.ones((seq, HEAD_W), BF16)

    lam = (jnp.exp(jnp.sum(lq1_ref[...] * lk1_ref[...], axis=-1, keepdims=True))
           - jnp.exp(jnp.sum(lq2_ref[...] * lk2_ref[...], axis=-1, keepdims=True)) + lambda_init)

    def keys_needed(diag, r):
        return (r * rb) % tq + rb if diag else tk

    def scores_block(qi, j, s_ref, diag, r):
        rows = slice(r * rb, (r + 1) * rb)
        nk = keys_needed(diag, r)
        start = j * tk
        s_ref[rows, 0:nk] = lax.dot_general(q2_ref[qi, rows, :], k_ref[pl.ds(start, nk), :],
                                            (((1,), (1,)), ((), ())), preferred_element_type=F32)

    def softmax_block(j, s_ref, diag, r):
        rows = slice(r * rb, (r + 1) * rb)
        nk = keys_needed(diag, r)
        start = j * tk
        s = s_ref[rows, 0:nk]
        if diag:
            qpos = (r * rb) % tq + lax.broadcasted_iota(jnp.int32, (rb, nk), 0)
            s = jnp.where(lax.broadcasted_iota(jnp.int32, (rb, nk), 1) <= qpos, s, NEG_BIG)
        m_prev = m_ref[rows, :]
        m_next = jnp.maximum(m_prev, jnp.max(s, axis=1, keepdims=True))
        alpha = jnp.exp2(m_prev - m_next)
        p = jnp.exp2(s - jnp.concatenate([m_next] * (nk // LANES), axis=1)).astype(BF16)
        acc_ref[rows, :] = (jnp.concatenate([alpha, alpha], axis=1) * acc_ref[rows, :]
                            + _dot(p, vaug_ref[pl.ds(start, nk), :]))
        m_ref[rows, :] = m_next

    def scores(qi, j, s_ref, diag):
        for r in range(2 * tq // rb):
            scores_block(qi, j, s_ref, diag, r)

    def scores_softmax(nxt, cur):
        for r in range(2 * tq // rb):
            scores_block(*nxt, r)
            softmax_block(*cur, r)

    def finish(qi):
        o = (acc_ref[0:tq, 0:HEAD_W] / acc_ref[0:tq, HEAD_W:2 * HEAD_W]
             - lam * (acc_ref[tq:2 * tq, 0:HEAD_W] / acc_ref[tq:2 * tq, HEAD_W:2 * HEAD_W]))
        o_ref[qi * tq:(qi + 1) * tq, :] = (_rms(o, subln_ref[...]) * (1.0 - lambda_init)).astype(o_ref.dtype)

    slots = (s0_ref, s1_ref)
    scores(0, 0, s2_ref, True)
    for qi in range(nq):
        m_ref[...] = jnp.full(m_ref.shape, NEG_BIG, F32)
        acc_ref[...] = jnp.zeros(acc_ref.shape, F32)
        for j in range(qi):
            nxt = (qi, j + 1, slots[(j + 1) % 2], False) if j + 1 < qi else (qi, qi, s2_ref, True)
            scores_softmax(nxt, (j, slots[j % 2], False))
        if qi + 1 < nq:
            scores_softmax((qi + 1, 0, s0_ref, False), (qi, s2_ref, True))
        else:
            for r in range(2 * tq // rb):
                softmax_block(qi, s2_ref, True, r)
        finish(qi)


def _attn(proj, lq1, lk1, lq2, lk2, subln, batch, seq, lambda_init):
    t = proj.shape[0]
    tq, tk, rb = 512, 512, 256
    nq = seq // tq
    lam_spec = _const_spec((1, DIFF_HEAD_DIM))
    head_cols = lambda off: (lambda b, h: (b, off // HEAD_W + h))
    return pl.pallas_call(
        functools.partial(_attn_kernel, tq=tq, tk=tk, rb=rb, lambda_init=lambda_init),
        out_shape=jax.ShapeDtypeStruct((t, DIFF_V), BF16),
        grid=(batch, DIFF_HEADS),
        in_specs=[pl.BlockSpec((seq, HEAD_W), head_cols(Q_OFF)),
                  pl.BlockSpec((seq, HEAD_W), head_cols(K_OFF)),
                  pl.BlockSpec((seq, HEAD_W), head_cols(V_OFF)),
                  lam_spec, lam_spec, lam_spec, lam_spec, _const_spec((1, HEAD_W))],
        out_specs=pl.BlockSpec((seq, HEAD_W), lambda b, h: (b, h)),
        scratch_shapes=[pltpu.VMEM((nq, 2 * tq, HEAD_W), BF16),
                        pltpu.VMEM((seq, 2 * HEAD_W), BF16),
                        pltpu.VMEM((2 * tq, tk), F32),
                        pltpu.VMEM((2 * tq, tk), F32),
                        pltpu.VMEM((2 * tq, tk), F32),
                        pltpu.VMEM((2 * tq, LANES), F32),
                        pltpu.VMEM((2 * tq, 2 * HEAD_W), F32)],
        compiler_params=_params(("parallel", "parallel")),
        name="diff_attn",
    )(proj, proj, proj, lq1, lk1, lq2, lk2, subln)


def _merge_kernel(y_ref, o_ref, gate_ref, x_ref, mod_ref, wos_ref, woa_ref, wout_ref, nw_ref, out_ref):
    y_ssd = _dot(y_ref[...], wos_ref[...])
    y_attn = _dot(o_ref[...], woa_ref[...])
    g_ssd = gate_ref[:, 0:D_MODEL].astype(F32)
    g_attn = gate_ref[:, D_MODEL:2 * D_MODEL].astype(F32)
    merged = (g_ssd * y_ssd + g_attn * y_attn).astype(BF16)
    mix = _dot(merged, wout_ref[...])
    g1 = mod_ref[0][:, 2 * D_MODEL:3 * D_MODEL]
    out_ref[...] = x_ref[...] + g1 * _rms(mix, nw_ref[...])


def _merge(y, o, proj, x2, mod3, wos, woa, wout, nw, seq):
    t = x2.shape[0]
    tm = 512
    per_b = seq // tm
    return pl.pallas_call(
        _merge_kernel,
        out_shape=jax.ShapeDtypeStruct((t, D_MODEL), F32),
        grid=(t // tm,),
        in_specs=[pl.BlockSpec((tm, D_INNER), lambda i: (i, 0)),
                  pl.BlockSpec((tm, DIFF_V), lambda i: (i, 0)),
                  pl.BlockSpec((tm, 2 * D_MODEL), lambda i: (i, GATE_OFF // (2 * D_MODEL))),
                  pl.BlockSpec((tm, D_MODEL), lambda i: (i, 0)),
                  pl.BlockSpec((1, 1, 6 * D_MODEL), lambda i: (i // per_b, 0, 0)),
                  _const_spec((D_INNER, D_MODEL)), _const_spec((DIFF_V, D_MODEL)),
                  _const_spec((D_MODEL, D_MODEL)), _const_spec((1, D_MODEL))],
        out_specs=pl.BlockSpec((tm, D_MODEL), lambda i: (i, 0)),
        compiler_params=_params(("parallel",)),
        name="merge",
    )(y, o, proj, x2, mod3, wos, woa, wout, nw)


def _ffn_kernel(x_ref, mod_ref, npre_ref, npost_ref, wg_ref, wu_ref, wd_ref, out_ref):
    mod = mod_ref[0]
    x = x_ref[...]
    h = (_rms(x, npre_ref[...]) * (1.0 + mod[:, 4 * D_MODEL:5 * D_MODEL]) + mod[:, 3 * D_MODEL:4 * D_MODEL]).astype(BF16)
    act = (_silu(_dot(h, wg_ref[...])) * _dot(h, wu_ref[...])).astype(BF16)
    f = _dot(act, wd_ref[...])
    out_ref[...] = x + mod[:, 5 * D_MODEL:6 * D_MODEL] * _rms(f, npost_ref[...])


def _ffn(x1, mod3, npre, npost, wg, wu, wd, seq):
    t = x1.shape[0]
    tm = 512
    per_b = seq // tm
    return pl.pallas_call(
        _ffn_kernel,
        out_shape=jax.ShapeDtypeStruct((t, D_MODEL), F32),
        grid=(t // tm,),
        in_specs=[pl.BlockSpec((tm, D_MODEL), lambda i: (i, 0)),
                  pl.BlockSpec((1, 1, 6 * D_MODEL), lambda i: (i // per_b, 0, 0)),
                  _const_spec((1, D_MODEL)), _const_spec((1, D_MODEL)),
                  _const_spec((D_MODEL, D_FF)), _const_spec((D_MODEL, D_FF)), _const_spec((D_FF, D_MODEL))],
        out_specs=pl.BlockSpec((tm, D_MODEL), lambda i: (i, 0)),
        compiler_params=_params(("parallel",)),
        name="ffn",
    )(x1, mod3, npre, npost, wg, wu, wd)


def _layer(x2, mod3, pos2, batch, seq, lambda_init, norm_pre_mix, norm_post_mix, norm_pre_ffn, norm_post_ffn,
           w_in, conv_w, conv_b, dt_bias, a_log, d_skip, ssd_norm, w_o_ssd,
           lambda_q1, lambda_k1, lambda_q2, lambda_k2, subln, w_o_attn, w_out, w_gate, w_up, w_down):
    row = lambda v: v.reshape(1, -1).astype(F32)
    w_z, w_xbc, w_dt, w_q, w_k, w_v, w_g = jnp.split(w_in, IN_SPLITS, axis=1)
    w_wide = jnp.concatenate([w_xbc, w_v, w_z, w_g, w_k, w_q], axis=1).astype(BF16)
    w_dt = jnp.pad(w_dt, ((0, 0), (0, DT_W - SSD_HEADS))).astype(BF16)
    inv_freq = 1.0 / (ROPE_THETA ** (jnp.arange(0, DIFF_HEAD_DIM, 2, dtype=F32) / DIFF_HEAD_DIM))
    invf = jnp.tile(inv_freq, LANES // inv_freq.shape[0]).reshape(1, LANES)
    half_lane = jnp.arange(LANES) % DIFF_HEAD_DIM < DIFF_HEAD_DIM // 2
    sgn = jnp.where(half_lane, -1.0, 1.0).astype(F32).reshape(1, LANES)

    proj, dt_raw = _in_proj(x2, mod3, row(norm_pre_mix), pos2, invf, sgn, w_wide, w_dt,
                            conv_w.T.astype(F32), row(conv_b), seq)

    pad_h = lambda v: jnp.pad(v.astype(F32), (0, DT_W - SSD_HEADS)).reshape(1, DT_W)
    expand = (jnp.arange(DT_W)[:, None] == (jnp.arange(D_INNER)[None, :] // SSD_HEAD_DIM)).astype(BF16)
    tril = (jnp.arange(SSD_CHUNK)[None, :] <= jnp.arange(SSD_CHUNK)[:, None]).astype(BF16)
    y = _ssd(proj, dt_raw, pad_h(dt_bias), pad_h(a_log), row(jnp.repeat(d_skip, SSD_HEAD_DIM)), row(ssd_norm),
             expand, tril, batch, seq)

    o = _attn(proj, row(lambda_q1), row(lambda_k1), row(lambda_q2), row(lambda_k2), row(subln),
              batch, seq, lambda_init)

    x1 = _merge(y, o, proj, x2, mod3, w_o_ssd.astype(BF16), w_o_attn.astype(BF16), w_out.astype(BF16),
                row(norm_post_mix), seq)
    return _ffn(x1, mod3, row(norm_pre_ffn), row(norm_post_ffn), w_gate.astype(BF16), w_up.astype(BF16),
                w_down.astype(BF16), seq)


def kernel(x, c, positions, w_ada, b_ada, norm_pre_mix, norm_post_mix, norm_pre_ffn, norm_post_ffn, w_in, conv_w, conv_b, dt_bias, a_log, d_skip, ssd_norm, w_o_ssd, lambda_q1, lambda_k1, lambda_q2, lambda_k2, subln, w_o_attn, w_out, w_gate, w_up, w_down):
    batch, seq, _ = x.shape
    depth = w_in.shape[0]
    x2 = x.reshape(batch * seq, D_MODEL)
    run = DIFF_HEAD_DIM // 2
    pos2 = jnp.repeat(positions.reshape(-1, LANES // run, INPROJ_TM // 4).astype(F32).transpose(0, 2, 1), run,
                      axis=2).reshape(-1, LANES)
    c_pad = jnp.pad(c, ((0, SUBLANES - batch), (0, 0)))
    for layer in range(depth):
        lambda_init = 0.8 - 0.6 * math.exp(-0.3 * layer)
        mod3 = _ada(c_pad, w_ada[layer], b_ada[layer].reshape(1, -1))[:batch].reshape(batch, 1, 6 * D_MODEL)
        x2 = _layer(x2, mod3, pos2, batch, seq, lambda_init, norm_pre_mix[layer], norm_post_mix[layer],
                    norm_pre_ffn[layer], norm_post_ffn[layer], w_in[layer], conv_w[layer], conv_b[layer],
                    dt_bias[layer], a_log[layer], d_skip[layer], ssd_norm[layer], w_o_ssd[layer],
                    lambda_q1[layer], lambda_k1[layer], lambda_q2[layer], lambda_k2[layer], subln[layer],
                    w_o_attn[layer], w_out[layer], w_gate[layer], w_up[layer], w_down[layer])
    return x2.reshape(batch, seq, D_MODEL)
```

```python
import functools
import math

import jax
import jax.numpy as jnp
import numpy as np
from jax import lax
from jax.experimental import pallas as pl
from jax.experimental.pallas import tpu as pltpu

F32 = jnp.float32
BF16 = jnp.bfloat16

D_MODEL = 1024
D_INNER = 2048
SSD_HEAD_DIM = 64
SSD_HEADS = 32
SSD_GROUPS = 4
SSD_STATE = 128
SSD_CONV = 4
SSD_CHUNK = 256
CONV_DIM = D_INNER + 2 * SSD_GROUPS * SSD_STATE
GROUP_CH = D_INNER // SSD_GROUPS
DIFF_HEADS = 8
DIFF_HEAD_DIM = 64
DIFF_QK = 1024
DIFF_V = 1024
ROPE_THETA = 10000.0
D_FF = 2816
EPS = 1e-6
IN_SIZES = (D_INNER, CONV_DIM, SSD_HEADS, DIFF_QK, DIFF_QK, DIFF_V, 2 * D_MODEL)
IN_SPLITS = tuple(int(v) for v in np.cumsum(IN_SIZES)[:-1])

LANES = 128
SUBLANES = 8
HEAD_W = 2 * DIFF_HEAD_DIM

XBC_OFF, V_OFF, Z_OFF, GATE_OFF, K_OFF, Q_OFF = 0, 3072, 4096, 6144, 8192, 9216
PROJ_W = 10240
PROJ_TN = 1024
DT_W = LANES

VMEM_LIMIT = 56 * 1024 * 1024

LOG2E = math.log2(math.e)
Q_SCALE = DIFF_HEAD_DIM ** -0.5 * LOG2E
NEG_BIG = -1e30


def _sigmoid(v):
    return 1.0 / (1.0 + jnp.exp(-v))


def _silu(v):
    return v * _sigmoid(v)


def _rms(v, w):
    return v * lax.rsqrt(jnp.mean(v * v, axis=-1, keepdims=True) + EPS) * w


def _dot(a, b):
    return jnp.dot(a, b, preferred_element_type=F32)


def _params(sem):
    return pltpu.CompilerParams(dimension_semantics=sem, vmem_limit_bytes=VMEM_LIMIT)


def _const_spec(shape):
    nd = len(shape)
    return pl.BlockSpec(shape, lambda *_: (0,) * nd, pipeline_mode=pl.Buffered(1))


def _ada_kernel(c_ref, w_ref, b_ref, o_ref):
    cond = _silu(c_ref[...])
    o_ref[...] = _dot(cond.astype(BF16), w_ref[...].astype(BF16)) + b_ref[...]


def _ada(c_pad, w_ada, b_ada):
    n = w_ada.shape[1]
    tn = 1024
    return pl.pallas_call(
        _ada_kernel,
        out_shape=jax.ShapeDtypeStruct((c_pad.shape[0], n), F32),
        grid=(n // tn,),
        in_specs=[pl.BlockSpec(c_pad.shape, lambda j: (0, 0)),
                  pl.BlockSpec((D_MODEL, tn), lambda j: (0, j)),
                  pl.BlockSpec((1, tn), lambda j: (0, j))],
        out_specs=pl.BlockSpec((c_pad.shape[0], tn), lambda j: (0, j)),
        compiler_params=_params(("arbitrary",)),
        name="ada",
    )(c_pad, w_ada, b_ada)


COL_KINDS = ("xbc", "xbc", "xbc", "v", "z", "z", "gate", "gate", "k", "q")
N_COL = len(COL_KINDS)
INPROJ_TM = 1024
TRIG_COL = 4
EPI_RB = 128
TRIG_RB = 64
MM_ROWS, MM_COLS = 256, 256
HALO = SUBLANES


def _sigmoid_t(v):
    return 0.5 + 0.5 * jnp.tanh(0.5 * v)


def _silu_t(v):
    h = 0.5 * v
    return h + h * jnp.tanh(h)


def _inproj_kernel(x_ref, mod_ref, nw_ref, posc_ref, invf_ref, sgn_ref, w_ref, wdt_ref, convw_ref, convb_ref,
                   out_ref, dt_ref, h_ref, park0_ref, park1_ref, cos_ref, sin_ref, halo_ref,
                   *, tiles_per_seq, n_row_tiles):
    i, j = pl.program_id(0), pl.program_id(1)
    tm = x_ref.shape[0]
    parks = (park0_ref, park1_ref)

    @pl.when((j == 0) & (i < n_row_tiles))
    def _():
        mod = mod_ref[0]
        h = _rms(x_ref[...], nw_ref[...]) * (1.0 + mod[:, D_MODEL:2 * D_MODEL]) + mod[:, 0:D_MODEL]
        hb = h.astype(BF16)
        h_ref[...] = hb
        dt_ref[...] = _dot(hb, wdt_ref[...])

    def matmul_pieces(col):
        park = parks[col % 2]

        def piece(rows, sl):
            def emit():
                park[rows, sl] = _dot(h_ref[rows, :], w_ref[:, col * PROJ_TN + sl.start:col * PROJ_TN + sl.stop])
            return emit

        return [piece(slice(m * MM_ROWS, (m + 1) * MM_ROWS), slice(n * MM_COLS, (n + 1) * MM_COLS))
                for n in range(PROJ_TN // MM_COLS) for m in range(tm // MM_ROWS)]

    def conv_piece(col, park, blk, slab):
        def emit():
            lo = blk * EPI_RB
            sl = slice(slab * LANES, (slab + 1) * LANES)
            cols = slice(col * PROJ_TN + slab * LANES, col * PROJ_TN + (slab + 1) * LANES)
            w = convw_ref[:, cols]
            b = convb_ref[:, cols]
            lead = jnp.where(first_tile, 0.0, halo_ref[:, cols]) if blk == 0 else park[lo - HALO:lo, sl]
            ext = jnp.concatenate([lead, park[lo:lo + EPI_RB, sl]], axis=0)
            conv = b + w[SSD_CONV - 1:SSD_CONV, :] * ext
            for s in range(1, SSD_CONV):
                conv = conv + w[SSD_CONV - 1 - s:SSD_CONV - s, :] * pltpu.roll(ext, s, 0)
            out_ref[lo:lo + EPI_RB, sl] = _silu_t(conv[HALO:, :]).astype(BF16)
            if blk == tm // EPI_RB - 1:
                halo_ref[:, cols] = park[tm - HALO:tm, sl]
        return emit

    def rope_piece(park, scale, blk, hd):
        def emit():
            lane = lax.broadcasted_iota(jnp.int32, (EPI_RB, HEAD_W), 1)
            first_half = (lane % DIFF_HEAD_DIM) < (DIFF_HEAD_DIM // 2)
            rows = slice(blk * EPI_RB, (blk + 1) * EPI_RB)
            sl = slice(hd * HEAD_W, (hd + 1) * HEAD_W)
            xh = park[rows, sl]
            partner = jnp.where(first_half, pltpu.roll(xh, HEAD_W - DIFF_HEAD_DIM // 2, 1),
                                pltpu.roll(xh, DIFF_HEAD_DIM // 2, 1))
            out_ref[rows, sl] = ((xh * cos_ref[rows, :] + partner * sin_ref[rows, :]) * scale).astype(BF16)
        return emit

    def map_piece(park, fn, blk, slab):
        def emit():
            rows = slice(blk * EPI_RB, (blk + 1) * EPI_RB)
            sl = slice(slab * LANES, (slab + 1) * LANES)
            out_ref[rows, sl] = fn(park[rows, sl]).astype(BF16)
        return emit

    def trig_piece(blk):
        def emit():
            quarter, run = tm // 4, DIFF_HEAD_DIM // 2
            lane = lax.broadcasted_iota(jnp.int32, (TRIG_RB, LANES), 1)
            ang = posc_ref[blk * TRIG_RB:(blk + 1) * TRIG_RB, :] * invf_ref[...]
            for table, ref in ((jnp.cos(ang), cos_ref), (jnp.sin(ang), sin_ref)):
                for g in range(4):
                    base = table if g == 0 else pltpu.roll(table, LANES - g * run, 1)
                    t = jnp.where(lane < run, base, pltpu.roll(base, run, 1))
                    t = jnp.where(lane < 2 * run, t, pltpu.roll(t, 2 * run, 1))
                    rows = slice(g * quarter + blk * TRIG_RB, g * quarter + (blk + 1) * TRIG_RB)
                    ref[rows, :] = t * sgn_ref[...] if ref is sin_ref else t
        return emit

    def epilogue_pieces(col):
        kind, park = COL_KINDS[col], parks[col % 2]
        grid = [(blk, slab) for blk in range(tm // EPI_RB) for slab in range(PROJ_TN // LANES)]
        if kind == "xbc":
            return [conv_piece(col, park, blk, slab) for blk, slab in grid]
        if kind in ("k", "q"):
            return [rope_piece(park, Q_SCALE if kind == "q" else 1.0, blk, hd) for blk, hd in grid]
        fn = {"v": lambda r: r, "z": _silu_t, "gate": _sigmoid_t}[kind]
        return [map_piece(park, fn, blk, slab) for blk, slab in grid]

    def fused(mm_col, ep_col, trig=False):
        mm = matmul_pieces(mm_col) if mm_col is not None else []
        ep = epilogue_pieces(ep_col) if ep_col is not None else []
        if trig:
            ep = ep + [trig_piece(blk) for blk in range(tm // 4 // TRIG_RB)]
        n = max(len(mm), 1)
        for idx in range(n):
            for emit in mm[idx:idx + 1] + ep[idx * len(ep) // n:(idx + 1) * len(ep) // n]:
                emit()

    first_tile = i % tiles_per_seq == 0

    @pl.when((i == 0) & (j == 0))
    def _():
        halo_ref[...] = jnp.zeros(halo_ref.shape, F32)
        fused(0, None)

    @pl.when((j == 0) & (i > 0) & (i < n_row_tiles))
    def _():
        fused(0, N_COL - 1)

    @pl.when((j == 0) & (i == n_row_tiles))
    def _():
        fused(None, N_COL - 1)

    for col in range(1, N_COL):
        @pl.when((j == col) & (i < n_row_tiles))
        def _(col=col):
            fused(col, col - 1, trig=col == TRIG_COL)


def _in_proj(x2, mod3, nw, pos2, invf, sgn, w_wide, w_dt, convw_t, convb, seq):
    t = x2.shape[0]
    tm, tn = INPROJ_TM, PROJ_TN
    n_rows = t // tm
    per_b = seq // tm
    row = lambda i, j: jnp.minimum(i, n_rows - 1)
    lagged = lambda i, j: (jnp.clip(jnp.where(j == 0, i - 1, i), 0, n_rows - 1),
                           jnp.where((j == 0) | (i == n_rows), jnp.where(i == 0, 0, N_COL - 1), j - 1))
    return pl.pallas_call(
        functools.partial(_inproj_kernel, tiles_per_seq=per_b, n_row_tiles=n_rows),
        out_shape=(jax.ShapeDtypeStruct((t, PROJ_W), BF16), jax.ShapeDtypeStruct((t, DT_W), F32)),
        grid=(n_rows + 1, N_COL),
        in_specs=[pl.BlockSpec((tm, D_MODEL), lambda i, j: (row(i, j), 0)),
                  pl.BlockSpec((1, 1, 6 * D_MODEL), lambda i, j: (row(i, j) // per_b, 0, 0)),
                  _const_spec((1, D_MODEL)),
                  pl.BlockSpec((tm // 4, LANES), lambda i, j: (row(i, j), 0)),
                  _const_spec((1, LANES)), _const_spec((1, LANES)),
                  _const_spec((D_MODEL, PROJ_W)),
                  _const_spec((D_MODEL, DT_W)),
                  _const_spec((SSD_CONV, CONV_DIM)),
                  _const_spec((1, CONV_DIM))],
        out_specs=(pl.BlockSpec((tm, tn), lagged),
                   pl.BlockSpec((tm, DT_W), lambda i, j: (row(i, j), 0))),
        scratch_shapes=[pltpu.VMEM((tm, D_MODEL), BF16),
                        pltpu.VMEM((tm, tn), F32),
                        pltpu.VMEM((tm, tn), F32),
                        pltpu.VMEM((tm, LANES), F32),
                        pltpu.VMEM((tm, LANES), F32),
                        pltpu.VMEM((HALO, CONV_DIM), F32)],
        compiler_params=_params(("arbitrary", "arbitrary")),
        name="in_proj",
    )(x2, mod3, nw, pos2, invf, sgn, w_wide, w_dt, convw_t, convb)


def _split3(v):
    hi = v.astype(BF16)
    r = v - hi.astype(F32)
    mid = r.astype(BF16)
    lo = (r - mid.astype(F32)).astype(BF16)
    return hi, mid, lo


def _ssd_kernel(act_ref, zs_ref, dt_ref, dtb_ref, alog_ref, dskip_ref, normw_ref, expand_ref, tril_ref,
                y_ref, yacc_ref, state_ref):
    L = SSD_CHUNK
    H = L // 2

    @pl.when(pl.program_id(1) == 0)
    def _():
        state_ref[...] = jnp.zeros(state_ref.shape, F32)

    dt = dt_ref[...] + dtb_ref[...]
    dt = jnp.maximum(dt, 0.0) + jnp.log1p(jnp.exp(-jnp.abs(dt)))
    a = dt * (-LOG2E * jnp.exp(alog_ref[...]))
    tril = tril_ref[...]
    cs = sum(_dot(tril, p) for p in _split3(a))
    key_t = cs.T - jnp.log2(dt.T)
    cs_last = cs[L - 1:L, :]
    ecs = jnp.exp2(cs).astype(BF16)
    dtw = (dt * jnp.exp2(cs_last - cs)).astype(BF16)
    cd16 = jnp.broadcast_to(jnp.exp2(cs_last), (2 * SUBLANES, DT_W))
    cd_e = sum(_dot(p, expand_ref[...]) for p in _split3(cd16))[0:1, :]

    tri = lax.broadcasted_iota(jnp.int32, (H, H), 1) <= lax.broadcasted_iota(jnp.int32, (H, H), 0)
    first_head = lax.broadcasted_iota(jnp.int32, (L, LANES), 1) < SSD_HEAD_DIM

    for g in range(SSD_GROUPS):
        gs = slice(g * GROUP_CH, (g + 1) * GROUP_CH)
        b_g = act_ref[:, D_INNER + g * SSD_STATE:D_INNER + (g + 1) * SSD_STATE]
        c_g = act_ref[:, D_INNER + (SSD_GROUPS + g) * SSD_STATE:D_INNER + (SSD_GROUPS + g + 1) * SSD_STATE]
        b_gt = b_g.astype(F32).T.astype(BF16)
        sc_top = _dot(c_g[0:H, :], b_gt[:, 0:H])
        sc_bot = _dot(c_g[H:L, :], b_gt)
        st = state_ref[g]
        y_off = _dot(c_g, st.astype(BF16)) * _dot(ecs, expand_ref[:, gs])
        xw = (act_ref[:, gs].astype(F32) * _dot(dtw, expand_ref[:, gs])).astype(BF16)
        state_ref[g] = st * cd_e[:, gs] + _dot(b_gt, xw)
        for pair in range(GROUP_CH // LANES):
            h0 = g * (SSD_HEADS // SSD_GROUPS) + 2 * pair
            ps = slice(g * GROUP_CH + pair * LANES, g * GROUP_CH + (pair + 1) * LANES)
            xp = act_ref[:, ps]
            zero = jnp.zeros_like(xp)
            y_top = y_bot = None
            for sub, rhs in ((0, jnp.where(first_head, xp, zero)), (1, jnp.where(first_head, zero, xp))):
                h = h0 + sub
                col = cs[:, h:h + 1]
                rowv = key_t[h:h + 1, :]
                d_tl = jnp.exp2(jnp.where(tri, col[0:H] - rowv[:, 0:H], NEG_BIG))
                d_bl = jnp.exp2(col[H:L] - rowv[:, 0:H])
                d_br = jnp.exp2(jnp.where(tri, col[H:L] - rowv[:, H:L], NEG_BIG))
                m_top = (sc_top * d_tl).astype(BF16)
                m_bot = jnp.concatenate([(sc_bot[:, 0:H] * d_bl).astype(BF16),
                                         (sc_bot[:, H:L] * d_br).astype(BF16)], axis=1)
                p_top = _dot(m_top, rhs[0:H, :])
                p_bot = _dot(m_bot, rhs)
                y_top = p_top if y_top is None else y_top + p_top
                y_bot = p_bot if y_bot is None else y_bot + p_bot
            y_pair = jnp.concatenate([y_top, y_bot], axis=0)
            yacc_ref[:, ps] = (y_pair + y_off[:, pair * LANES:(pair + 1) * LANES]
                               + xp.astype(F32) * dskip_ref[:, ps])

    for g in range(SSD_GROUPS):
        gs = slice(g * GROUP_CH, (g + 1) * GROUP_CH)
        yg = yacc_ref[:, gs] * zs_ref[:, gs].astype(F32)
        y_ref[:, gs] = _rms(yg, normw_ref[:, gs]).astype(y_ref.dtype)


def _ssd(proj, dt_raw, dtb, alog, dskip_e, normw, expand, tril, batch, seq):
    t = proj.shape[0]
    L = SSD_CHUNK
    nc = seq // L
    row = lambda b, c: b * nc + c
    return pl.pallas_call(
        _ssd_kernel,
        out_shape=jax.ShapeDtypeStruct((t, D_INNER), BF16),
        grid=(batch, nc),
        in_specs=[pl.BlockSpec((L, CONV_DIM), lambda b, c: (row(b, c), XBC_OFF // CONV_DIM)),
                  pl.BlockSpec((L, D_INNER), lambda b, c: (row(b, c), Z_OFF // D_INNER)),
                  pl.BlockSpec((L, DT_W), lambda b, c: (row(b, c), 0)),
                  _const_spec((1, DT_W)), _const_spec((1, DT_W)),
                  _const_spec((1, D_INNER)), _const_spec((1, D_INNER)),
                  _const_spec((DT_W, D_INNER)), _const_spec((L, L))],
        out_specs=pl.BlockSpec((L, D_INNER), lambda b, c: (row(b, c), 0)),
        scratch_shapes=[pltpu.VMEM((L, D_INNER), F32),
                        pltpu.VMEM((SSD_GROUPS, SSD_STATE, GROUP_CH), F32)],
        compiler_params=_params(("parallel", "arbitrary")),
        name="ssd",
    )(proj, proj, dt_raw, dtb, alog, dskip_e, normw, expand, tril)


def _attn_kernel(q_ref, k_ref, v_ref, lq1_ref, lk1_ref, lq2_ref, lk2_ref, subln_ref, o_ref,
                 q2_ref, vaug_ref, s0_ref, s1_ref, s2_ref, m_ref, acc_ref, *, tq, tk, rb, lambda_init):
    assert tq == tk and tq % rb == 0
    seq = q_ref.shape[0]
    nq = seq // tq

    lane = lax.broadcasted_iota(jnp.int32, (tq, HEAD_W), 1)
    comp0 = lane < DIFF_HEAD_DIM
    for t in range(nq):
        q = q_ref[t * tq:(t + 1) * tq, :]
        zero = jnp.zeros_like(q)
        q2_ref[t, 0:tq, :] = jnp.where(comp0, q, zero)
        q2_ref[t, tq:2 * tq, :] = jnp.where(comp0, zero, q)
    vaug_ref[:, 0:HEAD_W] = v_ref[...]
    vaug_ref[:, HEAD_W:2 * HEAD_W] = jnp.ones((seq, HEAD_W), BF16)

    lam = (jnp.exp(jnp.sum(lq1_ref[...] * lk1_ref[...], axis=-1, keepdims=True))
           - jnp.exp(jnp.sum(lq2_ref[...] * lk2_ref[...], axis=-1, keepdims=True)) + lambda_init)

    def keys_needed(diag, r):
        return (r * rb) % tq + rb if diag else tk

    def scores_block(qi, j, s_ref, diag, r):
        rows = slice(r * rb, (r + 1) * rb)
        nk = keys_needed(diag, r)
        start = j * tk
        s_ref[rows, 0:nk] = lax.dot_general(q2_ref[qi, rows, :], k_ref[pl.ds(start, nk), :],
                                            (((1,), (1,)), ((), ())), preferred_element_type=F32)

    def softmax_block(j, s_ref, diag, r):
        rows = slice(r * rb, (r + 1) * rb)
        nk = keys_needed(diag, r)
        start = j * tk
        s = s_ref[rows, 0:nk]
        if diag:
            qpos = (r * rb) % tq + lax.broadcasted_iota(jnp.int32, (rb, nk), 0)
            s = jnp.where(lax.broadcasted_iota(jnp.int32, (rb, nk), 1) <= qpos, s, NEG_BIG)
        m_prev = m_ref[rows, :]
        m_next = jnp.maximum(m_prev, jnp.max(s, axis=1, keepdims=True))
        alpha = jnp.exp2(m_prev - m_next)
        p = jnp.exp2(s - jnp.concatenate([m_next] * (nk // LANES), axis=1)).astype(BF16)
        acc_ref[rows, :] = (jnp.concatenate([alpha, alpha], axis=1) * acc_ref[rows, :]
                            + _dot(p, vaug_ref[pl.ds(start, nk), :]))
        m_ref[rows, :] = m_next

    def scores(qi, j, s_ref, diag):
        for r in range(2 * tq // rb):
            scores_block(qi, j, s_ref, diag, r)

    def scores_softmax(nxt, cur):
        for r in range(2 * tq // rb):
            scores_block(*nxt, r)
            softmax_block(*cur, r)

    def finish(qi):
        o = (acc_ref[0:tq, 0:HEAD_W] / acc_ref[0:tq, HEAD_W:2 * HEAD_W]
             - lam * (acc_ref[tq:2 * tq, 0:HEAD_W] / acc_ref[tq:2 * tq, HEAD_W:2 * HEAD_W]))
        o_ref[qi * tq:(qi + 1) * tq, :] = (_rms(o, subln_ref[...]) * (1.0 - lambda_init)).astype(o_ref.dtype)

    slots = (s0_ref, s1_ref)
    scores(0, 0, s2_ref, True)
    for qi in range(nq):
        m_ref[...] = jnp.full(m_ref.shape, NEG_BIG, F32)
        acc_ref[...] = jnp.zeros(acc_ref.shape, F32)
        for j in range(qi):
            nxt = (qi, j + 1, slots[(j + 1) % 2], False) if j + 1 < qi else (qi, qi, s2_ref, True)
            scores_softmax(nxt, (j, slots[j % 2], False))
        if qi + 1 < nq:
            scores_softmax((qi + 1, 0, s0_ref, False), (qi, s2_ref, True))
        else:
            for r in range(2 * tq // rb):
                softmax_block(qi, s2_ref, True, r)
        finish(qi)


def _attn(proj, lq1, lk1, lq2, lk2, subln, batch, seq, lambda_init):
    t = proj.shape[0]
    tq, tk, rb = 512, 512, 256
    nq = seq // tq
    lam_spec = _const_spec((1, DIFF_HEAD_DIM))
    head_cols = lambda off: (lambda b, h: (b, off // HEAD_W + h))
    return pl.pallas_call(
        functools.partial(_attn_kernel, tq=tq, tk=tk, rb=rb, lambda_init=lambda_init),
        out_shape=jax.ShapeDtypeStruct((t, DIFF_V), BF16),
        grid=(batch, DIFF_HEADS),
        in_specs=[pl.BlockSpec((seq, HEAD_W), head_cols(Q_OFF)),
                  pl.BlockSpec((seq, HEAD_W), head_cols(K_OFF)),
                  pl.BlockSpec((seq, HEAD_W), head_cols(V_OFF)),
                  lam_spec, lam_spec, lam_spec, lam_spec, _const_spec((1, HEAD_W))],
        out_specs=pl.BlockSpec((seq, HEAD_W), lambda b, h: (b, h)),
        scratch_shapes=[pltpu.VMEM((nq, 2 * tq, HEAD_W), BF16),
                        pltpu.VMEM((seq, 2 * HEAD_W), BF16),
                        pltpu.VMEM((2 * tq, tk), F32),
                        pltpu.VMEM((2 * tq, tk), F32),
                        pltpu.VMEM((2 * tq, tk), F32),
                        pltpu.VMEM((2 * tq, LANES), F32),
                        pltpu.VMEM((2 * tq, 2 * HEAD_W), F32)],
        compiler_params=_params(("parallel", "parallel")),
        name="diff_attn",
    )(proj, proj, proj, lq1, lk1, lq2, lk2, subln)


def _merge_kernel(y_ref, o_ref, gate_ref, x_ref, mod_ref, wos_ref, woa_ref, wout_ref, nw_ref, out_ref):
    y_ssd = _dot(y_ref[...], wos_ref[...])
    y_attn = _dot(o_ref[...], woa_ref[...])
    g_ssd = gate_ref[:, 0:D_MODEL].astype(F32)
    g_attn = gate_ref[:, D_MODEL:2 * D_MODEL].astype(F32)
    merged = (g_ssd * y_ssd + g_attn * y_attn).astype(BF16)
    mix = _dot(merged, wout_ref[...])
    g1 = mod_ref[0][:, 2 * D_MODEL:3 * D_MODEL]
    out_ref[...] = x_ref[...] + g1 * _rms(mix, nw_ref[...])


def _merge(y, o, proj, x2, mod3, wos, woa, wout, nw, seq):
    t = x2.shape[0]
    tm = 512
    per_b = seq // tm
    return pl.pallas_call(
        _merge_kernel,
        out_shape=jax.ShapeDtypeStruct((t, D_MODEL), F32),
        grid=(t // tm,),
        in_specs=[pl.BlockSpec((tm, D_INNER), lambda i: (i, 0)),
                  pl.BlockSpec((tm, DIFF_V), lambda i: (i, 0)),
                  pl.BlockSpec((tm, 2 * D_MODEL), lambda i: (i, GATE_OFF // (2 * D_MODEL))),
                  pl.BlockSpec((tm, D_MODEL), lambda i: (i, 0)),
                  pl.BlockSpec((1, 1, 6 * D_MODEL), lambda i: (i // per_b, 0, 0)),
                  _const_spec((D_INNER, D_MODEL)), _const_spec((DIFF_V, D_MODEL)),
                  _const_spec((D_MODEL, D_MODEL)), _const_spec((1, D_MODEL))],
        out_specs=pl.BlockSpec((tm, D_MODEL), lambda i: (i, 0)),
        compiler_params=_params(("parallel",)),
        name="merge",
    )(y, o, proj, x2, mod3, wos, woa, wout, nw)


def _ffn_kernel(x_ref, mod_ref, npre_ref, npost_ref, wg_ref, wu_ref, wd_ref, out_ref):
    mod = mod_ref[0]
    x = x_ref[...]
    h = (_rms(x, npre_ref[...]) * (1.0 + mod[:, 4 * D_MODEL:5 * D_MODEL]) + mod[:, 3 * D_MODEL:4 * D_MODEL]).astype(BF16)
    act = (_silu(_dot(h, wg_ref[...])) * _dot(h, wu_ref[...])).astype(BF16)
    f = _dot(act, wd_ref[...])
    out_ref[...] = x + mod[:, 5 * D_MODEL:6 * D_MODEL] * _rms(f, npost_ref[...])


def _ffn(x1, mod3, npre, npost, wg, wu, wd, seq):
    t = x1.shape[0]
    tm = 512
    per_b = seq // tm
    return pl.pallas_call(
        _ffn_kernel,
        out_shape=jax.ShapeDtypeStruct((t, D_MODEL), F32),
        grid=(t // tm,),
        in_specs=[pl.BlockSpec((tm, D_MODEL), lambda i: (i, 0)),
                  pl.BlockSpec((1, 1, 6 * D_MODEL), lambda i: (i // per_b, 0, 0)),
                  _const_spec((1, D_MODEL)), _const_spec((1, D_MODEL)),
                  _const_spec((D_MODEL, D_FF)), _const_spec((D_MODEL, D_FF)), _const_spec((D_FF, D_MODEL))],
        out_specs=pl.BlockSpec((tm, D_MODEL), lambda i: (i, 0)),
        compiler_params=_params(("parallel",)),
        name="ffn",
    )(x1, mod3, npre, npost, wg, wu, wd)


def _layer(x2, mod3, pos2, batch, seq, lambda_init, norm_pre_mix, norm_post_mix, norm_pre_ffn, norm_post_ffn,
           w_in, conv_w, conv_b, dt_bias, a_log, d_skip, ssd_norm, w_o_ssd,
           lambda_q1, lambda_k1, lambda_q2, lambda_k2, subln, w_o_attn, w_out, w_gate, w_up, w_down):
    row = lambda v: v.reshape(1, -1).astype(F32)
    w_z, w_xbc, w_dt, w_q, w_k, w_v, w_g = jnp.split(w_in, IN_SPLITS, axis=1)
    w_wide = jnp.concatenate([w_xbc, w_v, w_z, w_g, w_k, w_q], axis=1).astype(BF16)
    w_dt = jnp.pad(w_dt, ((0, 0), (0, DT_W - SSD_HEADS))).astype(BF16)
    inv_freq = 1.0 / (ROPE_THETA ** (jnp.arange(0, DIFF_HEAD_DIM, 2, dtype=F32) / DIFF_HEAD_DIM))
    invf = jnp.tile(inv_freq, LANES // inv_freq.shape[0]).reshape(1, LANES)
    half_lane = jnp.arange(LANES) % DIFF_HEAD_DIM < DIFF_HEAD_DIM // 2
    sgn = jnp.where(half_lane, -1.0, 1.0).astype(F32).reshape(1, LANES)

    proj, dt_raw = _in_proj(x2, mod3, row(norm_pre_mix), pos2, invf, sgn, w_wide, w_dt,
                            conv_w.T.astype(F32), row(conv_b), seq)

    pad_h = lambda v: jnp.pad(v.astype(F32), (0, DT_W - SSD_HEADS)).reshape(1, DT_W)
    expand = (jnp.arange(DT_W)[:, None] == (jnp.arange(D_INNER)[None, :] // SSD_HEAD_DIM)).astype(BF16)
    tril = (jnp.arange(SSD_CHUNK)[None, :] <= jnp.arange(SSD_CHUNK)[:, None]).astype(BF16)
    y = _ssd(proj, dt_raw, pad_h(dt_bias), pad_h(a_log), row(jnp.repeat(d_skip, SSD_HEAD_DIM)), row(ssd_norm),
             expand, tril, batch, seq)

    o = _attn(proj, row(lambda_q1), row(lambda_k1), row(lambda_q2), row(lambda_k2), row(subln),
              batch, seq, lambda_init)

    x1 = _merge(y, o, proj, x2, mod3, w_o_ssd.astype(BF16), w_o_attn.astype(BF16), w_out.astype(BF16),
                row(norm_post_mix), seq)
    return _ffn(x1, mod3, row(norm_pre_ffn), row(norm_post_ffn), w_gate.astype(BF16), w_up.astype(BF16),
                w_down.astype(BF16), seq)


def kernel(x, c, positions, w_ada, b_ada, norm_pre_mix, norm_post_mix, norm_pre_ffn, norm_post_ffn, w_in, conv_w, conv_b, dt_bias, a_log, d_skip, ssd_norm, w_o_ssd, lambda_q1, lambda_k1, lambda_q2, lambda_k2, subln, w_o_attn, w_out, w_gate, w_up, w_down):
    batch, seq, _ = x.shape
    depth = w_in.shape[0]
    x2 = x.reshape(batch * seq, D_MODEL)
    run = DIFF_HEAD_DIM // 2
    pos2 = jnp.repeat(positions.reshape(-1, LANES // run, INPROJ_TM // 4).astype(F32).transpose(0, 2, 1), run,
                      axis=2).reshape(-1, LANES)
    c_pad = jnp.pad(c, ((0, SUBLANES - batch), (0, 0)))
    for layer in range(depth):
        lambda_init = 0.8 - 0.6 * math.exp(-0.3 * layer)
        mod3 = _ada(c_pad, w_ada[layer], b_ada[layer].reshape(1, -1))[:batch].reshape(batch, 1, 6 * D_MODEL)
        x2 = _layer(x2, mod3, pos2, batch, seq, lambda_init, norm_pre_mix[layer], norm_post_mix[layer],
                    norm_pre_ffn[layer], norm_post_ffn[layer], w_in[layer], conv_w[layer], conv_b[layer],
                    dt_bias[layer], a_log[layer], d_skip[layer], ssd_norm[layer], w_o_ssd[layer],
                    lambda_q1[layer], lambda_k1[layer], lambda_q2[layer], lambda_k2[layer], subln[layer],
                    w_o_attn[layer], w_out[layer], w_gate[layer], w_up[layer], w_down[layer])
    return x2.reshape(batch, seq, D_MODEL)
```

```python
import functools
import math

import jax
import jax.numpy as jnp
import numpy as np
from jax import lax
from jax.experimental import pallas as pl
from jax.experimental.pallas import tpu as pltpu

F32 = jnp.float32
BF16 = jnp.bfloat16

D_MODEL = 1024
D_INNER = 2048
SSD_HEAD_DIM = 64
SSD_HEADS = 32
SSD_GROUPS = 4
SSD_STATE = 128
SSD_CONV = 4
SSD_CHUNK = 256
SSD_CHUNKS_PER_STEP = 2
CONV_DIM = D_INNER + 2 * SSD_GROUPS * SSD_STATE
GROUP_CH = D_INNER // SSD_GROUPS
DIFF_HEADS = 8
DIFF_HEAD_DIM = 64
DIFF_QK = 1024
DIFF_V = 1024
ROPE_THETA = 10000.0
D_FF = 2816
EPS = 1e-6
IN_SIZES = (D_INNER, CONV_DIM, SSD_HEADS, DIFF_QK, DIFF_QK, DIFF_V, 2 * D_MODEL)
IN_SPLITS = tuple(int(v) for v in np.cumsum(IN_SIZES)[:-1])

LANES = 128
SUBLANES = 8
HEAD_W = 2 * DIFF_HEAD_DIM

XBC_OFF, V_OFF, Z_OFF, GATE_OFF, K_OFF, Q_OFF = 0, 3072, 4096, 6144, 8192, 9216
PROJ_W = 10240
PROJ_TN = 1024
DT_W = LANES

VMEM_LIMIT = 56 * 1024 * 1024

LOG2E = math.log2(math.e)
Q_SCALE = DIFF_HEAD_DIM ** -0.5 * LOG2E
NEG_BIG = -1e30


def _sigmoid(v):
    return 1.0 / (1.0 + jnp.exp(-v))


def _silu(v):
    return v * _sigmoid(v)


def _rms(v, w):
    return v * lax.rsqrt(jnp.mean(v * v, axis=-1, keepdims=True) + EPS) * w


def _dot(a, b):
    return jnp.dot(a, b, preferred_element_type=F32)


def _params(sem):
    return pltpu.CompilerParams(dimension_semantics=sem, vmem_limit_bytes=VMEM_LIMIT)


def _const_spec(shape):
    nd = len(shape)
    return pl.BlockSpec(shape, lambda *_: (0,) * nd, pipeline_mode=pl.Buffered(1))


def _ada_kernel(c_ref, w_ref, b_ref, o_ref):
    cond = _silu(c_ref[...])
    o_ref[...] = _dot(cond.astype(BF16), w_ref[...].astype(BF16)) + b_ref[...]


def _ada(c_pad, w_ada, b_ada):
    n = w_ada.shape[1]
    tn = 1024
    return pl.pallas_call(
        _ada_kernel,
        out_shape=jax.ShapeDtypeStruct((c_pad.shape[0], n), F32),
        grid=(n // tn,),
        in_specs=[pl.BlockSpec(c_pad.shape, lambda j: (0, 0)),
                  pl.BlockSpec((D_MODEL, tn), lambda j: (0, j)),
                  pl.BlockSpec((1, tn), lambda j: (0, j))],
        out_specs=pl.BlockSpec((c_pad.shape[0], tn), lambda j: (0, j)),
        compiler_params=_params(("arbitrary",)),
        name="ada",
    )(c_pad, w_ada, b_ada)


COL_KINDS = ("xbc", "xbc", "xbc", "v", "z", "z", "gate", "gate", "k", "q")
N_COL = len(COL_KINDS)
INPROJ_TM = 1024
TRIG_COL = 4
EPI_RB = 128
TRIG_RB = 64
MM_ROWS, MM_COLS = 256, 256
HALO = SUBLANES


def _sigmoid_t(v):
    return 0.5 + 0.5 * jnp.tanh(0.5 * v)


def _silu_t(v):
    h = 0.5 * v
    return h + h * jnp.tanh(h)


def _inproj_kernel(x_ref, mod_ref, nw_ref, posc_ref, invf_ref, sgn_ref, w_ref, wdt_ref, convw_ref, convb_ref,
                   out_ref, dt_ref, h_ref, park0_ref, park1_ref, cos_ref, sin_ref, halo_ref,
                   *, tiles_per_seq, n_row_tiles):
    i, j = pl.program_id(0), pl.program_id(1)
    tm = x_ref.shape[0]
    parks = (park0_ref, park1_ref)

    @pl.when((j == 0) & (i < n_row_tiles))
    def _():
        mod = mod_ref[0]
        h = _rms(x_ref[...], nw_ref[...]) * (1.0 + mod[:, D_MODEL:2 * D_MODEL]) + mod[:, 0:D_MODEL]
        hb = h.astype(BF16)
        h_ref[...] = hb
        dt_ref[...] = _dot(hb, wdt_ref[...])

    def matmul_pieces(col):
        park = parks[col % 2]

        def piece(rows, sl):
            def emit():
                park[rows, sl] = _dot(h_ref[rows, :], w_ref[:, col * PROJ_TN + sl.start:col * PROJ_TN + sl.stop])
            return emit

        return [piece(slice(m * MM_ROWS, (m + 1) * MM_ROWS), slice(n * MM_COLS, (n + 1) * MM_COLS))
                for n in range(PROJ_TN // MM_COLS) for m in range(tm // MM_ROWS)]

    def conv_piece(col, park, blk, slab):
        def emit():
            lo = blk * EPI_RB
            sl = slice(slab * LANES, (slab + 1) * LANES)
            cols = slice(col * PROJ_TN + slab * LANES, col * PROJ_TN + (slab + 1) * LANES)
            w = convw_ref[:, cols]
            b = convb_ref[:, cols]
            lead = jnp.where(first_tile, 0.0, halo_ref[:, cols]) if blk == 0 else park[lo - HALO:lo, sl]
            ext = jnp.concatenate([lead, park[lo:lo + EPI_RB, sl]], axis=0)
            conv = b + w[SSD_CONV - 1:SSD_CONV, :] * ext
            for s in range(1, SSD_CONV):
                conv = conv + w[SSD_CONV - 1 - s:SSD_CONV - s, :] * pltpu.roll(ext, s, 0)
            out_ref[lo:lo + EPI_RB, sl] = _silu_t(conv[HALO:, :]).astype(BF16)
            if blk == tm // EPI_RB - 1:
                halo_ref[:, cols] = park[tm - HALO:tm, sl]
        return emit

    def rope_piece(park, scale, blk, hd):
        def emit():
            lane = lax.broadcasted_iota(jnp.int32, (EPI_RB, HEAD_W), 1)
            first_half = (lane % DIFF_HEAD_DIM) < (DIFF_HEAD_DIM // 2)
            rows = slice(blk * EPI_RB, (blk + 1) * EPI_RB)
            sl = slice(hd * HEAD_W, (hd + 1) * HEAD_W)
            xh = park[rows, sl]
            partner = jnp.where(first_half, pltpu.roll(xh, HEAD_W - DIFF_HEAD_DIM // 2, 1),
                                pltpu.roll(xh, DIFF_HEAD_DIM // 2, 1))
            out_ref[rows, sl] = ((xh * cos_ref[rows, :] + partner * sin_ref[rows, :]) * scale).astype(BF16)
        return emit

    def map_piece(park, fn, blk, slab):
        def emit():
            rows = slice(blk * EPI_RB, (blk + 1) * EPI_RB)
            sl = slice(slab * LANES, (slab + 1) * LANES)
            out_ref[rows, sl] = fn(park[rows, sl]).astype(BF16)
        return emit

    def trig_piece(blk):
        def emit():
            quarter, run = tm // 4, DIFF_HEAD_DIM // 2
            lane = lax.broadcasted_iota(jnp.int32, (TRIG_RB, LANES), 1)
            ang = posc_ref[blk * TRIG_RB:(blk + 1) * TRIG_RB, :] * invf_ref[...]
            for table, ref in ((jnp.cos(ang), cos_ref), (jnp.sin(ang), sin_ref)):
                for g in range(4):
                    base = table if g == 0 else pltpu.roll(table, LANES - g * run, 1)
                    t = jnp.where(lane < run, base, pltpu.roll(base, run, 1))
                    t = jnp.where(lane < 2 * run, t, pltpu.roll(t, 2 * run, 1))
                    rows = slice(g * quarter + blk * TRIG_RB, g * quarter + (blk + 1) * TRIG_RB)
                    ref[rows, :] = t * sgn_ref[...] if ref is sin_ref else t
        return emit

    def epilogue_pieces(col):
        kind, park = COL_KINDS[col], parks[col % 2]
        grid = [(blk, slab) for blk in range(tm // EPI_RB) for slab in range(PROJ_TN // LANES)]
        if kind == "xbc":
            return [conv_piece(col, park, blk, slab) for blk, slab in grid]
        if kind in ("k", "q"):
            return [rope_piece(park, Q_SCALE if kind == "q" else 1.0, blk, hd) for blk, hd in grid]
        fn = {"v": lambda r: r, "z": _silu_t, "gate": _sigmoid_t}[kind]
        return [map_piece(park, fn, blk, slab) for blk, slab in grid]

    def fused(mm_col, ep_col, trig=False):
        mm = matmul_pieces(mm_col) if mm_col is not None else []
        ep = epilogue_pieces(ep_col) if ep_col is not None else []
        if trig:
            ep = ep + [trig_piece(blk) for blk in range(tm // 4 // TRIG_RB)]
        n = max(len(mm), 1)
        for idx in range(n):
            for emit in mm[idx:idx + 1] + ep[idx * len(ep) // n:(idx + 1) * len(ep) // n]:
                emit()

    first_tile = i % tiles_per_seq == 0

    @pl.when((i == 0) & (j == 0))
    def _():
        halo_ref[...] = jnp.zeros(halo_ref.shape, F32)
        fused(0, None)

    @pl.when((j == 0) & (i > 0) & (i < n_row_tiles))
    def _():
        fused(0, N_COL - 1)

    @pl.when((j == 0) & (i == n_row_tiles))
    def _():
        fused(None, N_COL - 1)

    for col in range(1, N_COL):
        @pl.when((j == col) & (i < n_row_tiles))
        def _(col=col):
            fused(col, col - 1, trig=col == TRIG_COL)


def _in_proj(x2, mod3, nw, pos2, invf, sgn, w_wide, w_dt, convw_t, convb, seq):
    t = x2.shape[0]
    tm, tn = INPROJ_TM, PROJ_TN
    n_rows = t // tm
    per_b = seq // tm
    row = lambda i, j: jnp.minimum(i, n_rows - 1)
    lagged = lambda i, j: (jnp.clip(jnp.where(j == 0, i - 1, i), 0, n_rows - 1),
                           jnp.where((j == 0) | (i == n_rows), jnp.where(i == 0, 0, N_COL - 1), j - 1))
    return pl.pallas_call(
        functools.partial(_inproj_kernel, tiles_per_seq=per_b, n_row_tiles=n_rows),
        out_shape=(jax.ShapeDtypeStruct((t, PROJ_W), BF16), jax.ShapeDtypeStruct((t, DT_W), F32)),
        grid=(n_rows + 1, N_COL),
        in_specs=[pl.BlockSpec((tm, D_MODEL), lambda i, j: (row(i, j), 0)),
                  pl.BlockSpec((1, 1, 6 * D_MODEL), lambda i, j: (row(i, j) // per_b, 0, 0)),
                  _const_spec((1, D_MODEL)),
                  pl.BlockSpec((tm // 4, LANES), lambda i, j: (row(i, j), 0)),
                  _const_spec((1, LANES)), _const_spec((1, LANES)),
                  _const_spec((D_MODEL, PROJ_W)),
                  _const_spec((D_MODEL, DT_W)),
                  _const_spec((SSD_CONV, CONV_DIM)),
                  _const_spec((1, CONV_DIM))],
        out_specs=(pl.BlockSpec((tm, tn), lagged),
                   pl.BlockSpec((tm, DT_W), lambda i, j: (row(i, j), 0))),
        scratch_shapes=[pltpu.VMEM((tm, D_MODEL), BF16),
                        pltpu.VMEM((tm, tn), F32),
                        pltpu.VMEM((tm, tn), F32),
                        pltpu.VMEM((tm, LANES), F32),
                        pltpu.VMEM((tm, LANES), F32),
                        pltpu.VMEM((HALO, CONV_DIM), F32)],
        compiler_params=_params(("arbitrary", "arbitrary")),
        name="in_proj",
    )(x2, mod3, nw, pos2, invf, sgn, w_wide, w_dt, convw_t, convb)


def _split3(v):
    hi = v.astype(BF16)
    r = v - hi.astype(F32)
    mid = r.astype(BF16)
    lo = (r - mid.astype(F32)).astype(BF16)
    return hi, mid, lo


def _ssd_kernel(act_ref, zs_ref, dt_ref, dtb_ref, alog_ref, dskip_ref, normw_ref, expand_ref, tril_ref,
                y_ref, yacc_ref, state_ref):
    @pl.when(pl.program_id(1) == 0)
    def _():
        state_ref[...] = jnp.zeros(state_ref.shape, F32)

    for ch in range(act_ref.shape[0] // SSD_CHUNK):
        rows = slice(ch * SSD_CHUNK, (ch + 1) * SSD_CHUNK)
        _ssd_chunk(act_ref.at[rows], zs_ref.at[rows], dt_ref.at[rows], dtb_ref, alog_ref, dskip_ref, normw_ref,
                   expand_ref, tril_ref, y_ref.at[rows], yacc_ref.at[rows], state_ref)


def _ssd_chunk(act_ref, zs_ref, dt_ref, dtb_ref, alog_ref, dskip_ref, normw_ref, expand_ref, tril_ref,
               y_ref, yacc_ref, state_ref):
    L = SSD_CHUNK
    H = L // 2

    dt = dt_ref[...] + dtb_ref[...]
    dt = jnp.maximum(dt, 0.0) + jnp.log1p(jnp.exp(-jnp.abs(dt)))
    a = dt * (-LOG2E * jnp.exp(alog_ref[...]))
    tril = tril_ref[...]
    cs = sum(_dot(tril, p) for p in _split3(a))
    key_t = cs.T - jnp.log2(dt.T)
    cs_last = cs[L - 1:L, :]
    ecs = jnp.exp2(cs).astype(BF16)
    dtw = (dt * jnp.exp2(cs_last - cs)).astype(BF16)
    cd16 = jnp.broadcast_to(jnp.exp2(cs_last), (2 * SUBLANES, DT_W))
    cd_e = sum(_dot(p, expand_ref[...]) for p in _split3(cd16))[0:1, :]

    tri = lax.broadcasted_iota(jnp.int32, (H, H), 1) <= lax.broadcasted_iota(jnp.int32, (H, H), 0)
    first_head = lax.broadcasted_iota(jnp.int32, (L, LANES), 1) < SSD_HEAD_DIM

    for g in range(SSD_GROUPS):
        gs = slice(g * GROUP_CH, (g + 1) * GROUP_CH)
        b_g = act_ref[:, D_INNER + g * SSD_STATE:D_INNER + (g + 1) * SSD_STATE]
        c_g = act_ref[:, D_INNER + (SSD_GROUPS + g) * SSD_STATE:D_INNER + (SSD_GROUPS + g + 1) * SSD_STATE]
        b_gt = b_g.astype(F32).T.astype(BF16)
        sc_top = _dot(c_g[0:H, :], b_gt[:, 0:H])
        sc_bot = _dot(c_g[H:L, :], b_gt)
        st = state_ref[g]
        y_off = _dot(c_g, st.astype(BF16)) * _dot(ecs, expand_ref[:, gs])
        xw = (act_ref[:, gs].astype(F32) * _dot(dtw, expand_ref[:, gs])).astype(BF16)
        state_ref[g] = st * cd_e[:, gs] + _dot(b_gt, xw)
        for pair in range(GROUP_CH // LANES):
            h0 = g * (SSD_HEADS // SSD_GROUPS) + 2 * pair
            ps = slice(g * GROUP_CH + pair * LANES, g * GROUP_CH + (pair + 1) * LANES)
            xp = act_ref[:, ps]
            zero = jnp.zeros_like(xp)
            y_top = y_bot = None
            for sub, rhs in ((0, jnp.where(first_head, xp, zero)), (1, jnp.where(first_head, zero, xp))):
                h = h0 + sub
                col = cs[:, h:h + 1]
                rowv = key_t[h:h + 1, :]
                d_tl = jnp.exp2(jnp.where(tri, col[0:H] - rowv[:, 0:H], NEG_BIG))
                d_bl = jnp.exp2(col[H:L] - rowv[:, 0:H])
                d_br = jnp.exp2(jnp.where(tri, col[H:L] - rowv[:, H:L], NEG_BIG))
                m_top = (sc_top * d_tl).astype(BF16)
                m_bot = jnp.concatenate([(sc_bot[:, 0:H] * d_bl).astype(BF16),
                                         (sc_bot[:, H:L] * d_br).astype(BF16)], axis=1)
                p_top = _dot(m_top, rhs[0:H, :])
                p_bot = _dot(m_bot, rhs)
                y_top = p_top if y_top is None else y_top + p_top
                y_bot = p_bot if y_bot is None else y_bot + p_bot
            y_pair = jnp.concatenate([y_top, y_bot], axis=0)
            yacc_ref[:, ps] = (y_pair + y_off[:, pair * LANES:(pair + 1) * LANES]
                               + xp.astype(F32) * dskip_ref[:, ps])

    for g in range(SSD_GROUPS):
        gs = slice(g * GROUP_CH, (g + 1) * GROUP_CH)
        yg = yacc_ref[:, gs] * zs_ref[:, gs].astype(F32)
        y_ref[:, gs] = _rms(yg, normw_ref[:, gs]).astype(y_ref.dtype)


def _ssd(proj, dt_raw, dtb, alog, dskip_e, normw, expand, tril, batch, seq):
    t = proj.shape[0]
    L = SSD_CHUNK
    R = SSD_CHUNKS_PER_STEP * L
    nc = seq // R
    row = lambda b, c: b * nc + c
    return pl.pallas_call(
        _ssd_kernel,
        out_shape=jax.ShapeDtypeStruct((t, D_INNER), BF16),
        grid=(batch, nc),
        in_specs=[pl.BlockSpec((R, CONV_DIM), lambda b, c: (row(b, c), XBC_OFF // CONV_DIM)),
                  pl.BlockSpec((R, D_INNER), lambda b, c: (row(b, c), Z_OFF // D_INNER)),
                  pl.BlockSpec((R, DT_W), lambda b, c: (row(b, c), 0)),
                  _const_spec((1, DT_W)), _const_spec((1, DT_W)),
                  _const_spec((1, D_INNER)), _const_spec((1, D_INNER)),
                  _const_spec((DT_W, D_INNER)), _const_spec((L, L))],
        out_specs=pl.BlockSpec((R, D_INNER), lambda b, c: (row(b, c), 0)),
        scratch_shapes=[pltpu.VMEM((R, D_INNER), F32),
                        pltpu.VMEM((SSD_GROUPS, SSD_STATE, GROUP_CH), F32)],
        compiler_params=_params(("parallel", "arbitrary")),
        name="ssd",
    )(proj, proj, dt_raw, dtb, alog, dskip_e, normw, expand, tril)


def _attn_kernel(q_ref, k_ref, v_ref, lq1_ref, lk1_ref, lq2_ref, lk2_ref, subln_ref, o_ref,
                 q2_ref, vaug_ref, s0_ref, s1_ref, s2_ref, m_ref, acc_ref, *, tq, tk, rb, lambda_init):
    assert tq == tk and tq % rb == 0
    seq = q_ref.shape[0]
    nq = seq // tq

    lane = lax.broadcasted_iota(jnp.int32, (tq, HEAD_W), 1)
    comp0 = lane < DIFF_HEAD_DIM
    for t in range(nq):
        q = q_ref[t * tq:(t + 1) * tq, :]
        zero = jnp.zeros_like(q)
        q2_ref[t, 0:tq, :] = jnp.where(comp0, q, zero)
        q2_ref[t, tq:2 * tq, :] = jnp.where(comp0, zero, q)
    vaug_ref[:, 0:HEAD_W] = v_ref[...]
    vaug_ref[:, HEAD_W:2 * HEAD_W] = jnp.ones((seq, HEAD_W), BF16)

    lam = (jnp.exp(jnp.sum(lq1_ref[...] * lk1_ref[...], axis=-1, keepdims=True))
           - jnp.exp(jnp.sum(lq2_ref[...] * lk2_ref[...], axis=-1, keepdims=True)) + lambda_init)

    def keys_needed(diag, r):
        return (r * rb) % tq + rb if diag else tk

    def scores_block(qi, j, s_ref, diag, r):
        rows = slice(r * rb, (r + 1) * rb)
        nk = keys_needed(diag, r)
        start = j * tk
        s_ref[rows, 0:nk] = lax.dot_general(q2_ref[qi, rows, :], k_ref[pl.ds(start, nk), :],
                                            (((1,), (1,)), ((), ())), preferred_element_type=F32)

    def softmax_block(j, s_ref, diag, r):
        rows = slice(r * rb, (r + 1) * rb)
        nk = keys_needed(diag, r)
        start = j * tk
        s = s_ref[rows, 0:nk]
        if diag:
            qpos = (r * rb) % tq + lax.broadcasted_iota(jnp.int32, (rb, nk), 0)
            s = jnp.where(lax.broadcasted_iota(jnp.int32, (rb, nk), 1) <= qpos, s, NEG_BIG)
        m_prev = m_ref[rows, :]
        m_next = jnp.maximum(m_prev, jnp.max(s, axis=1, keepdims=True))
        alpha = jnp.exp2(m_prev - m_next)
        p = jnp.exp2(s - jnp.concatenate([m_next] * (nk // LANES), axis=1)).astype(BF16)
        acc_ref[rows, :] = (jnp.concatenate([alpha, alpha], axis=1) * acc_ref[rows, :]
                            + _dot(p, vaug_ref[pl.ds(start, nk), :]))
        m_ref[rows, :] = m_next

    def scores(qi, j, s_ref, diag):
        for r in range(2 * tq // rb):
            scores_block(qi, j, s_ref, diag, r)

    def scores_softmax(nxt, cur):
        for r in range(2 * tq // rb):
            scores_block(*nxt, r)
            softmax_block(*cur, r)

    def finish(qi):
        o = (acc_ref[0:tq, 0:HEAD_W] / acc_ref[0:tq, HEAD_W:2 * HEAD_W]
             - lam * (acc_ref[tq:2 * tq, 0:HEAD_W] / acc_ref[tq:2 * tq, HEAD_W:2 * HEAD_W]))
        o_ref[qi * tq:(qi + 1) * tq, :] = (_rms(o, subln_ref[...]) * (1.0 - lambda_init)).astype(o_ref.dtype)

    slots = (s0_ref, s1_ref)
    scores(0, 0, s2_ref, True)
    for qi in range(nq):
        m_ref[...] = jnp.full(m_ref.shape, NEG_BIG, F32)
        acc_ref[...] = jnp.zeros(acc_ref.shape, F32)
        for j in range(qi):
            nxt = (qi, j + 1, slots[(j + 1) % 2], False) if j + 1 < qi else (qi, qi, s2_ref, True)
            scores_softmax(nxt, (j, slots[j % 2], False))
        if qi + 1 < nq:
            scores_softmax((qi + 1, 0, s0_ref, False), (qi, s2_ref, True))
        else:
            for r in range(2 * tq // rb):
                softmax_block(qi, s2_ref, True, r)
        finish(qi)


def _attn(proj, lq1, lk1, lq2, lk2, subln, batch, seq, lambda_init):
    t = proj.shape[0]
    tq, tk, rb = 512, 512, 256
    nq = seq // tq
    lam_spec = _const_spec((1, DIFF_HEAD_DIM))
    head_cols = lambda off: (lambda b, h: (b, off // HEAD_W + h))
    return pl.pallas_call(
        functools.partial(_attn_kernel, tq=tq, tk=tk, rb=rb, lambda_init=lambda_init),
        out_shape=jax.ShapeDtypeStruct((t, DIFF_V), BF16),
        grid=(batch, DIFF_HEADS),
        in_specs=[pl.BlockSpec((seq, HEAD_W), head_cols(Q_OFF)),
                  pl.BlockSpec((seq, HEAD_W), head_cols(K_OFF)),
                  pl.BlockSpec((seq, HEAD_W), head_cols(V_OFF)),
                  lam_spec, lam_spec, lam_spec, lam_spec, _const_spec((1, HEAD_W))],
        out_specs=pl.BlockSpec((seq, HEAD_W), lambda b, h: (b, h)),
        scratch_shapes=[pltpu.VMEM((nq, 2 * tq, HEAD_W), BF16),
                        pltpu.VMEM((seq, 2 * HEAD_W), BF16),
                        pltpu.VMEM((2 * tq, tk), F32),
                        pltpu.VMEM((2 * tq, tk), F32),
                        pltpu.VMEM((2 * tq, tk), F32),
                        pltpu.VMEM((2 * tq, LANES), F32),
                        pltpu.VMEM((2 * tq, 2 * HEAD_W), F32)],
        compiler_params=_params(("parallel", "parallel")),
        name="diff_attn",
    )(proj, proj, proj, lq1, lk1, lq2, lk2, subln)


def _merge_kernel(y_ref, o_ref, gate_ref, x_ref, mod_ref, wos_ref, woa_ref, wout_ref, nw_ref, out_ref):
    y_ssd = _dot(y_ref[...], wos_ref[...])
    y_attn = _dot(o_ref[...], woa_ref[...])
    g_ssd = gate_ref[:, 0:D_MODEL].astype(F32)
    g_attn = gate_ref[:, D_MODEL:2 * D_MODEL].astype(F32)
    merged = (g_ssd * y_ssd + g_attn * y_attn).astype(BF16)
    mix = _dot(merged, wout_ref[...])
    g1 = mod_ref[0][:, 2 * D_MODEL:3 * D_MODEL]
    out_ref[...] = x_ref[...] + g1 * _rms(mix, nw_ref[...])


def _merge(y, o, proj, x2, mod3, wos, woa, wout, nw, seq):
    t = x2.shape[0]
    tm = 512
    per_b = seq // tm
    return pl.pallas_call(
        _merge_kernel,
        out_shape=jax.ShapeDtypeStruct((t, D_MODEL), F32),
        grid=(t // tm,),
        in_specs=[pl.BlockSpec((tm, D_INNER), lambda i: (i, 0)),
                  pl.BlockSpec((tm, DIFF_V), lambda i: (i, 0)),
                  pl.BlockSpec((tm, 2 * D_MODEL), lambda i: (i, GATE_OFF // (2 * D_MODEL))),
                  pl.BlockSpec((tm, D_MODEL), lambda i: (i, 0)),
                  pl.BlockSpec((1, 1, 6 * D_MODEL), lambda i: (i // per_b, 0, 0)),
                  _const_spec((D_INNER, D_MODEL)), _const_spec((DIFF_V, D_MODEL)),
                  _const_spec((D_MODEL, D_MODEL)), _const_spec((1, D_MODEL))],
        out_specs=pl.BlockSpec((tm, D_MODEL), lambda i: (i, 0)),
        compiler_params=_params(("parallel",)),
        name="merge",
    )(y, o, proj, x2, mod3, wos, woa, wout, nw)


def _ffn_kernel(x_ref, mod_ref, npre_ref, npost_ref, wg_ref, wu_ref, wd_ref, out_ref):
    mod = mod_ref[0]
    x = x_ref[...]
    h = (_rms(x, npre_ref[...]) * (1.0 + mod[:, 4 * D_MODEL:5 * D_MODEL]) + mod[:, 3 * D_MODEL:4 * D_MODEL]).astype(BF16)
    act = (_silu(_dot(h, wg_ref[...])) * _dot(h, wu_ref[...])).astype(BF16)
    f = _dot(act, wd_ref[...])
    out_ref[...] = x + mod[:, 5 * D_MODEL:6 * D_MODEL] * _rms(f, npost_ref[...])


def _ffn(x1, mod3, npre, npost, wg, wu, wd, seq):
    t = x1.shape[0]
    tm = 512
    per_b = seq // tm
    return pl.pallas_call(
        _ffn_kernel,
        out_shape=jax.ShapeDtypeStruct((t, D_MODEL), F32),
        grid=(t // tm,),
        in_specs=[pl.BlockSpec((tm, D_MODEL), lambda i: (i, 0)),
                  pl.BlockSpec((1, 1, 6 * D_MODEL), lambda i: (i // per_b, 0, 0)),
                  _const_spec((1, D_MODEL)), _const_spec((1, D_MODEL)),
                  _const_spec((D_MODEL, D_FF)), _const_spec((D_MODEL, D_FF)), _const_spec((D_FF, D_MODEL))],
        out_specs=pl.BlockSpec((tm, D_MODEL), lambda i: (i, 0)),
        compiler_params=_params(("parallel",)),
        name="ffn",
    )(x1, mod3, npre, npost, wg, wu, wd)


def _layer(x2, mod3, pos2, batch, seq, lambda_init, norm_pre_mix, norm_post_mix, norm_pre_ffn, norm_post_ffn,
           w_in, conv_w, conv_b, dt_bias, a_log, d_skip, ssd_norm, w_o_ssd,
           lambda_q1, lambda_k1, lambda_q2, lambda_k2, subln, w_o_attn, w_out, w_gate, w_up, w_down):
    row = lambda v: v.reshape(1, -1).astype(F32)
    w_z, w_xbc, w_dt, w_q, w_k, w_v, w_g = jnp.split(w_in, IN_SPLITS, axis=1)
    w_wide = jnp.concatenate([w_xbc, w_v, w_z, w_g, w_k, w_q], axis=1).astype(BF16)
    w_dt = jnp.pad(w_dt, ((0, 0), (0, DT_W - SSD_HEADS))).astype(BF16)
    inv_freq = 1.0 / (ROPE_THETA ** (jnp.arange(0, DIFF_HEAD_DIM, 2, dtype=F32) / DIFF_HEAD_DIM))
    invf = jnp.tile(inv_freq, LANES // inv_freq.shape[0]).reshape(1, LANES)
    half_lane = jnp.arange(LANES) % DIFF_HEAD_DIM < DIFF_HEAD_DIM // 2
    sgn = jnp.where(half_lane, -1.0, 1.0).astype(F32).reshape(1, LANES)

    proj, dt_raw = _in_proj(x2, mod3, row(norm_pre_mix), pos2, invf, sgn, w_wide, w_dt,
                            conv_w.T.astype(F32), row(conv_b), seq)

    pad_h = lambda v: jnp.pad(v.astype(F32), (0, DT_W - SSD_HEADS)).reshape(1, DT_W)
    expand = (jnp.arange(DT_W)[:, None] == (jnp.arange(D_INNER)[None, :] // SSD_HEAD_DIM)).astype(BF16)
    tril = (jnp.arange(SSD_CHUNK)[None, :] <= jnp.arange(SSD_CHUNK)[:, None]).astype(BF16)
    y = _ssd(proj, dt_raw, pad_h(dt_bias), pad_h(a_log), row(jnp.repeat(d_skip, SSD_HEAD_DIM)), row(ssd_norm),
             expand, tril, batch, seq)

    o = _attn(proj, row(lambda_q1), row(lambda_k1), row(lambda_q2), row(lambda_k2), row(subln),
              batch, seq, lambda_init)

    x1 = _merge(y, o, proj, x2, mod3, w_o_ssd.astype(BF16), w_o_attn.astype(BF16), w_out.astype(BF16),
                row(norm_post_mix), seq)
    return _ffn(x1, mod3, row(norm_pre_ffn), row(norm_post_ffn), w_gate.astype(BF16), w_up.astype(BF16),
                w_down.astype(BF16), seq)


def kernel(x, c, positions, w_ada, b_ada, norm_pre_mix, norm_post_mix, norm_pre_ffn, norm_post_ffn, w_in, conv_w, conv_b, dt_bias, a_log, d_skip, ssd_norm, w_o_ssd, lambda_q1, lambda_k1, lambda_q2, lambda_k2, subln, w_o_attn, w_out, w_gate, w_up, w_down):
    batch, seq, _ = x.shape
    depth = w_in.shape[0]
    x2 = x.reshape(batch * seq, D_MODEL)
    run = DIFF_HEAD_DIM // 2
    pos2 = jnp.repeat(positions.reshape(-1, LANES // run, INPROJ_TM // 4).astype(F32).transpose(0, 2, 1), run,
                      axis=2).reshape(-1, LANES)
    c_pad = jnp.pad(c, ((0, SUBLANES - batch), (0, 0)))
    for layer in range(depth):
        lambda_init = 0.8 - 0.6 * math.exp(-0.3 * layer)
        mod3 = _ada(c_pad, w_ada[layer], b_ada[layer].reshape(1, -1))[:batch].reshape(batch, 1, 6 * D_MODEL)
        x2 = _layer(x2, mod3, pos2, batch, seq, lambda_init, norm_pre_mix[layer], norm_post_mix[layer],
                    norm_pre_ffn[layer], norm_post_ffn[layer], w_in[layer], conv_w[layer], conv_b[layer],
                    dt_bias[layer], a_log[layer], d_skip[layer], ssd_norm[layer], w_o_ssd[layer],
                    lambda_q1[layer], lambda_k1[layer], lambda_q2[layer], lambda_k2[layer], subln[layer],
                    w_o_attn[layer], w_out[layer], w_gate[layer], w_up[layer], w_down[layer])
    return x2.reshape(batch, seq, D_MODEL)
```

```python
import functools
import math

import jax
import jax.numpy as jnp
import numpy as np
from jax import lax
from jax.experimental import pallas as pl
from jax.experimental.pallas import tpu as pltpu

F32 = jnp.float32
BF16 = jnp.bfloat16

D_MODEL = 1024
D_INNER = 2048
SSD_HEAD_DIM = 64
SSD_HEADS = 32
SSD_GROUPS = 4
SSD_STATE = 128
SSD_CONV = 4
SSD_CHUNK = 256
SSD_CHUNKS_PER_STEP = 4
CONV_DIM = D_INNER + 2 * SSD_GROUPS * SSD_STATE
GROUP_CH = D_INNER // SSD_GROUPS
DIFF_HEADS = 8
DIFF_HEAD_DIM = 64
DIFF_QK = 1024
DIFF_V = 1024
ROPE_THETA = 10000.0
D_FF = 2816
EPS = 1e-6
IN_SIZES = (D_INNER, CONV_DIM, SSD_HEADS, DIFF_QK, DIFF_QK, DIFF_V, 2 * D_MODEL)
IN_SPLITS = tuple(int(v) for v in np.cumsum(IN_SIZES)[:-1])

LANES = 128
SUBLANES = 8
HEAD_W = 2 * DIFF_HEAD_DIM

XBC_OFF, V_OFF, Z_OFF, GATE_OFF, K_OFF, Q_OFF = 0, 3072, 4096, 6144, 8192, 9216
PROJ_W = 10240
PROJ_TN = 1024
DT_W = LANES

VMEM_LIMIT = 56 * 1024 * 1024

LOG2E = math.log2(math.e)
Q_SCALE = DIFF_HEAD_DIM ** -0.5 * LOG2E
NEG_BIG = -1e30


def _sigmoid(v):
    return 1.0 / (1.0 + jnp.exp(-v))


def _silu(v):
    return v * _sigmoid(v)


def _rms(v, w):
    return v * lax.rsqrt(jnp.mean(v * v, axis=-1, keepdims=True) + EPS) * w


def _dot(a, b):
    return jnp.dot(a, b, preferred_element_type=F32)


def _params(sem):
    return pltpu.CompilerParams(dimension_semantics=sem, vmem_limit_bytes=VMEM_LIMIT)


def _const_spec(shape):
    nd = len(shape)
    return pl.BlockSpec(shape, lambda *_: (0,) * nd, pipeline_mode=pl.Buffered(1))


def _ada_kernel(c_ref, w_ref, b_ref, o_ref):
    cond = _silu(c_ref[...])
    o_ref[...] = _dot(cond.astype(BF16), w_ref[...].astype(BF16)) + b_ref[...]


def _ada(c_pad, w_ada, b_ada):
    n = w_ada.shape[1]
    tn = 1024
    return pl.pallas_call(
        _ada_kernel,
        out_shape=jax.ShapeDtypeStruct((c_pad.shape[0], n), F32),
        grid=(n // tn,),
        in_specs=[pl.BlockSpec(c_pad.shape, lambda j: (0, 0)),
                  pl.BlockSpec((D_MODEL, tn), lambda j: (0, j)),
                  pl.BlockSpec((1, tn), lambda j: (0, j))],
        out_specs=pl.BlockSpec((c_pad.shape[0], tn), lambda j: (0, j)),
        compiler_params=_params(("arbitrary",)),
        name="ada",
    )(c_pad, w_ada, b_ada)


COL_KINDS = ("xbc", "xbc", "xbc", "v", "z", "z", "gate", "gate", "k", "q")
N_COL = len(COL_KINDS)
INPROJ_TM = 1024
TRIG_COL = 4
EPI_RB = 128
TRIG_RB = 64
MM_ROWS, MM_COLS = 256, 256
HALO = SUBLANES


def _sigmoid_t(v):
    return 0.5 + 0.5 * jnp.tanh(0.5 * v)


def _silu_t(v):
    h = 0.5 * v
    return h + h * jnp.tanh(h)


def _inproj_kernel(x_ref, mod_ref, nw_ref, posc_ref, invf_ref, sgn_ref, w_ref, wdt_ref, convw_ref, convb_ref,
                   out_ref, dt_ref, h_ref, park0_ref, park1_ref, cos_ref, sin_ref, halo_ref,
                   *, tiles_per_seq, n_row_tiles):
    i, j = pl.program_id(0), pl.program_id(1)
    tm = x_ref.shape[0]
    parks = (park0_ref, park1_ref)

    @pl.when((j == 0) & (i < n_row_tiles))
    def _():
        mod = mod_ref[0]
        h = _rms(x_ref[...], nw_ref[...]) * (1.0 + mod[:, D_MODEL:2 * D_MODEL]) + mod[:, 0:D_MODEL]
        hb = h.astype(BF16)
        h_ref[...] = hb
        dt_ref[...] = _dot(hb, wdt_ref[...])

    def matmul_pieces(col):
        park = parks[col % 2]

        def piece(rows, sl):
            def emit():
                park[rows, sl] = _dot(h_ref[rows, :], w_ref[:, col * PROJ_TN + sl.start:col * PROJ_TN + sl.stop])
            return emit

        return [piece(slice(m * MM_ROWS, (m + 1) * MM_ROWS), slice(n * MM_COLS, (n + 1) * MM_COLS))
                for n in range(PROJ_TN // MM_COLS) for m in range(tm // MM_ROWS)]

    def conv_piece(col, park, blk, slab):
        def emit():
            lo = blk * EPI_RB
            sl = slice(slab * LANES, (slab + 1) * LANES)
            cols = slice(col * PROJ_TN + slab * LANES, col * PROJ_TN + (slab + 1) * LANES)
            w = convw_ref[:, cols]
            b = convb_ref[:, cols]
            lead = jnp.where(first_tile, 0.0, halo_ref[:, cols]) if blk == 0 else park[lo - HALO:lo, sl]
            ext = jnp.concatenate([lead, park[lo:lo + EPI_RB, sl]], axis=0)
            conv = b + w[SSD_CONV - 1:SSD_CONV, :] * ext
            for s in range(1, SSD_CONV):
                conv = conv + w[SSD_CONV - 1 - s:SSD_CONV - s, :] * pltpu.roll(ext, s, 0)
            out_ref[lo:lo + EPI_RB, sl] = _silu_t(conv[HALO:, :]).astype(BF16)
            if blk == tm // EPI_RB - 1:
                halo_ref[:, cols] = park[tm - HALO:tm, sl]
        return emit

    def rope_piece(park, scale, blk, hd):
        def emit():
            lane = lax.broadcasted_iota(jnp.int32, (EPI_RB, HEAD_W), 1)
            first_half = (lane % DIFF_HEAD_DIM) < (DIFF_HEAD_DIM // 2)
            rows = slice(blk * EPI_RB, (blk + 1) * EPI_RB)
            sl = slice(hd * HEAD_W, (hd + 1) * HEAD_W)
            xh = park[rows, sl]
            partner = jnp.where(first_half, pltpu.roll(xh, HEAD_W - DIFF_HEAD_DIM // 2, 1),
                                pltpu.roll(xh, DIFF_HEAD_DIM // 2, 1))
            out_ref[rows, sl] = ((xh * cos_ref[rows, :] + partner * sin_ref[rows, :]) * scale).astype(BF16)
        return emit

    def map_piece(park, fn, blk, slab):
        def emit():
            rows = slice(blk * EPI_RB, (blk + 1) * EPI_RB)
            sl = slice(slab * LANES, (slab + 1) * LANES)
            out_ref[rows, sl] = fn(park[rows, sl]).astype(BF16)
        return emit

    def trig_piece(blk):
        def emit():
            quarter, run = tm // 4, DIFF_HEAD_DIM // 2
            lane = lax.broadcasted_iota(jnp.int32, (TRIG_RB, LANES), 1)
            ang = posc_ref[blk * TRIG_RB:(blk + 1) * TRIG_RB, :] * invf_ref[...]
            for table, ref in ((jnp.cos(ang), cos_ref), (jnp.sin(ang), sin_ref)):
                for g in range(4):
                    base = table if g == 0 else pltpu.roll(table, LANES - g * run, 1)
                    t = jnp.where(lane < run, base, pltpu.roll(base, run, 1))
                    t = jnp.where(lane < 2 * run, t, pltpu.roll(t, 2 * run, 1))
                    rows = slice(g * quarter + blk * TRIG_RB, g * quarter + (blk + 1) * TRIG_RB)
                    ref[rows, :] = t * sgn_ref[...] if ref is sin_ref else t
        return emit

    def epilogue_pieces(col):
        kind, park = COL_KINDS[col], parks[col % 2]
        grid = [(blk, slab) for blk in range(tm // EPI_RB) for slab in range(PROJ_TN // LANES)]
        if kind == "xbc":
            return [conv_piece(col, park, blk, slab) for blk, slab in grid]
        if kind in ("k", "q"):
            return [rope_piece(park, Q_SCALE if kind == "q" else 1.0, blk, hd) for blk, hd in grid]
        fn = {"v": lambda r: r, "z": _silu_t, "gate": _sigmoid_t}[kind]
        return [map_piece(park, fn, blk, slab) for blk, slab in grid]

    def fused(mm_col, ep_col, trig=False):
        mm = matmul_pieces(mm_col) if mm_col is not None else []
        ep = epilogue_pieces(ep_col) if ep_col is not None else []
        if trig:
            ep = ep + [trig_piece(blk) for blk in range(tm // 4 // TRIG_RB)]
        n = max(len(mm), 1)
        for idx in range(n):
            for emit in mm[idx:idx + 1] + ep[idx * len(ep) // n:(idx + 1) * len(ep) // n]:
                emit()

    first_tile = i % tiles_per_seq == 0

    @pl.when((i == 0) & (j == 0))
    def _():
        halo_ref[...] = jnp.zeros(halo_ref.shape, F32)
        fused(0, None)

    @pl.when((j == 0) & (i > 0) & (i < n_row_tiles))
    def _():
        fused(0, N_COL - 1)

    @pl.when((j == 0) & (i == n_row_tiles))
    def _():
        fused(None, N_COL - 1)

    for col in range(1, N_COL):
        @pl.when((j == col) & (i < n_row_tiles))
        def _(col=col):
            fused(col, col - 1, trig=col == TRIG_COL)


def _in_proj(x2, mod3, nw, pos2, invf, sgn, w_wide, w_dt, convw_t, convb, seq):
    t = x2.shape[0]
    tm, tn = INPROJ_TM, PROJ_TN
    n_rows = t // tm
    per_b = seq // tm
    row = lambda i, j: jnp.minimum(i, n_rows - 1)
    lagged = lambda i, j: (jnp.clip(jnp.where(j == 0, i - 1, i), 0, n_rows - 1),
                           jnp.where((j == 0) | (i == n_rows), jnp.where(i == 0, 0, N_COL - 1), j - 1))
    return pl.pallas_call(
        functools.partial(_inproj_kernel, tiles_per_seq=per_b, n_row_tiles=n_rows),
        out_shape=(jax.ShapeDtypeStruct((t, PROJ_W), BF16), jax.ShapeDtypeStruct((t, DT_W), F32)),
        grid=(n_rows + 1, N_COL),
        in_specs=[pl.BlockSpec((tm, D_MODEL), lambda i, j: (row(i, j), 0)),
                  pl.BlockSpec((1, 1, 6 * D_MODEL), lambda i, j: (row(i, j) // per_b, 0, 0)),
                  _const_spec((1, D_MODEL)),
                  pl.BlockSpec((tm // 4, LANES), lambda i, j: (row(i, j), 0)),
                  _const_spec((1, LANES)), _const_spec((1, LANES)),
                  _const_spec((D_MODEL, PROJ_W)),
                  _const_spec((D_MODEL, DT_W)),
                  _const_spec((SSD_CONV, CONV_DIM)),
                  _const_spec((1, CONV_DIM))],
        out_specs=(pl.BlockSpec((tm, tn), lagged),
                   pl.BlockSpec((tm, DT_W), lambda i, j: (row(i, j), 0))),
        scratch_shapes=[pltpu.VMEM((tm, D_MODEL), BF16),
                        pltpu.VMEM((tm, tn), F32),
                        pltpu.VMEM((tm, tn), F32),
                        pltpu.VMEM((tm, LANES), F32),
                        pltpu.VMEM((tm, LANES), F32),
                        pltpu.VMEM((HALO, CONV_DIM), F32)],
        compiler_params=_params(("arbitrary", "arbitrary")),
        name="in_proj",
    )(x2, mod3, nw, pos2, invf, sgn, w_wide, w_dt, convw_t, convb)


def _split3(v):
    hi = v.astype(BF16)
    r = v - hi.astype(F32)
    mid = r.astype(BF16)
    lo = (r - mid.astype(F32)).astype(BF16)
    return hi, mid, lo


def _ssd_kernel(act_ref, zs_ref, dt_ref, dtb_ref, alog_ref, dskip_ref, normw_ref, expand_ref, tril_ref,
                y_ref, yacc_ref, state_ref):
    @pl.when(pl.program_id(1) == 0)
    def _():
        state_ref[...] = jnp.zeros(state_ref.shape, F32)

    for ch in range(act_ref.shape[0] // SSD_CHUNK):
        rows = slice(ch * SSD_CHUNK, (ch + 1) * SSD_CHUNK)
        _ssd_chunk(act_ref.at[rows], zs_ref.at[rows], dt_ref.at[rows], dtb_ref, alog_ref, dskip_ref, normw_ref,
                   expand_ref, tril_ref, y_ref.at[rows], yacc_ref.at[rows], state_ref)


def _ssd_chunk(act_ref, zs_ref, dt_ref, dtb_ref, alog_ref, dskip_ref, normw_ref, expand_ref, tril_ref,
               y_ref, yacc_ref, state_ref):
    L = SSD_CHUNK
    H = L // 2

    dt = dt_ref[...] + dtb_ref[...]
    dt = jnp.maximum(dt, 0.0) + jnp.log1p(jnp.exp(-jnp.abs(dt)))
    a = dt * (-LOG2E * jnp.exp(alog_ref[...]))
    tril = tril_ref[...]
    cs = sum(_dot(tril, p) for p in _split3(a))
    key_t = cs.T - jnp.log2(dt.T)
    cs_last = cs[L - 1:L, :]
    ecs = jnp.exp2(cs).astype(BF16)
    dtw = (dt * jnp.exp2(cs_last - cs)).astype(BF16)
    cd16 = jnp.broadcast_to(jnp.exp2(cs_last), (2 * SUBLANES, DT_W))
    cd_e = sum(_dot(p, expand_ref[...]) for p in _split3(cd16))[0:1, :]

    tri = lax.broadcasted_iota(jnp.int32, (H, H), 1) <= lax.broadcasted_iota(jnp.int32, (H, H), 0)
    first_head = lax.broadcasted_iota(jnp.int32, (L, LANES), 1) < SSD_HEAD_DIM

    for g in range(SSD_GROUPS):
        gs = slice(g * GROUP_CH, (g + 1) * GROUP_CH)
        b_g = act_ref[:, D_INNER + g * SSD_STATE:D_INNER + (g + 1) * SSD_STATE]
        c_g = act_ref[:, D_INNER + (SSD_GROUPS + g) * SSD_STATE:D_INNER + (SSD_GROUPS + g + 1) * SSD_STATE]
        b_gt = b_g.astype(F32).T.astype(BF16)
        sc_top = _dot(c_g[0:H, :], b_gt[:, 0:H])
        sc_bot = _dot(c_g[H:L, :], b_gt)
        st = state_ref[g]
        y_off = _dot(c_g, st.astype(BF16)) * _dot(ecs, expand_ref[:, gs])
        xw = (act_ref[:, gs].astype(F32) * _dot(dtw, expand_ref[:, gs])).astype(BF16)
        state_ref[g] = st * cd_e[:, gs] + _dot(b_gt, xw)
        for pair in range(GROUP_CH // LANES):
            h0 = g * (SSD_HEADS // SSD_GROUPS) + 2 * pair
            ps = slice(g * GROUP_CH + pair * LANES, g * GROUP_CH + (pair + 1) * LANES)
            xp = act_ref[:, ps]
            zero = jnp.zeros_like(xp)
            y_top = y_bot = None
            for sub, rhs in ((0, jnp.where(first_head, xp, zero)), (1, jnp.where(first_head, zero, xp))):
                h = h0 + sub
                col = cs[:, h:h + 1]
                rowv = key_t[h:h + 1, :]
                d_tl = jnp.exp2(jnp.where(tri, col[0:H] - rowv[:, 0:H], NEG_BIG))
                d_bl = jnp.exp2(col[H:L] - rowv[:, 0:H])
                d_br = jnp.exp2(jnp.where(tri, col[H:L] - rowv[:, H:L], NEG_BIG))
                m_top = (sc_top * d_tl).astype(BF16)
                m_bot = jnp.concatenate([(sc_bot[:, 0:H] * d_bl).astype(BF16),
                                         (sc_bot[:, H:L] * d_br).astype(BF16)], axis=1)
                p_top = _dot(m_top, rhs[0:H, :])
                p_bot = _dot(m_bot, rhs)
                y_top = p_top if y_top is None else y_top + p_top
                y_bot = p_bot if y_bot is None else y_bot + p_bot
            y_pair = jnp.concatenate([y_top, y_bot], axis=0)
            yacc_ref[:, ps] = (y_pair + y_off[:, pair * LANES:(pair + 1) * LANES]
                               + xp.astype(F32) * dskip_ref[:, ps])

    for g in range(SSD_GROUPS):
        gs = slice(g * GROUP_CH, (g + 1) * GROUP_CH)
        yg = yacc_ref[:, gs] * zs_ref[:, gs].astype(F32)
        y_ref[:, gs] = _rms(yg, normw_ref[:, gs]).astype(y_ref.dtype)


def _ssd(proj, dt_raw, dtb, alog, dskip_e, normw, expand, tril, batch, seq):
    t = proj.shape[0]
    L = SSD_CHUNK
    R = SSD_CHUNKS_PER_STEP * L
    nc = seq // R
    row = lambda b, c: b * nc + c
    return pl.pallas_call(
        _ssd_kernel,
        out_shape=jax.ShapeDtypeStruct((t, D_INNER), BF16),
        grid=(batch, nc),
        in_specs=[pl.BlockSpec((R, CONV_DIM), lambda b, c: (row(b, c), XBC_OFF // CONV_DIM)),
                  pl.BlockSpec((R, D_INNER), lambda b, c: (row(b, c), Z_OFF // D_INNER)),
                  pl.BlockSpec((R, DT_W), lambda b, c: (row(b, c), 0)),
                  _const_spec((1, DT_W)), _const_spec((1, DT_W)),
                  _const_spec((1, D_INNER)), _const_spec((1, D_INNER)),
                  _const_spec((DT_W, D_INNER)), _const_spec((L, L))],
        out_specs=pl.BlockSpec((R, D_INNER), lambda b, c: (row(b, c), 0)),
        scratch_shapes=[pltpu.VMEM((R, D_INNER), F32),
                        pltpu.VMEM((SSD_GROUPS, SSD_STATE, GROUP_CH), F32)],
        compiler_params=_params(("parallel", "arbitrary")),
        name="ssd",
    )(proj, proj, dt_raw, dtb, alog, dskip_e, normw, expand, tril)


def _attn_kernel(q_ref, k_ref, v_ref, lq1_ref, lk1_ref, lq2_ref, lk2_ref, subln_ref, o_ref,
                 q2_ref, vaug_ref, s0_ref, s1_ref, s2_ref, m_ref, acc_ref, *, tq, tk, rb, lambda_init):
    assert tq == tk and tq % rb == 0
    seq = q_ref.shape[0]
    nq = seq // tq

    lane = lax.broadcasted_iota(jnp.int32, (tq, HEAD_W), 1)
    comp0 = lane < DIFF_HEAD_DIM
    for t in range(nq):
        q = q_ref[t * tq:(t + 1) * tq, :]
        zero = jnp.zeros_like(q)
        q2_ref[t, 0:tq, :] = jnp.where(comp0, q, zero)
        q2_ref[t, tq:2 * tq, :] = jnp.where(comp0, zero, q)
    vaug_ref[:, 0:HEAD_W] = v_ref[...]
    vaug_ref[:, HEAD_W:2 * HEAD_W] = jnp.ones((seq, HEAD_W), BF16)

    lam = (jnp.exp(jnp.sum(lq1_ref[...] * lk1_ref[...], axis=-1, keepdims=True))
           - jnp.exp(jnp.sum(lq2_ref[...] * lk2_ref[...], axis=-1, keepdims=True)) + lambda_init)

    def keys_needed(diag, r):
        return (r * rb) % tq + rb if diag else tk

    def scores_block(qi, j, s_ref, diag, r):
        rows = slice(r * rb, (r + 1) * rb)
        nk = keys_needed(diag, r)
        start = j * tk
        s_ref[rows, 0:nk] = lax.dot_general(q2_ref[qi, rows, :], k_ref[pl.ds(start, nk), :],
                                            (((1,), (1,)), ((), ())), preferred_element_type=F32)

    def softmax_block(j, s_ref, diag, r):
        rows = slice(r * rb, (r + 1) * rb)
        nk = keys_needed(diag, r)
        start = j * tk
        s = s_ref[rows, 0:nk]
        if diag:
            qpos = (r * rb) % tq + lax.broadcasted_iota(jnp.int32, (rb, nk), 0)
            s = jnp.where(lax.broadcasted_iota(jnp.int32, (rb, nk), 1) <= qpos, s, NEG_BIG)
        m_prev = m_ref[rows, :]
        m_next = jnp.maximum(m_prev, jnp.max(s, axis=1, keepdims=True))
        alpha = jnp.exp2(m_prev - m_next)
        p = jnp.exp2(s - jnp.concatenate([m_next] * (nk // LANES), axis=1)).astype(BF16)
        acc_ref[rows, :] = (jnp.concatenate([alpha, alpha], axis=1) * acc_ref[rows, :]
                            + _dot(p, vaug_ref[pl.ds(start, nk), :]))
        m_ref[rows, :] = m_next

    def scores(qi, j, s_ref, diag):
        for r in range(2 * tq // rb):
            scores_block(qi, j, s_ref, diag, r)

    def scores_softmax(nxt, cur):
        for r in range(2 * tq // rb):
            scores_block(*nxt, r)
            softmax_block(*cur, r)

    def finish(qi):
        o = (acc_ref[0:tq, 0:HEAD_W] / acc_ref[0:tq, HEAD_W:2 * HEAD_W]
             - lam * (acc_ref[tq:2 * tq, 0:HEAD_W] / acc_ref[tq:2 * tq, HEAD_W:2 * HEAD_W]))
        o_ref[qi * tq:(qi + 1) * tq, :] = (_rms(o, subln_ref[...]) * (1.0 - lambda_init)).astype(o_ref.dtype)

    slots = (s0_ref, s1_ref)
    scores(0, 0, s2_ref, True)
    for qi in range(nq):
        m_ref[...] = jnp.full(m_ref.shape, NEG_BIG, F32)
        acc_ref[...] = jnp.zeros(acc_ref.shape, F32)
        for j in range(qi):
            nxt = (qi, j + 1, slots[(j + 1) % 2], False) if j + 1 < qi else (qi, qi, s2_ref, True)
            scores_softmax(nxt, (j, slots[j % 2], False))
        if qi + 1 < nq:
            scores_softmax((qi + 1, 0, s0_ref, False), (qi, s2_ref, True))
        else:
            for r in range(2 * tq // rb):
                softmax_block(qi, s2_ref, True, r)
        finish(qi)


def _attn(proj, lq1, lk1, lq2, lk2, subln, batch, seq, lambda_init):
    t = proj.shape[0]
    tq, tk, rb = 512, 512, 256
    nq = seq // tq
    lam_spec = _const_spec((1, DIFF_HEAD_DIM))
    head_cols = lambda off: (lambda b, h: (b, off // HEAD_W + h))
    return pl.pallas_call(
        functools.partial(_attn_kernel, tq=tq, tk=tk, rb=rb, lambda_init=lambda_init),
        out_shape=jax.ShapeDtypeStruct((t, DIFF_V), BF16),
        grid=(batch, DIFF_HEADS),
        in_specs=[pl.BlockSpec((seq, HEAD_W), head_cols(Q_OFF)),
                  pl.BlockSpec((seq, HEAD_W), head_cols(K_OFF)),
                  pl.BlockSpec((seq, HEAD_W), head_cols(V_OFF)),
                  lam_spec, lam_spec, lam_spec, lam_spec, _const_spec((1, HEAD_W))],
        out_specs=pl.BlockSpec((seq, HEAD_W), lambda b, h: (b, h)),
        scratch_shapes=[pltpu.VMEM((nq, 2 * tq, HEAD_W), BF16),
                        pltpu.VMEM((seq, 2 * HEAD_W), BF16),
                        pltpu.VMEM((2 * tq, tk), F32),
                        pltpu.VMEM((2 * tq, tk), F32),
                        pltpu.VMEM((2 * tq, tk), F32),
                        pltpu.VMEM((2 * tq, LANES), F32),
                        pltpu.VMEM((2 * tq, 2 * HEAD_W), F32)],
        compiler_params=_params(("parallel", "parallel")),
        name="diff_attn",
    )(proj, proj, proj, lq1, lk1, lq2, lk2, subln)


def _merge_kernel(y_ref, o_ref, gate_ref, x_ref, mod_ref, wos_ref, woa_ref, wout_ref, nw_ref, out_ref):
    y_ssd = _dot(y_ref[...], wos_ref[...])
    y_attn = _dot(o_ref[...], woa_ref[...])
    g_ssd = gate_ref[:, 0:D_MODEL].astype(F32)
    g_attn = gate_ref[:, D_MODEL:2 * D_MODEL].astype(F32)
    merged = (g_ssd * y_ssd + g_attn * y_attn).astype(BF16)
    mix = _dot(merged, wout_ref[...])
    g1 = mod_ref[0][:, 2 * D_MODEL:3 * D_MODEL]
    out_ref[...] = x_ref[...] + g1 * _rms(mix, nw_ref[...])


def _merge(y, o, proj, x2, mod3, wos, woa, wout, nw, seq):
    t = x2.shape[0]
    tm = 512
    per_b = seq // tm
    return pl.pallas_call(
        _merge_kernel,
        out_shape=jax.ShapeDtypeStruct((t, D_MODEL), F32),
        grid=(t // tm,),
        in_specs=[pl.BlockSpec((tm, D_INNER), lambda i: (i, 0)),
                  pl.BlockSpec((tm, DIFF_V), lambda i: (i, 0)),
                  pl.BlockSpec((tm, 2 * D_MODEL), lambda i: (i, GATE_OFF // (2 * D_MODEL))),
                  pl.BlockSpec((tm, D_MODEL), lambda i: (i, 0)),
                  pl.BlockSpec((1, 1, 6 * D_MODEL), lambda i: (i // per_b, 0, 0)),
                  _const_spec((D_INNER, D_MODEL)), _const_spec((DIFF_V, D_MODEL)),
                  _const_spec((D_MODEL, D_MODEL)), _const_spec((1, D_MODEL))],
        out_specs=pl.BlockSpec((tm, D_MODEL), lambda i: (i, 0)),
        compiler_params=_params(("parallel",)),
        name="merge",
    )(y, o, proj, x2, mod3, wos, woa, wout, nw)


def _ffn_kernel(x_ref, mod_ref, npre_ref, npost_ref, wg_ref, wu_ref, wd_ref, out_ref):
    mod = mod_ref[0]
    x = x_ref[...]
    h = (_rms(x, npre_ref[...]) * (1.0 + mod[:, 4 * D_MODEL:5 * D_MODEL]) + mod[:, 3 * D_MODEL:4 * D_MODEL]).astype(BF16)
    act = (_silu(_dot(h, wg_ref[...])) * _dot(h, wu_ref[...])).astype(BF16)
    f = _dot(act, wd_ref[...])
    out_ref[...] = x + mod[:, 5 * D_MODEL:6 * D_MODEL] * _rms(f, npost_ref[...])


def _ffn(x1, mod3, npre, npost, wg, wu, wd, seq):
    t = x1.shape[0]
    tm = 512
    per_b = seq // tm
    return pl.pallas_call(
        _ffn_kernel,
        out_shape=jax.ShapeDtypeStruct((t, D_MODEL), F32),
        grid=(t // tm,),
        in_specs=[pl.BlockSpec((tm, D_MODEL), lambda i: (i, 0)),
                  pl.BlockSpec((1, 1, 6 * D_MODEL), lambda i: (i // per_b, 0, 0)),
                  _const_spec((1, D_MODEL)), _const_spec((1, D_MODEL)),
                  _const_spec((D_MODEL, D_FF)), _const_spec((D_MODEL, D_FF)), _const_spec((D_FF, D_MODEL))],
        out_specs=pl.BlockSpec((tm, D_MODEL), lambda i: (i, 0)),
        compiler_params=_params(("parallel",)),
        name="ffn",
    )(x1, mod3, npre, npost, wg, wu, wd)


def _layer(x2, mod3, pos2, batch, seq, lambda_init, norm_pre_mix, norm_post_mix, norm_pre_ffn, norm_post_ffn,
           w_in, conv_w, conv_b, dt_bias, a_log, d_skip, ssd_norm, w_o_ssd,
           lambda_q1, lambda_k1, lambda_q2, lambda_k2, subln, w_o_attn, w_out, w_gate, w_up, w_down):
    row = lambda v: v.reshape(1, -1).astype(F32)
    w_z, w_xbc, w_dt, w_q, w_k, w_v, w_g = jnp.split(w_in, IN_SPLITS, axis=1)
    w_wide = jnp.concatenate([w_xbc, w_v, w_z, w_g, w_k, w_q], axis=1).astype(BF16)
    w_dt = jnp.pad(w_dt, ((0, 0), (0, DT_W - SSD_HEADS))).astype(BF16)
    inv_freq = 1.0 / (ROPE_THETA ** (jnp.arange(0, DIFF_HEAD_DIM, 2, dtype=F32) / DIFF_HEAD_DIM))
    invf = jnp.tile(inv_freq, LANES // inv_freq.shape[0]).reshape(1, LANES)
    half_lane = jnp.arange(LANES) % DIFF_HEAD_DIM < DIFF_HEAD_DIM // 2
    sgn = jnp.where(half_lane, -1.0, 1.0).astype(F32).reshape(1, LANES)

    proj, dt_raw = _in_proj(x2, mod3, row(norm_pre_mix), pos2, invf, sgn, w_wide, w_dt,
                            conv_w.T.astype(F32), row(conv_b), seq)

    pad_h = lambda v: jnp.pad(v.astype(F32), (0, DT_W - SSD_HEADS)).reshape(1, DT_W)
    expand = (jnp.arange(DT_W)[:, None] == (jnp.arange(D_INNER)[None, :] // SSD_HEAD_DIM)).astype(BF16)
    tril = (jnp.arange(SSD_CHUNK)[None, :] <= jnp.arange(SSD_CHUNK)[:, None]).astype(BF16)
    y = _ssd(proj, dt_raw, pad_h(dt_bias), pad_h(a_log), row(jnp.repeat(d_skip, SSD_HEAD_DIM)), row(ssd_norm),
             expand, tril, batch, seq)

    o = _attn(proj, row(lambda_q1), row(lambda_k1), row(lambda_q2), row(lambda_k2), row(subln),
              batch, seq, lambda_init)

    x1 = _merge(y, o, proj, x2, mod3, w_o_ssd.astype(BF16), w_o_attn.astype(BF16), w_out.astype(BF16),
                row(norm_post_mix), seq)
    return _ffn(x1, mod3, row(norm_pre_ffn), row(norm_post_ffn), w_gate.astype(BF16), w_up.astype(BF16),
                w_down.astype(BF16), seq)


def kernel(x, c, positions, w_ada, b_ada, norm_pre_mix, norm_post_mix, norm_pre_ffn, norm_post_ffn, w_in, conv_w, conv_b, dt_bias, a_log, d_skip, ssd_norm, w_o_ssd, lambda_q1, lambda_k1, lambda_q2, lambda_k2, subln, w_o_attn, w_out, w_gate, w_up, w_down):
    batch, seq, _ = x.shape
    depth = w_in.shape[0]
    x2 = x.reshape(batch * seq, D_MODEL)
    run = DIFF_HEAD_DIM // 2
    pos2 = jnp.repeat(positions.reshape(-1, LANES // run, INPROJ_TM // 4).astype(F32).transpose(0, 2, 1), run,
                      axis=2).reshape(-1, LANES)
    c_pad = jnp.pad(c, ((0, SUBLANES - batch), (0, 0)))
    for layer in range(depth):
        lambda_init = 0.8 - 0.6 * math.exp(-0.3 * layer)
        mod3 = _ada(c_pad, w_ada[layer], b_ada[layer].reshape(1, -1))[:batch].reshape(batch, 1, 6 * D_MODEL)
        x2 = _layer(x2, mod3, pos2, batch, seq, lambda_init, norm_pre_mix[layer], norm_post_mix[layer],
                    norm_pre_ffn[layer], norm_post_ffn[layer], w_in[layer], conv_w[layer], conv_b[layer],
                    dt_bias[layer], a_log[layer], d_skip[layer], ssd_norm[layer], w_o_ssd[layer],
                    lambda_q1[layer], lambda_k1[layer], lambda_q2[layer], lambda_k2[layer], subln[layer],
                    w_o_attn[layer], w_out[layer], w_gate[layer], w_up[layer], w_down[layer])
    return x2.reshape(batch, seq, D_MODEL)
```
